```python
import math
import jax, jax.numpy as jnp
from jax import lax
import numpy as np

D_MODEL = 1024
BATCH = 8
SEQ = 2048
DEPTH = 4
DEC_BATCH = 128
DEC_SEQ = 8
PAST_LEN = 16384
PAGE_SIZE = 128

N_META = 16
N_EVEN = (DEPTH + 1) // 2
N_ODD = DEPTH // 2
S5_WIDTH = D_MODEL // 2
S5_GROUP = 16
S5_GROUPS = S5_WIDTH // S5_GROUP
S5_STATE = 64
RET_HEADS = 4
RET_DK = D_MODEL // 16
RET_DV = 2 * RET_DK
RET_QK = RET_HEADS * RET_DK
RET_WIDTH = RET_HEADS * RET_DV
AB_IN = S5_WIDTH + 2 * RET_QK + 2 * RET_WIDTH
AB_SPLITS = [S5_WIDTH, S5_WIDTH + RET_QK, S5_WIDTH + 2 * RET_QK, S5_WIDTH + 2 * RET_QK + RET_WIDTH]
AB_OUT = S5_WIDTH + RET_WIDTH
GLA_HEADS = 4
GLA_DK = D_MODEL // (2 * GLA_HEADS)
GLA_DV = D_MODEL // GLA_HEADS
GLA_QK = GLA_HEADS * GLA_DK
GLA_V = GLA_HEADS * GLA_DV
GLA_LOWRANK = 16
GLA_TAU = 16.0
GLA_IN = 2 * GLA_QK + 2 * GLA_V + GLA_LOWRANK
GLA_SPLITS = [GLA_QK, 2 * GLA_QK, 2 * GLA_QK + GLA_V, 2 * GLA_QK + 2 * GLA_V]
CHUNK = 16
D_FF = 128 * ((8 * D_MODEL // 3 + 127) // 128)
EPS = 1e-6
ROPE_BASE = 10000.0

kernel_name = 's5_retention_gla_macaron_hybrid'


def rmsnorm(x, g):
    xf = x.astype(jnp.float32)
    y = xf * lax.rsqrt(jnp.mean(xf * xf, axis=-1, keepdims=True) + EPS)
    return (y * g.astype(jnp.float32)).astype(x.dtype)


def swiglu(x, w_gu, w_down):
    gate, up = jnp.split(x @ w_gu, 2, axis=-1)
    return (jax.nn.silu(gate) * up) @ w_down


def rotary(x, pos):
    half = x.shape[-1] // 2
    inv_freq = 1.0 / (ROPE_BASE ** (jnp.arange(half, dtype=jnp.float32) / half))
    ang = pos.astype(jnp.float32)[:, None] * inv_freq[None, :]
    cos = jnp.cos(ang)[None, :, None, :]
    sin = jnp.sin(ang)[None, :, None, :]
    xf = x.astype(jnp.float32)
    x1, x2 = xf[..., :half], xf[..., half:]
    return jnp.concatenate([x1 * cos - x2 * sin, x1 * sin + x2 * cos], axis=-1).astype(x.dtype)


def chunked_gated_linear_attn(q, k, v, log_a, s0):
    f32 = jnp.float32
    bsz, L, H, _ = q.shape
    dv = v.shape[-1]
    pad = (-L) % CHUNK

    def to_chunks(t):
        t = jnp.pad(t.astype(f32), ((0, 0), (0, pad), (0, 0), (0, 0)))
        n = t.shape[1] // CHUNK
        return t.reshape(bsz, n, CHUNK, H, t.shape[-1]).transpose(1, 0, 2, 3, 4)

    qc, kc, vc, lac = to_chunks(q), to_chunks(k), to_chunks(v), to_chunks(log_a)
    causal = jnp.tril(jnp.ones((CHUNK, CHUNK), dtype=bool))[None, :, :, None, None]

    def step(S, inp):
        qi, ki, vi, lai = inp
        b = jnp.cumsum(lai, axis=1)
        o_inter = jnp.einsum('bthk,bhkv->bthv', qi * jnp.exp(b), S)
        diff = jnp.where(causal, b[:, :, None] - b[:, None, :], -jnp.inf)
        scores = jnp.einsum('bthk,bshk,btshk->bhts', qi, ki, jnp.exp(diff))
        o_intra = jnp.einsum('bhts,bshv->bthv', scores, vi)
        b_last = b[:, -1]
        S_new = jnp.exp(b_last)[..., None] * S + jnp.einsum(
            'bshk,bshv->bhkv', ki * jnp.exp(b_last[:, None] - b), vi)
        return S_new, o_inter + o_intra

    s_final, o = lax.scan(step, s0.astype(f32), (qc, kc, vc, lac))
    o = o.transpose(1, 0, 2, 3, 4).reshape(bsz, -1, H, dv)[:, :L]
    return o.astype(v.dtype), s_final


def _complex_affine_combine(e1, e2):
    a1r, a1i, b1r, b1i = e1
    a2r, a2i, b2r, b2i = e2
    return (a2r * a1r - a2i * a1i, a2r * a1i + a2i * a1r,
            a2r * b1r - a2i * b1i + b2r, a2r * b1i + a2i * b1r + b2i)


def s5_mixer(u, h0, a_re, a_im, log_dt, b_re, b_im, c_re, c_im, d_skip, w_glu):
    f32 = jnp.float32
    bsz, L, _ = u.shape
    uf = u.astype(f32).reshape(bsz, L, S5_GROUPS, S5_GROUP)
    ar, ai = a_re.astype(f32), a_im.astype(f32)
    dt = jnp.exp(log_dt.astype(f32))[:, None]
    mag = jnp.exp(dt * ar)
    abar_re, abar_im = mag * jnp.cos(dt * ai), mag * jnp.sin(dt * ai)
    den = ar * ar + ai * ai
    num_re = abar_re - 1.0
    f_re = (num_re * ar + abar_im * ai) / den
    f_im = (abar_im * ar - num_re * ai) / den
    br, bi = b_re.astype(f32), b_im.astype(f32)
    bbar_re = f_re[..., None] * br - f_im[..., None] * bi
    bbar_im = f_re[..., None] * bi + f_im[..., None] * br
    x_re = jnp.einsum('gpn,blgn->blgp', bbar_re, uf)
    x_im = jnp.einsum('gpn,blgn->blgp', bbar_im, uf)
    if h0 is not None:
        h0_re, h0_im = h0[0].astype(f32), h0[1].astype(f32)
        x_re = x_re.at[:, 0].add(abar_re * h0_re - abar_im * h0_im)
        x_im = x_im.at[:, 0].add(abar_re * h0_im + abar_im * h0_re)
    a_seq_re = jnp.broadcast_to(abar_re[None, None], (1, L, S5_GROUPS, S5_STATE))
    a_seq_im = jnp.broadcast_to(abar_im[None, None], (1, L, S5_GROUPS, S5_STATE))
    _, _, h_re, h_im = lax.associative_scan(
        _complex_affine_combine, (a_seq_re, a_seq_im, x_re, x_im), axis=1)
    y = (jnp.einsum('gnp,blgp->blgn', c_re.astype(f32), h_re)
         - jnp.einsum('gnp,blgp->blgn', c_im.astype(f32), h_im))
    y = y.reshape(bsz, L, S5_WIDTH) + d_skip.astype(f32) * u.astype(f32)
    z = jax.nn.gelu(y).astype(u.dtype)
    out = z * jax.nn.sigmoid(z @ w_glu)
    return out, (h_re[:, -1], h_im[:, -1])


def mixer_ab(h, pos, s5_h0, ret_s0, w_in, w_out, a_re, a_im, log_dt, b_re, b_im, c_re, c_im, d_skip, w_glu):
    f32 = jnp.float32
    bsz, L, _ = h.shape
    u, q, k, v, g = jnp.split(h @ w_in, AB_SPLITS, axis=-1)
    s5_out, s5_state = s5_mixer(u, s5_h0, a_re, a_im, log_dt, b_re, b_im, c_re, c_im, d_skip, w_glu)
    q = rotary(q.reshape(bsz, L, RET_HEADS, RET_DK), pos)
    k = rotary(k.reshape(bsz, L, RET_HEADS, RET_DK), pos) * (RET_DK ** -0.5)
    v = v.reshape(bsz, L, RET_HEADS, RET_DV)
    log_gamma = jnp.log(1.0 - 2.0 ** (-5.0 - jnp.arange(RET_HEADS, dtype=f32)))
    log_a = jnp.broadcast_to(log_gamma[None, None, :, None], (bsz, L, RET_HEADS, RET_DK))
    if ret_s0 is None:
        ret_s0 = jnp.zeros((bsz, RET_HEADS, RET_DK, RET_DV), f32)
    o, ret_state = chunked_gated_linear_attn(q, k, v, log_a, ret_s0)
    of = o.astype(f32)
    mu = jnp.mean(of, axis=-1, keepdims=True)
    var = jnp.mean(jnp.square(of - mu), axis=-1, keepdims=True)
    on = ((of - mu) * lax.rsqrt(var + EPS)).reshape(bsz, L, RET_WIDTH).astype(h.dtype)
    ret_out = on * jax.nn.silu(g)
    mixed = jnp.concatenate([s5_out, ret_out], axis=-1) @ w_out
    return mixed, s5_state, ret_state


def mixer_gla(h, gla_s0, w_in, w_alpha2, b_alpha, norm_g, w_out):
    f32 = jnp.float32
    bsz, L, _ = h.shape
    q, k, v, r, lr = jnp.split(h @ w_in, GLA_SPLITS, axis=-1)
    log_a = jax.nn.log_sigmoid((lr @ w_alpha2 + b_alpha).astype(f32)) / GLA_TAU
    log_a = log_a.reshape(bsz, L, GLA_HEADS, GLA_DK)
    q = q.reshape(bsz, L, GLA_HEADS, GLA_DK)
    k = k.reshape(bsz, L, GLA_HEADS, GLA_DK) * (GLA_DK ** -0.5)
    v = v.reshape(bsz, L, GLA_HEADS, GLA_DV)
    if gla_s0 is None:
        gla_s0 = jnp.zeros((bsz, GLA_HEADS, GLA_DK, GLA_DV), f32)
    o, gla_state = chunked_gated_linear_attn(q, k, v, log_a, gla_s0)
    of = o.astype(f32)
    on = of * lax.rsqrt(jnp.mean(of * of, axis=-1, keepdims=True) + EPS) * norm_g.astype(f32)
    on = on.reshape(bsz, L, GLA_V).astype(h.dtype)
    return (on * jax.nn.silu(r)) @ w_out, gla_state


def run_trunk(x, pos, s5_re0, s5_im0, ret0, gla0, w):
    s5_re_new, s5_im_new, ret_new, gla_new = [], [], [], []
    for layer in range(DEPTH):
        x = x + 0.5 * swiglu(rmsnorm(x, w['norm_ffn1'][layer]), w['ffn1_w_gu'][layer], w['ffn1_w_down'][layer])
        h = rmsnorm(x, w['norm_mix'][layer])
        if layer % 2 == 0:
            i = layer // 2
            h0 = None if s5_re0 is None else (s5_re0[i], s5_im0[i])
            r0 = None if ret0 is None else ret0[i]
            mixed, (hr, hi), rs = mixer_ab(
                h, pos, h0, r0, w['ab_w_in'][i], w['ab_w_out'][i], w['s5_a_re'][i], w['s5_a_im'][i],
                w['s5_log_dt'][i], w['s5_b_re'][i], w['s5_b_im'][i], w['s5_c_re'][i], w['s5_c_im'][i],
                w['s5_d'][i], w['s5_w_glu'][i])
            s5_re_new.append(hr)
            s5_im_new.append(hi)
            ret_new.append(rs)
        else:
            i = layer // 2
            g0 = None if gla0 is None else gla0[i]
            mixed, gs = mixer_gla(h, g0, w['gla_w_in'][i], w['gla_w_alpha2'][i], w['gla_b_alpha'][i],
                                  w['gla_norm'][i], w['gla_w_out'][i])
            gla_new.append(gs)
        x = x + mixed
        x = x + 0.5 * swiglu(rmsnorm(x, w['norm_ffn2'][layer]), w['ffn2_w_gu'][layer], w['ffn2_w_down'][layer])
    y = rmsnorm(x, w['norm_final'])
    return y, jnp.stack(s5_re_new), jnp.stack(s5_im_new), jnp.stack(ret_new), jnp.stack(gla_new)


def setup_inputs(seed: int = 0) -> dict:
    key = jax.random.key(seed)
    ks = jax.random.split(key, 40)
    f32 = jnp.float32

    def nrm(k, shape, scale):
        return scale * jax.random.normal(k, shape, f32)

    n_idx = jnp.arange(S5_STATE, dtype=f32)
    a_re = -0.5 + nrm(ks[10], (N_EVEN, S5_GROUPS, S5_STATE), 0.01)
    a_im = math.pi * n_idx[None, None, :] + nrm(ks[11], (N_EVEN, S5_GROUPS, S5_STATE), 0.01)
    log_dt = jax.random.uniform(ks[12], (N_EVEN, S5_GROUPS), f32, math.log(1e-3), math.log(1e-1))
    return {
        'x_prompt': nrm(ks[0], (BATCH, SEQ, D_MODEL), 1.0),
        'x_sample': nrm(ks[1], (DEC_BATCH, DEC_SEQ, D_MODEL), 1.0),
        'state_s5_re': nrm(ks[2], (N_EVEN, DEC_BATCH, S5_GROUPS, S5_STATE), 0.1),
        'state_s5_im': nrm(ks[3], (N_EVEN, DEC_BATCH, S5_GROUPS, S5_STATE), 0.1),
        'state_ret': nrm(ks[4], (N_EVEN, DEC_BATCH, RET_HEADS, RET_DK, RET_DV), 0.5),
        'state_gla': nrm(ks[5], (N_ODD, DEC_BATCH, GLA_HEADS, GLA_DK, GLA_DV), 0.5),
        'meta_tokens': nrm(ks[6], (N_META, D_MODEL), 1.0),
        'norm_ffn1': 1.0 + nrm(ks[7], (DEPTH, D_MODEL), 0.01),
        'norm_mix': 1.0 + nrm(ks[8], (DEPTH, D_MODEL), 0.01),
        'norm_ffn2': 1.0 + nrm(ks[9], (DEPTH, D_MODEL), 0.01),
        'norm_final': 1.0 + nrm(ks[13], (D_MODEL,), 0.01),
        'ffn1_w_gu': nrm(ks[14], (DEPTH, D_MODEL, 2 * D_FF), D_MODEL ** -0.5),
        'ffn1_w_down': nrm(ks[15], (DEPTH, D_FF, D_MODEL), D_FF ** -0.5),
        'ffn2_w_gu': nrm(ks[16], (DEPTH, D_MODEL, 2 * D_FF), D_MODEL ** -0.5),
        'ffn2_w_down': nrm(ks[17], (DEPTH, D_FF, D_MODEL), D_FF ** -0.5),
        'ab_w_in': nrm(ks[18], (N_EVEN, D_MODEL, AB_IN), D_MODEL ** -0.5),
        'ab_w_out': nrm(ks[19], (N_EVEN, AB_OUT, D_MODEL), AB_OUT ** -0.5),
        's5_a_re': a_re,
        's5_a_im': a_im,
        's5_log_dt': log_dt,
        's5_b_re': nrm(ks[20], (N_EVEN, S5_GROUPS, S5_STATE, S5_GROUP), (2 * S5_GROUP) ** -0.5),
        's5_b_im': nrm(ks[21], (N_EVEN, S5_GROUPS, S5_STATE, S5_GROUP), (2 * S5_GROUP) ** -0.5),
        's5_c_re': nrm(ks[22], (N_EVEN, S5_GROUPS, S5_GROUP, S5_STATE), (2 * S5_STATE) ** -0.5),
        's5_c_im': nrm(ks[23], (N_EVEN, S5_GROUPS, S5_GROUP, S5_STATE), (2 * S5_STATE) ** -0.5),
        's5_d': nrm(ks[24], (N_EVEN, S5_WIDTH), 1.0),
        's5_w_glu': nrm(ks[25], (N_EVEN, S5_WIDTH, S5_WIDTH), S5_WIDTH ** -0.5),
        'gla_w_in': nrm(ks[26], (N_ODD, D_MODEL, GLA_IN), D_MODEL ** -0.5),
        'gla_w_alpha2': nrm(ks[27], (N_ODD, GLA_LOWRANK, GLA_QK), GLA_LOWRANK ** -0.5),
        'gla_b_alpha': nrm(ks[28], (N_ODD, GLA_QK), 0.1),
        'gla_norm': 1.0 + nrm(ks[29], (N_ODD, GLA_DV), 0.01),
        'gla_w_out': nrm(ks[30], (N_ODD, GLA_V, D_MODEL), GLA_V ** -0.5),
    }


def reference(x_prompt, x_sample, state_s5_re, state_s5_im, state_ret, state_gla, meta_tokens,
              norm_ffn1, norm_mix, norm_ffn2, norm_final, ffn1_w_gu, ffn1_w_down, ffn2_w_gu, ffn2_w_down,
              ab_w_in, ab_w_out, s5_a_re, s5_a_im, s5_log_dt, s5_b_re, s5_b_im, s5_c_re, s5_c_im, s5_d,
              s5_w_glu, gla_w_in, gla_w_alpha2, gla_b_alpha, gla_norm, gla_w_out):
    w = dict(norm_ffn1=norm_ffn1, norm_mix=norm_mix, norm_ffn2=norm_ffn2, norm_final=norm_final,
             ffn1_w_gu=ffn1_w_gu, ffn1_w_down=ffn1_w_down, ffn2_w_gu=ffn2_w_gu, ffn2_w_down=ffn2_w_down,
             ab_w_in=ab_w_in, ab_w_out=ab_w_out, s5_a_re=s5_a_re, s5_a_im=s5_a_im, s5_log_dt=s5_log_dt,
             s5_b_re=s5_b_re, s5_b_im=s5_b_im, s5_c_re=s5_c_re, s5_c_im=s5_c_im, s5_d=s5_d,
             s5_w_glu=s5_w_glu, gla_w_in=gla_w_in, gla_w_alpha2=gla_w_alpha2, gla_b_alpha=gla_b_alpha,
             gla_norm=gla_norm, gla_w_out=gla_w_out)
    bsz = x_prompt.shape[0]
    meta = jnp.broadcast_to(meta_tokens.astype(x_prompt.dtype)[None], (bsz, N_META, D_MODEL))
    xp = jnp.concatenate([meta, x_prompt], axis=1)
    pos_p = jnp.arange(N_META + x_prompt.shape[1], dtype=jnp.int32)
    yp, p_s5_re, p_s5_im, p_ret, p_gla = run_trunk(xp, pos_p, None, None, None, None, w)
    y_prompt = yp[:, N_META:]
    pos_s = PAST_LEN + jnp.arange(x_sample.shape[1], dtype=jnp.int32)
    y_sample, s_s5_re, s_s5_im, s_ret, s_gla = run_trunk(
        x_sample, pos_s, state_s5_re, state_s5_im, state_ret, state_gla, w)
    return (y_prompt, y_sample, p_s5_re, p_s5_im, p_ret, p_gla, s_s5_re, s_s5_im, s_ret, s_gla)
```

```python
import functools
import math

import numpy as np
import jax
import jax.numpy as jnp
from jax import lax
from jax.experimental import pallas as pl
from jax.experimental.pallas import tpu as pltpu

F32 = jnp.float32
BF16 = jnp.bfloat16

D_MODEL = 1024
BATCH = 8
SEQ = 2048
DEPTH = 4
DEC_BATCH = 128
DEC_SEQ = 8
PAST_LEN = 16384
N_META = 16
N_EVEN = (DEPTH + 1) // 2
N_ODD = DEPTH // 2
S5_WIDTH = D_MODEL // 2
S5_GROUP = 16
S5_GROUPS = S5_WIDTH // S5_GROUP
S5_STATE = 64
S5_CH = S5_GROUPS * S5_STATE
RET_HEADS = 4
RET_DK = D_MODEL // 16
RET_DV = 2 * RET_DK
RET_QK = RET_HEADS * RET_DK
RET_WIDTH = RET_HEADS * RET_DV
AB_IN = S5_WIDTH + 2 * RET_QK + 2 * RET_WIDTH
AB_OUT = S5_WIDTH + RET_WIDTH
GLA_HEADS = 4
GLA_DK = D_MODEL // (2 * GLA_HEADS)
GLA_DV = D_MODEL // GLA_HEADS
GLA_QK = GLA_HEADS * GLA_DK
GLA_V = GLA_HEADS * GLA_DV
GLA_LOWRANK = 16
GLA_TAU = 16.0
GLA_IN = 2 * GLA_QK + 2 * GLA_V + GLA_LOWRANK
LANES = 128
GLA_IN_PAD = 2 * GLA_QK + 2 * GLA_V + LANES
D_FF = 128 * ((8 * D_MODEL // 3 + 127) // 128)
EPS = 1e-6
ROPE_BASE = 10000.0
NEG_BIG = -1e30

ROWS_PROMPT = BATCH * SEQ
ROWS_SAMPLE = DEC_BATCH * DEC_SEQ
ROW0_SAMPLE = ROWS_PROMPT
ROW0_META = ROWS_PROMPT + ROWS_SAMPLE
ROWS = ROW0_META + N_META

VMEM_LIMIT = 56 * 1024 * 1024
DENSE_TM = 264


def _cparams(sem):
    return pltpu.CompilerParams(dimension_semantics=sem, vmem_limit_bytes=VMEM_LIMIT)


def _dot(a, b):
    return jnp.dot(a, b, preferred_element_type=F32)


def _dot_tn(a, b):
    return lax.dot_general(a, b, (((0,), (0,)), ((), ())), preferred_element_type=F32)


def _dot_nt(a, b):
    return lax.dot_general(a, b, (((1,), (1,)), ((), ())), preferred_element_type=F32)


def _dot_exact01(m_bf, x):
    h1 = x.astype(BF16)
    r1 = x - h1.astype(F32)
    h2 = r1.astype(BF16)
    r2 = r1 - h2.astype(F32)
    h3 = r2.astype(BF16)
    return _dot(m_bf, h1) + _dot(m_bf, h2) + _dot(m_bf, h3)


def _rms(x, g):
    return x * lax.rsqrt(jnp.mean(x * x, axis=-1, keepdims=True) + EPS) * g


def _swiglu_half(x, g, wg, wu, wd):
    h = _rms(x, g).astype(BF16)
    gate = _dot(h, wg)
    up = _dot(h, wu)
    act = (gate * jax.nn.sigmoid(gate) * up).astype(BF16)
    return x + 0.5 * _dot(act, wd)


def _const_spec(shape, index):
    return pl.BlockSpec(shape, lambda *_: index, pipeline_mode=pl.Buffered(1))


def _pre_kernel(x_ref, g1_ref, wg_ref, wu_ref, wd_ref, g2_ref, win_ref, x1_ref, p_ref):
    x1 = _swiglu_half(x_ref[...], g1_ref[...], wg_ref[...], wu_ref[...], wd_ref[...])
    x1_ref[...] = x1
    p_ref[...] = _dot(_rms(x1, g2_ref[...]).astype(BF16), win_ref[...])


def _pre_call(x, g1, w_gu, w_down, g2, w_in, layer, mix_idx):
    n_in = w_in.shape[-1]
    tm = DENSE_TM
    row = lambda i: (i, 0)
    return pl.pallas_call(
        _pre_kernel,
        grid=(ROWS // tm,),
        in_specs=[
            pl.BlockSpec((tm, D_MODEL), row),
            _const_spec((None, 1, D_MODEL), (layer, 0, 0)),
            _const_spec((None, D_MODEL, D_FF), (layer, 0, 0)),
            _const_spec((None, D_MODEL, D_FF), (layer, 0, 1)),
            _const_spec((None, D_FF, D_MODEL), (layer, 0, 0)),
            _const_spec((None, 1, D_MODEL), (layer, 0, 0)),
            _const_spec((None, D_MODEL, n_in), (mix_idx, 0, 0)),
        ],
        out_specs=[pl.BlockSpec((tm, D_MODEL), row), pl.BlockSpec((tm, n_in), row)],
        out_shape=[jax.ShapeDtypeStruct((ROWS, D_MODEL), F32), jax.ShapeDtypeStruct((ROWS, n_in), F32)],
        compiler_params=_cparams(("arbitrary",)),
        name="pre",
    )(x, g1, w_gu, w_gu, w_down, g2, w_in)


def _post_kernel(n_mix, final, *refs):
    x1_ref = refs[0]
    mix_refs = refs[1:1 + n_mix]
    wout_ref, g_ref, wg_ref, wu_ref, wd_ref = refs[1 + n_mix:6 + n_mix]
    rest = refs[6 + n_mix:]
    x2 = x1_ref[...]
    width = AB_OUT // n_mix
    for i, m_ref in enumerate(mix_refs):
        x2 = x2 + _dot(m_ref[...], wout_ref[i * width:(i + 1) * width, :])
    y = _swiglu_half(x2, g_ref[...], wg_ref[...], wu_ref[...], wd_ref[...])
    if final:
        gf_ref, o_ref = rest
        y = _rms(y, gf_ref[...])
    else:
        (o_ref,) = rest
    o_ref[...] = y


def _post_call(x1, mixes, w_out, g, w_gu, w_down, layer, mix_idx, g_final):
    tm = DENSE_TM
    row = lambda i: (i, 0)
    final = g_final is not None
    in_specs = [pl.BlockSpec((tm, D_MODEL), row)]
    in_specs += [pl.BlockSpec((tm, m.shape[1]), row) for m in mixes]
    in_specs += [
        _const_spec((None, AB_OUT, D_MODEL), (mix_idx, 0, 0)),
        _const_spec((None, 1, D_MODEL), (layer, 0, 0)),
        _const_spec((None, D_MODEL, D_FF), (layer, 0, 0)),
        _const_spec((None, D_MODEL, D_FF), (layer, 0, 1)),
        _const_spec((None, D_FF, D_MODEL), (layer, 0, 0)),
    ]
    args = [x1, *mixes, w_out, g, w_gu, w_gu, w_down]
    if final:
        in_specs.append(_const_spec((1, D_MODEL), (0, 0)))
        args.append(g_final)
    return pl.pallas_call(
        functools.partial(_post_kernel, len(mixes), final),
        grid=(ROWS // tm,),
        in_specs=in_specs,
        out_specs=pl.BlockSpec((tm, D_MODEL), row),
        out_shape=jax.ShapeDtypeStruct((ROWS, D_MODEL), F32),
        compiler_params=_cparams(("arbitrary",)),
        name="post",
    )(*args)


S5_HALF_IN = S5_WIDTH // 2
S5_HALF_CH = S5_CH // 2
SUBLANES = 8


def _s5_kernel(chained, tc, u_ref, h0re_ref, h0im_ref, bcat_ref, cre_ref, cim_ref, ast_re_ref, ast_im_ref,
               apw_re_ref, apw_im_ref, d_ref, wglu_ref, out_ref, hre_out, him_out, xre, xim, *carry):
    nb = tc // SUBLANES
    u = u_ref[...]
    ub = u.astype(BF16)
    for hf in range(2):
        xh = _dot(ub[:, hf * S5_HALF_IN:(hf + 1) * S5_HALF_IN], bcat_ref[hf])
        xre[:, hf * S5_HALF_CH:(hf + 1) * S5_HALF_CH] = xh[:, :S5_HALF_CH]
        xim[:, hf * S5_HALF_CH:(hf + 1) * S5_HALF_CH] = xh[:, S5_HALF_CH:]

    sr = xre[...].reshape(nb, SUBLANES, S5_CH)
    si = xim[...].reshape(nb, SUBLANES, S5_CH)
    rowi = lax.broadcasted_iota(jnp.int32, (nb, SUBLANES, S5_CH), 1)
    for step, d in enumerate((1, 2, 4)):
        ar = ast_re_ref[step:step + 1, :][None]
        ai = ast_im_ref[step:step + 1, :][None]
        pr = pltpu.roll(sr, d, 1)
        pi = pltpu.roll(si, d, 1)
        keep = rowi >= d
        sr, si = (sr + jnp.where(keep, ar * pr - ai * pi, 0.0),
                  si + jnp.where(keep, ar * pi + ai * pr, 0.0))
    apr = apw_re_ref[...]
    api = apw_im_ref[...]
    if chained:
        cre, cim = carry
        xre[...] = sr.reshape(tc, S5_CH)
        xim[...] = si.reshape(tc, S5_CH)

        @pl.when(pl.program_id(1) == 0)
        def _():
            cre[...] = h0re_ref[...]
            cim[...] = h0im_ref[...]

        def group(r, c):
            hr, hi = c
            rows = pl.ds(pl.multiple_of(r * SUBLANES, SUBLANES), SUBLANES)
            nr = xre[rows, :] + apr * hr - api * hi
            ni = xim[rows, :] + apr * hi + api * hr
            xre[rows, :] = nr
            xim[rows, :] = ni
            return nr[SUBLANES - 1:, :], ni[SUBLANES - 1:, :]

        hr, hi = lax.fori_loop(0, nb, group, (cre[...], cim[...]), unroll=min(nb, 4))
        cre[...] = hr
        cim[...] = hi
        hre_out[...] = hr
        him_out[...] = hi
    else:
        h0r = h0re_ref[...].reshape(nb, SUBLANES, S5_CH)
        h0i = h0im_ref[...].reshape(nb, SUBLANES, S5_CH)
        fr = (sr + apr[None] * h0r - api[None] * h0i).reshape(tc, S5_CH)
        fi = (si + apr[None] * h0i + api[None] * h0r).reshape(tc, S5_CH)
        xre[...] = fr
        xim[...] = fi
        hre_out[...] = fr
        him_out[...] = fi

    ys = []
    for hf in range(2):
        cols = slice(hf * S5_HALF_CH, (hf + 1) * S5_HALF_CH)
        ys.append(_dot(xre[:, cols].astype(BF16), cre_ref[hf]) + _dot(xim[:, cols].astype(BF16), cim_ref[hf]))
    y = jnp.concatenate(ys, axis=1) + d_ref[...] * u
    z = jax.nn.gelu(y)
    out_ref[...] = (z * jax.nn.sigmoid(_dot(z.astype(BF16), wglu_ref[...]))).astype(out_ref.dtype)


def _s5_call(p, row0, n_seq, seq_len, h0re, h0im, prm, chained, tc):
    n_rows = n_seq * seq_len
    blk0 = row0 // tc
    assert row0 % tc == 0 and n_rows % tc == 0
    consts = [prm["bcat"], prm["cre"], prm["cim"], prm["ast_re"], prm["ast_im"], prm["apw_re"], prm["apw_im"],
              prm["d"], prm["wglu"]]
    const_specs = [_const_spec(c.shape, (0,) * c.ndim) for c in consts]
    scratch = [pltpu.VMEM((tc, S5_CH), F32), pltpu.VMEM((tc, S5_CH), F32)]
    if chained:
        n_chunk = seq_len // tc
        grid = (n_seq, n_chunk)
        u_spec = pl.BlockSpec((tc, S5_WIDTH), lambda b, c: (blk0 + b * n_chunk + c, 0))
        h_specs = [_const_spec((1, S5_CH), (0, 0))] * 2
        out_specs = [pl.BlockSpec((tc, S5_WIDTH), lambda b, c: (b * n_chunk + c, 0)),
                     pl.BlockSpec((None, 1, S5_CH), lambda b, c: (b, 0, 0)),
                     pl.BlockSpec((None, 1, S5_CH), lambda b, c: (b, 0, 0))]
        out_shape = [jax.ShapeDtypeStruct((n_rows, S5_WIDTH), BF16),
                     jax.ShapeDtypeStruct((n_seq, 1, S5_CH), F32), jax.ShapeDtypeStruct((n_seq, 1, S5_CH), F32)]
        scratch += [pltpu.VMEM((1, S5_CH), F32), pltpu.VMEM((1, S5_CH), F32)]
        sem = ("arbitrary", "arbitrary")
    else:
        assert seq_len == SUBLANES
        grid = (n_rows // tc,)
        u_spec = pl.BlockSpec((tc, S5_WIDTH), lambda i: (blk0 + i, 0))
        h_specs = [pl.BlockSpec((tc, S5_CH), lambda i: (i, 0))] * 2
        out_specs = [pl.BlockSpec((tc, S5_WIDTH), lambda i: (i, 0)),
                     pl.BlockSpec((tc, S5_CH), lambda i: (i, 0)), pl.BlockSpec((tc, S5_CH), lambda i: (i, 0))]
        out_shape = [jax.ShapeDtypeStruct((n_rows, S5_WIDTH), BF16),
                     jax.ShapeDtypeStruct((n_rows, S5_CH), F32), jax.ShapeDtypeStruct((n_rows, S5_CH), F32)]
        sem = ("arbitrary",)
    return pl.pallas_call(
        functools.partial(_s5_kernel, chained, tc),
        grid=grid,
        in_specs=[u_spec, *h_specs, *const_specs],
        out_specs=out_specs,
        out_shape=out_shape,
        scratch_shapes=scratch,
        compiler_params=_cparams(sem),
        name="s5",
    )(p, h0re, h0im, *consts)


def _s5_params(a_re, a_im, log_dt, b_re, b_im, c_re, c_im, d_skip, w_glu):
    dt = jnp.exp(log_dt)[:, None]
    mag = jnp.exp(dt * a_re)
    abar_re, abar_im = mag * jnp.cos(dt * a_im), mag * jnp.sin(dt * a_im)
    den = a_re * a_re + a_im * a_im
    num_re = abar_re - 1.0
    f_re = (num_re * a_re + abar_im * a_im) / den
    f_im = (abar_im * a_re - num_re * a_im) / den
    bbar_re = f_re[..., None] * b_re - f_im[..., None] * b_im
    bbar_im = f_re[..., None] * b_im + f_im[..., None] * b_re

    def block_diag_in(w):
        w = w.reshape(2, S5_GROUPS // 2, S5_STATE, S5_GROUP)
        eye = jnp.eye(S5_GROUPS // 2, dtype=F32)
        return jnp.einsum("hgpn,gk->hgnkp", w, eye).reshape(2, S5_HALF_IN, S5_HALF_CH)

    def block_diag_out(w):
        w = w.reshape(2, S5_GROUPS // 2, S5_GROUP, S5_STATE)
        eye = jnp.eye(S5_GROUPS // 2, dtype=F32)
        return jnp.einsum("hgnp,gk->hgpkn", w, eye).reshape(2, S5_HALF_CH, S5_HALF_IN)

    bcat = jnp.concatenate([block_diag_in(bbar_re), block_diag_in(bbar_im)], axis=-1).astype(BF16)
    ar, ai = abar_re.reshape(1, S5_CH), abar_im.reshape(1, S5_CH)
    pows_re, pows_im = [ar], [ai]
    for _ in range(SUBLANES - 1):
        pr, pi = pows_re[-1], pows_im[-1]
        pows_re.append(pr * ar - pi * ai)
        pows_im.append(pr * ai + pi * ar)
    zeros = jnp.zeros((SUBLANES - 3, S5_CH), F32)
    return dict(
        bcat=bcat,
        cre=block_diag_out(c_re).astype(BF16),
        cim=block_diag_out(-c_im).astype(BF16),
        ast_re=jnp.concatenate([pows_re[0], pows_re[1], pows_re[3], zeros], axis=0),
        ast_im=jnp.concatenate([pows_im[0], pows_im[1], pows_im[3], zeros], axis=0),
        apw_re=jnp.concatenate(pows_re, axis=0),
        apw_im=jnp.concatenate(pows_im, axis=0),
        d=d_skip.reshape(1, S5_WIDTH),
        wglu=w_glu.astype(BF16),
    )


def _ret_tables(t):
    gam = np.log(1.0 - 2.0 ** (-5.0 - np.arange(RET_HEADS, dtype=np.float64)))
    tt = np.arange(t, dtype=np.float64)
    dq = np.exp(gam[:, None] * (tt[None, :] + 1.0))
    dq = np.broadcast_to(dq.reshape(RET_HEADS * t, 1), (RET_HEADS * t, RET_DV))
    diff = tt[:, None] - tt[None, :]
    dm = np.where(diff >= 0, np.exp(gam[:, None, None] * np.maximum(diff, 0.0)[None]), 0.0)
    dk = np.exp(gam[:, None] * (t - 1.0 - tt[None, :]))
    dk = np.repeat(dk.T, RET_DK, axis=1)
    ds = np.repeat(np.exp(gam * t), RET_DK)[:, None] * np.ones((1, RET_DV))
    hm = np.repeat(np.eye(RET_HEADS), RET_DK, axis=1)
    f = lambda a: jnp.asarray(np.ascontiguousarray(a), F32)
    return dict(dq=f(dq), dm=f(dm.reshape(RET_HEADS * t, t)), dk=f(dk), ds=f(ds),
                hm=f(np.concatenate([hm, np.zeros((SUBLANES - RET_HEADS, RET_QK))], axis=0)))


def _rope_tables(pos):
    half = RET_DK // 2
    inv_freq = 1.0 / (ROPE_BASE ** (jnp.arange(half, dtype=F32) / half))
    ang = pos.astype(F32)[:, None] * inv_freq[None, :]
    cos, sin = jnp.cos(ang), jnp.sin(ang)
    zero = jnp.zeros_like(sin)
    tile = lambda a, b: jnp.tile(jnp.concatenate([a, b], axis=1), (1, RET_HEADS))
    return tile(cos, cos), tile(-sin, zero), tile(zero, sin)


def _ret_chunk(t, q, k, v, g, s, cos, s_up, s_dn, dq, dm, dk, ds, hm):
    half = RET_DK // 2

    def rope(x):
        return x * cos + pltpu.roll(x, RET_QK - half, 1) * s_up + pltpu.roll(x, half, 1) * s_dn

    qr = rope(q)
    kr = rope(k) * (RET_DK ** -0.5)
    qs = jnp.concatenate([qr * hm[h:h + 1, :] for h in range(RET_HEADS)], axis=0).astype(BF16)
    inter = _dot(qs, s.astype(BF16)) * dq
    prob = (_dot_nt(qs, kr.astype(BF16)) * dm).astype(BF16)
    vb = v.astype(BF16)
    outs = []
    for h in range(RET_HEADS):
        rows = slice(h * t, (h + 1) * t)
        cols = slice(h * RET_DV, (h + 1) * RET_DV)
        o = inter[rows] + _dot(prob[rows], vb[:, cols])
        mu = jnp.mean(o, axis=-1, keepdims=True)
        oc = o - mu
        var = jnp.mean(oc * oc, axis=-1, keepdims=True)
        gh = g[:, cols]
        outs.append(oc * lax.rsqrt(var + EPS) * (gh * jax.nn.sigmoid(gh)))
    kd = kr * dk
    ks = jnp.concatenate([kd * hm[h:h + 1, :] for h in range(RET_HEADS)], axis=0).astype(BF16)
    vs = jnp.concatenate([vb[:, h * RET_DV:(h + 1) * RET_DV] for h in range(RET_HEADS)], axis=0)
    s_new = s * ds + _dot_tn(ks, vs)
    return jnp.concatenate(outs, axis=1), s_new


def _ret_kernel(chained, t, n_sub, q_ref, k_ref, v_ref, g_ref, s0_ref, cos_ref, sup_ref, sdn_ref, dq_ref, dm_ref,
                dk_ref, ds_ref, hm_ref, out_ref, st_ref, *scratch):
    tabs = (dq_ref[...], dm_ref[...], dk_ref[...], ds_ref[...], hm_ref[...])
    if chained:
        (s_scr,) = scratch

        @pl.when(pl.program_id(1) == 0)
        def _():
            s_scr[...] = s0_ref[...]

        o, s_new = _ret_chunk(t, q_ref[...], k_ref[...], v_ref[...], g_ref[...], s_scr[...],
                              cos_ref[...], sup_ref[...], sdn_ref[...], *tabs)
        out_ref[...] = o.astype(out_ref.dtype)
        s_scr[...] = s_new
        st_ref[...] = s_new
    else:
        def one(i, carry):
            rows = pl.ds(pl.multiple_of(i * t, t), t)
            o, s_new = _ret_chunk(t, q_ref[rows, :], k_ref[rows, :], v_ref[rows, :], g_ref[rows, :], s0_ref[i],
                                  cos_ref[...], sup_ref[...], sdn_ref[...], *tabs)
            out_ref[rows, :] = o.astype(out_ref.dtype)
            st_ref[i] = s_new
            return carry

        lax.fori_loop(0, n_sub, one, 0)


def _ret_call(p, row0, n_seq, seq_len, s0, pos0, chained, t, n_sub=1):
    n_rows = n_seq * seq_len
    rows_blk = t if chained else t * n_sub
    blk0 = row0 // rows_blk
    assert row0 % rows_blk == 0
    tabs = _ret_tables(t)
    consts = [tabs[n] for n in ("dq", "dm", "dk", "ds", "hm")]
    const_specs = [_const_spec(c.shape, (0, 0)) for c in consts]
    cos, s_up, s_dn = _rope_tables(pos0 + jnp.arange(seq_len, dtype=jnp.int32))
    sd = RET_QK, RET_DV
    if chained:
        n_chunk = seq_len // t
        grid = (n_seq, n_chunk)
        rmap = lambda cb: (lambda b, c: (blk0 + b * n_chunk + c, cb))
        data_specs = [pl.BlockSpec((t, RET_QK), rmap(2)), pl.BlockSpec((t, RET_QK), rmap(3)),
                      pl.BlockSpec((t, RET_WIDTH), rmap(2)), pl.BlockSpec((t, RET_WIDTH), rmap(3)),
                      _const_spec(sd, (0, 0))]
        rope_specs = [pl.BlockSpec((t, RET_QK), lambda b, c: (c, 0))] * 3
        out_specs = [pl.BlockSpec((t, RET_WIDTH), lambda b, c: (b * n_chunk + c, 0)),
                     pl.BlockSpec((None, *sd), lambda b, c: (b, 0, 0))]
        scratch = [pltpu.VMEM(sd, F32)]
        sem = ("arbitrary", "arbitrary")
    else:
        assert seq_len == t and n_seq % n_sub == 0
        grid = (n_seq // n_sub,)
        rmap = lambda cb: (lambda i: (blk0 + i, cb))
        data_specs = [pl.BlockSpec((rows_blk, RET_QK), rmap(2)), pl.BlockSpec((rows_blk, RET_QK), rmap(3)),
                      pl.BlockSpec((rows_blk, RET_WIDTH), rmap(2)), pl.BlockSpec((rows_blk, RET_WIDTH), rmap(3)),
                      pl.BlockSpec((n_sub, *sd), lambda i: (i, 0, 0))]
        rope_specs = [_const_spec((t, RET_QK), (0, 0))] * 3
        out_specs = [pl.BlockSpec((rows_blk, RET_WIDTH), lambda i: (i, 0)),
                     pl.BlockSpec((n_sub, *sd), lambda i: (i, 0, 0))]
        scratch = []
        sem = ("arbitrary",)
    return pl.pallas_call(
        functools.partial(_ret_kernel, chained, t, n_sub),
        grid=grid,
        in_specs=[*data_specs, *rope_specs, *const_specs],
        out_specs=out_specs,
        out_shape=[jax.ShapeDtypeStruct((n_rows, RET_WIDTH), BF16), jax.ShapeDtypeStruct((n_seq, *sd), F32)],
        scratch_shapes=scratch,
        compiler_params=_cparams(sem),
        name="retention",
    )(p, p, p, p, s0, cos, s_up, s_dn, *consts)


GLA_SEL_ROWS = LANES


def _gla_tables(t, n_blk):
    cg = t * n_blk
    blk = np.arange(cg) // t
    same = blk[:, None] == blk[None, :]
    tri = same & (np.arange(cg)[:, None] >= np.arange(cg)[None, :])
    sel = np.arange(GLA_SEL_ROWS)[:, None] == blk[None, :]
    cum = np.concatenate([tri, same, sel], axis=0).astype(np.float32)
    ones = (np.arange(t * GLA_DK)[:, None] // GLA_DK == np.arange(LANES)[None, :]).astype(np.float32)
    hm = np.repeat(np.eye(GLA_HEADS), GLA_DK, axis=1)
    hm = np.concatenate([hm, np.zeros((SUBLANES - GLA_HEADS, GLA_QK))], axis=0).astype(np.float32)
    return jnp.asarray(cum, BF16), jnp.asarray(ones, BF16), jnp.asarray(hm, F32)


def _gla_kernel(chained, t, n_blk, q_ref, k_ref, v_ref, r_ref, lr_ref, s0_ref, wa_ref, ba_ref, ng_ref, cum_ref,
                ones_ref, hm_ref, out_ref, st_ref, rt_scr, o_scr, *scratch):
    cg = t * n_blk
    nh, dk, dv = GLA_HEADS, GLA_DK, GLA_DV
    q = q_ref[...]
    ksc = k_ref[...] * (dk ** -0.5)
    vb = v_ref[...].astype(BF16)
    hm = hm_ref[...]
    la = jax.nn.log_sigmoid(_dot(lr_ref[...].astype(BF16), wa_ref[...]) + ba_ref[...]) / GLA_TAU
    cums = _dot_exact01(cum_ref[...], la)
    b = cums[:cg]
    bl = cums[cg:2 * cg]
    a_cols = jnp.exp(cums[2 * cg:]).T
    qe = q * jnp.exp(b)
    ke = ksc * jnp.exp(bl - b)

    row_t = lax.broadcasted_iota(jnp.int32, (t, GLA_QK), 0)
    for j in range(n_blk):
        rows = slice(j * t, (j + 1) * t)
        qj, kj, bj = q[rows], ksc[rows], b[rows]
        for s in range(t):
            e = jnp.exp(jnp.where(row_t >= s, bj - bj[s:s + 1, :], NEG_BIG))
            prod = (qj * kj[s:s + 1, :]) * e
            for h in range(nh):
                rt_scr[(j * nh + h) * t:(j * nh + h + 1) * t, s * dk:(s + 1) * dk] = prod[:, h * dk:(h + 1) * dk]
    scores = _dot(rt_scr[...].astype(BF16), ones_ref[...])
    intra = []
    for h in range(nh):
        pieces = []
        for j in range(n_blk):
            piece = scores[(j * nh + h) * t:(j * nh + h + 1) * t, :]
            pieces.append(pltpu.roll(piece, j * t, 1) if j else piece)
        pfull = jnp.concatenate(pieces, axis=0) if n_blk > 1 else pieces[0]
        intra.append(_dot(pfull[:, :cg].astype(BF16), vb[:, h * dv:(h + 1) * dv]))

    if chained:
        (s_scr,) = scratch

        @pl.when(pl.program_id(1) == 0)
        def _():
            s_scr[...] = s0_ref[...]

        state = s_scr[...]
    for j in range(n_blk):
        rows = slice(j * t, (j + 1) * t)
        if not chained:
            state = s0_ref[j]
        qs = jnp.concatenate([qe[rows] * hm[h:h + 1, :] for h in range(nh)], axis=0).astype(BF16)
        inter = _dot(qs, state.astype(BF16))
        for h in range(nh):
            o_scr[rows, h * dv:(h + 1) * dv] = inter[h * t:(h + 1) * t] + intra[h][rows]
        ks = jnp.concatenate([ke[rows] * hm[h:h + 1, :] for h in range(nh)], axis=0).astype(BF16)
        vs = jnp.concatenate([vb[rows, h * dv:(h + 1) * dv] for h in range(nh)], axis=0)
        state = state * a_cols[:, j:j + 1] + _dot_tn(ks, vs)
        if not chained:
            st_ref[j] = state
    if chained:
        s_scr[...] = state
        st_ref[...] = state

    o = o_scr[...]
    r = r_ref[...]
    for h in range(nh):
        cols = slice(h * dv, (h + 1) * dv)
        oh = o[:, cols]
        on = oh * lax.rsqrt(jnp.mean(oh * oh, axis=-1, keepdims=True) + EPS) * ng_ref[...]
        rh = r[:, cols]
        out_ref[:, cols] = (on * (rh * jax.nn.sigmoid(rh))).astype(out_ref.dtype)


def _gla_call(p, row0, n_seq, seq_len, s0, wa, ba, ng, chained, t, n_blk):
    n_rows = n_seq * seq_len
    cg = t * n_blk
    blk0 = row0 // cg
    assert row0 % cg == 0 and cg <= LANES and n_blk <= GLA_SEL_ROWS
    cum, ones, hm = _gla_tables(t, n_blk)
    consts = [wa, ba, ng, cum, ones, hm]
    const_specs = [_const_spec(c.shape, (0,) * c.ndim) for c in consts]
    sd = GLA_QK, GLA_DV
    if chained:
        n_chunk = seq_len // cg
        grid = (n_seq, n_chunk)
        rmap = lambda cb: (lambda b, c: (blk0 + b * n_chunk + c, cb))
        s_spec = _const_spec(sd, (0, 0))
        out_specs = [pl.BlockSpec((cg, GLA_V), lambda b, c: (b * n_chunk + c, 0)),
                     pl.BlockSpec((None, *sd), lambda b, c: (b, 0, 0))]
        scratch = [pltpu.VMEM(sd, F32)]
        sem = ("arbitrary", "arbitrary")
    else:
        assert seq_len == t and n_seq % n_blk == 0
        grid = (n_seq // n_blk,)
        rmap = lambda cb: (lambda i: (blk0 + i, cb))
        s_spec = pl.BlockSpec((n_blk, *sd), lambda i: (i, 0, 0))
        out_specs = [pl.BlockSpec((cg, GLA_V), lambda i: (i, 0)), pl.BlockSpec((n_blk, *sd), lambda i: (i, 0, 0))]
        scratch = []
        sem = ("arbitrary",)
    data_specs = [pl.BlockSpec((cg, GLA_QK), rmap(0)), pl.BlockSpec((cg, GLA_QK), rmap(1)),
                  pl.BlockSpec((cg, GLA_V), rmap(1)), pl.BlockSpec((cg, GLA_V), rmap(2)),
                  pl.BlockSpec((cg, LANES), rmap((2 * GLA_QK + 2 * GLA_V) // LANES)), s_spec]
    return pl.pallas_call(
        functools.partial(_gla_kernel, chained, t, n_blk),
        grid=grid,
        in_specs=[*data_specs, *const_specs],
        out_specs=out_specs,
        out_shape=[jax.ShapeDtypeStruct((n_rows, GLA_V), BF16), jax.ShapeDtypeStruct((n_seq, *sd), F32)],
        scratch_shapes=[pltpu.VMEM((n_blk * GLA_HEADS * t, t * GLA_DK), F32), pltpu.VMEM((cg, GLA_V), F32),
                        *scratch],
        compiler_params=_cparams(sem),
        name="gla",
    )(p, p, p, p, p, s0, *consts)


S5_TC_PROMPT = 256
S5_TC_SAMPLE = 512
RET_T_PROMPT = 256
RET_SUB_SAMPLE = 16
GLA_T_PROMPT = 16
GLA_BLK_PROMPT = 8
GLA_BLK_SAMPLE = 8


def _mixer_ab(p, i, prm, st_s5_re, st_s5_im, st_ret):
    zeros_h = jnp.zeros((1, S5_CH), F32)
    a_m, hre_m, him_m = _s5_call(p, ROW0_META, 1, N_META, zeros_h, zeros_h, prm, True, N_META)
    a_p, hre_p, him_p = _s5_call(p, 0, BATCH, SEQ, hre_m[0], him_m[0], prm, True, S5_TC_PROMPT)
    h0re = jnp.repeat(st_s5_re[i].reshape(DEC_BATCH, S5_CH), DEC_SEQ, axis=0)
    h0im = jnp.repeat(st_s5_im[i].reshape(DEC_BATCH, S5_CH), DEC_SEQ, axis=0)
    a_s, hre_s, him_s = _s5_call(p, ROW0_SAMPLE, DEC_BATCH, DEC_SEQ, h0re, h0im, prm, False, S5_TC_SAMPLE)
    last = lambda h: h.reshape(DEC_BATCH, DEC_SEQ, S5_GROUPS, S5_STATE)[:, DEC_SEQ - 1]

    zeros_s = jnp.zeros((RET_QK, RET_DV), F32)
    b_m, s_m = _ret_call(p, ROW0_META, 1, N_META, zeros_s, 0, True, N_META)
    b_p, s_p = _ret_call(p, 0, BATCH, SEQ, s_m[0], N_META, True, RET_T_PROMPT)
    b_s, s_s = _ret_call(p, ROW0_SAMPLE, DEC_BATCH, DEC_SEQ, st_ret[i].reshape(DEC_BATCH, RET_QK, RET_DV),
                         PAST_LEN, False, DEC_SEQ, RET_SUB_SAMPLE)
    mix_a = jnp.concatenate([a_p, a_s, a_m], axis=0)
    mix_b = jnp.concatenate([b_p, b_s, b_m], axis=0)
    states = dict(
        p_s5_re=hre_p.reshape(BATCH, S5_GROUPS, S5_STATE), p_s5_im=him_p.reshape(BATCH, S5_GROUPS, S5_STATE),
        p_ret=s_p.reshape(BATCH, RET_HEADS, RET_DK, RET_DV),
        s_s5_re=last(hre_s), s_s5_im=last(him_s), s_ret=s_s.reshape(DEC_BATCH, RET_HEADS, RET_DK, RET_DV))
    return [mix_a, mix_b], states


def _mixer_gla(p, i, wa, ba, ng, st_gla):
    zeros_s = jnp.zeros((GLA_QK, GLA_DV), F32)
    o_m, s_m = _gla_call(p, ROW0_META, 1, N_META, zeros_s, wa, ba, ng, True, N_META, 1)
    o_p, s_p = _gla_call(p, 0, BATCH, SEQ, s_m[0], wa, ba, ng, True, GLA_T_PROMPT, GLA_BLK_PROMPT)
    o_s, s_s = _gla_call(p, ROW0_SAMPLE, DEC_BATCH, DEC_SEQ, st_gla[i].reshape(DEC_BATCH, GLA_QK, GLA_DV),
                         wa, ba, ng, False, DEC_SEQ, GLA_BLK_SAMPLE)
    mix = jnp.concatenate([o_p, o_s, o_m], axis=0)
    states = dict(p_gla=s_p.reshape(BATCH, GLA_HEADS, GLA_DK, GLA_DV),
                  s_gla=s_s.reshape(DEC_BATCH, GLA_HEADS, GLA_DK, GLA_DV))
    return [mix], states


def kernel(x_prompt, x_sample, state_s5_re, state_s5_im, state_ret, state_gla, meta_tokens, norm_ffn1, norm_mix,
           norm_ffn2, norm_final, ffn1_w_gu, ffn1_w_down, ffn2_w_gu, ffn2_w_down, ab_w_in, ab_w_out, s5_a_re,
           s5_a_im, s5_log_dt, s5_b_re, s5_b_im, s5_c_re, s5_c_im, s5_d, s5_w_glu, gla_w_in, gla_w_alpha2,
           gla_b_alpha, gla_norm, gla_w_out):
    bf = lambda w: w.astype(BF16)
    ffn1_gu, ffn1_dn, ffn2_gu, ffn2_dn = bf(ffn1_w_gu), bf(ffn1_w_down), bf(ffn2_w_gu), bf(ffn2_w_down)
    ab_in, ab_out, gla_out = bf(ab_w_in), bf(ab_w_out), bf(gla_w_out)
    gla_in = bf(jnp.pad(gla_w_in, ((0, 0), (0, 0), (0, GLA_IN_PAD - GLA_IN))))
    gla_wa = bf(jnp.pad(gla_w_alpha2, ((0, 0), (0, LANES - GLA_LOWRANK), (0, 0))))
    n1 = norm_ffn1.reshape(DEPTH, 1, D_MODEL)
    nm = norm_mix.reshape(DEPTH, 1, D_MODEL)
    n2 = norm_ffn2.reshape(DEPTH, 1, D_MODEL)

    x = jnp.concatenate([x_prompt.reshape(ROWS_PROMPT, D_MODEL), x_sample.reshape(ROWS_SAMPLE, D_MODEL),
                         meta_tokens.astype(x_prompt.dtype)], axis=0)
    collected = {}
    for layer in range(DEPTH):
        i = layer // 2
        last = layer == DEPTH - 1
        g_final = norm_final.reshape(1, D_MODEL) if last else None
        if layer % 2 == 0:
            x1, p = _pre_call(x, n1, ffn1_gu, ffn1_dn, nm, ab_in, layer, i)
            prm = _s5_params(s5_a_re[i], s5_a_im[i], s5_log_dt[i], s5_b_re[i], s5_b_im[i], s5_c_re[i],
                             s5_c_im[i], s5_d[i], s5_w_glu[i])
            mixes, states = _mixer_ab(p, i, prm, state_s5_re, state_s5_im, state_ret)
            x = _post_call(x1, mixes, ab_out, n2, ffn2_gu, ffn2_dn, layer, i, g_final)
        else:
            x1, p = _pre_call(x, n1, ffn1_gu, ffn1_dn, nm, gla_in, layer, i)
            mixes, states = _mixer_gla(p, i, gla_wa[i], gla_b_alpha[i].reshape(1, GLA_QK),
                                       gla_norm[i].reshape(1, GLA_DV), state_gla)
            x = _post_call(x1, mixes, gla_out, n2, ffn2_gu, ffn2_dn, layer, i, g_final)
        for name, val in states.items():
            collected.setdefault(name, []).append(val)
    out = {name: jnp.stack(vals) for name, vals in collected.items()}
    y_prompt = x[:ROWS_PROMPT].reshape(BATCH, SEQ, D_MODEL)
    y_sample = x[ROW0_SAMPLE:ROW0_META].reshape(DEC_BATCH, DEC_SEQ, D_MODEL)
    return (y_prompt, y_sample, out["p_s5_re"], out["p_s5_im"], out["p_ret"], out["p_gla"],
            out["s_s5_re"], out["s_s5_im"], out["s_ret"], out["s_gla"])
```

```python
import functools
import math

import numpy as np
import jax
import jax.numpy as jnp
from jax import lax
from jax.experimental import pallas as pl
from jax.experimental.pallas import tpu as pltpu

F32 = jnp.float32
BF16 = jnp.bfloat16

D_MODEL = 1024
BATCH = 8
SEQ = 2048
DEPTH = 4
DEC_BATCH = 128
DEC_SEQ = 8
PAST_LEN = 16384
N_META = 16
N_EVEN = (DEPTH + 1) // 2
N_ODD = DEPTH // 2
S5_WIDTH = D_MODEL // 2
S5_GROUP = 16
S5_GROUPS = S5_WIDTH // S5_GROUP
S5_STATE = 64
S5_CH = S5_GROUPS * S5_STATE
RET_HEADS = 4
RET_DK = D_MODEL // 16
RET_DV = 2 * RET_DK
RET_QK = RET_HEADS * RET_DK
RET_WIDTH = RET_HEADS * RET_DV
AB_IN = S5_WIDTH + 2 * RET_QK + 2 * RET_WIDTH
AB_OUT = S5_WIDTH + RET_WIDTH
GLA_HEADS = 4
GLA_DK = D_MODEL // (2 * GLA_HEADS)
GLA_DV = D_MODEL // GLA_HEADS
GLA_QK = GLA_HEADS * GLA_DK
GLA_V = GLA_HEADS * GLA_DV
GLA_LOWRANK = 16
GLA_TAU = 16.0
GLA_IN = 2 * GLA_QK + 2 * GLA_V + GLA_LOWRANK
LANES = 128
GLA_IN_PAD = 2 * GLA_QK + 2 * GLA_V + LANES
D_FF = 128 * ((8 * D_MODEL // 3 + 127) // 128)
EPS = 1e-6
ROPE_BASE = 10000.0
NEG_BIG = -1e30

ROWS_PROMPT = BATCH * SEQ
ROWS_SAMPLE = DEC_BATCH * DEC_SEQ
ROW0_SAMPLE = ROWS_PROMPT
ROW0_META = ROWS_PROMPT + ROWS_SAMPLE
ROWS = ROW0_META + N_META

VMEM_LIMIT = 56 * 1024 * 1024
DENSE_TM = 256
TILES_PROMPT = ROWS_PROMPT // DENSE_TM
TILES_SAMPLE = ROWS_SAMPLE // DENSE_TM
N_TILES = TILES_PROMPT + TILES_SAMPLE + 1
ROWS_PAD = N_TILES * DENSE_TM
ROWS_REST = ROWS_PAD - ROWS_PROMPT


def _cparams(sem):
    return pltpu.CompilerParams(dimension_semantics=sem, vmem_limit_bytes=VMEM_LIMIT)


def _dot(a, b):
    return jnp.dot(a, b, preferred_element_type=F32)


def _dot_tn(a, b):
    return lax.dot_general(a, b, (((0,), (0,)), ((), ())), preferred_element_type=F32)


def _dot_nt(a, b):
    return lax.dot_general(a, b, (((1,), (1,)), ((), ())), preferred_element_type=F32)


def _dot_exact01(m_bf, x):
    h1 = x.astype(BF16)
    r1 = x - h1.astype(F32)
    h2 = r1.astype(BF16)
    r2 = r1 - h2.astype(F32)
    h3 = r2.astype(BF16)
    return _dot(m_bf, h1) + _dot(m_bf, h2) + _dot(m_bf, h3)


def _rms(x, g):
    return x * lax.rsqrt(jnp.mean(x * x, axis=-1, keepdims=True) + EPS) * g


def _swiglu_half(x, g, wg, wu, wd):
    h = _rms(x, g).astype(BF16)
    gate = _dot(h, wg)
    up = _dot(h, wu)
    act = (gate * jax.nn.sigmoid(gate) * up).astype(BF16)
    return x + 0.5 * _dot(act, wd)


def _store_rows(out_ref, val):
    rows = val.shape[0]
    out_ref[0:rows, :] = val.astype(out_ref.dtype)
    if out_ref.shape[0] > rows:
        out_ref[rows:, :] = jnp.zeros((out_ref.shape[0] - rows, out_ref.shape[1]), out_ref.dtype)


def _const_spec(shape, index):
    return pl.BlockSpec(shape, lambda *_: index, pipeline_mode=pl.Buffered(1))


def _group_specs(width):
    tm = DENSE_TM
    return [pl.BlockSpec((tm, width), lambda i: (jnp.minimum(i, TILES_PROMPT - 1), 0)),
            pl.BlockSpec((tm, width), lambda i: (jnp.clip(i - TILES_PROMPT, 0, TILES_SAMPLE - 1), 0)),
            pl.BlockSpec((tm, width), lambda i: (0, 0))]


def _pick_group(refs):
    if len(refs) == 1:
        return refs[0][...]
    i = pl.program_id(0)
    return jnp.where(i < TILES_PROMPT, refs[0][...],
                     jnp.where(i < TILES_PROMPT + TILES_SAMPLE, refs[1][...], refs[2][...]))


def _pre_kernel(n_x, *refs):
    x_refs = refs[:n_x]
    g1_ref, wg_ref, wu_ref, wd_ref, g2_ref, win_ref, x1_ref, p_ref = refs[n_x:]
    x1 = _swiglu_half(_pick_group(x_refs), g1_ref[...], wg_ref[...], wu_ref[...], wd_ref[...])
    x1_ref[...] = x1
    p_ref[...] = _dot(_rms(x1, g2_ref[...]).astype(BF16), win_ref[...])


def _pre_call(xs, g1, w_gu, w_down, g2, w_in, layer, mix_idx):
    n_in = w_in.shape[-1]
    tm = DENSE_TM
    row = lambda i: (i, 0)
    x_specs = [pl.BlockSpec((tm, D_MODEL), row)] if len(xs) == 1 else _group_specs(D_MODEL)
    return pl.pallas_call(
        functools.partial(_pre_kernel, len(xs)),
        grid=(N_TILES,),
        in_specs=[
            *x_specs,
            _const_spec((None, 1, D_MODEL), (layer, 0, 0)),
            _const_spec((None, D_MODEL, D_FF), (layer, 0, 0)),
            _const_spec((None, D_MODEL, D_FF), (layer, 0, 1)),
            _const_spec((None, D_FF, D_MODEL), (layer, 0, 0)),
            _const_spec((None, 1, D_MODEL), (layer, 0, 0)),
            _const_spec((None, D_MODEL, n_in), (mix_idx, 0, 0)),
        ],
        out_specs=[pl.BlockSpec((tm, D_MODEL), row), pl.BlockSpec((tm, n_in), row)],
        out_shape=[jax.ShapeDtypeStruct((ROWS_PAD, D_MODEL), F32), jax.ShapeDtypeStruct((ROWS_PAD, n_in), F32)],
        compiler_params=_cparams(("arbitrary",)),
        name="pre",
    )(*xs, g1, w_gu, w_gu, w_down, g2, w_in)


def _post_kernel(n_mix, final, *refs):
    x1_ref = refs[0]
    mix_refs = refs[1:1 + 3 * n_mix]
    wout_ref, g_ref, wg_ref, wu_ref, wd_ref = refs[1 + 3 * n_mix:6 + 3 * n_mix]
    rest = refs[6 + 3 * n_mix:]
    x2 = x1_ref[...]
    width = AB_OUT // n_mix
    for i in range(n_mix):
        x2 = x2 + _dot(_pick_group(mix_refs[3 * i:3 * i + 3]), wout_ref[i * width:(i + 1) * width, :])
    y = _swiglu_half(x2, g_ref[...], wg_ref[...], wu_ref[...], wd_ref[...])
    if final:
        gf_ref, yp_ref, yr_ref = rest
        y = _rms(y, gf_ref[...])
        step = pl.program_id(0)

        @pl.when(step < TILES_PROMPT)
        def _():
            yp_ref[...] = y

        @pl.when(step >= TILES_PROMPT)
        def _():
            yr_ref[...] = y
    else:
        (o_ref,) = rest
        o_ref[...] = y


def _post_call(x1, mixes, w_out, g, w_gu, w_down, layer, mix_idx, g_final):
    tm = DENSE_TM
    row = lambda i: (i, 0)
    final = g_final is not None
    in_specs = [pl.BlockSpec((tm, D_MODEL), row)]
    for triple in mixes:
        in_specs += _group_specs(triple[0].shape[1])
    in_specs += [
        _const_spec((None, AB_OUT, D_MODEL), (mix_idx, 0, 0)),
        _const_spec((None, 1, D_MODEL), (layer, 0, 0)),
        _const_spec((None, D_MODEL, D_FF), (layer, 0, 0)),
        _const_spec((None, D_MODEL, D_FF), (layer, 0, 1)),
        _const_spec((None, D_FF, D_MODEL), (layer, 0, 0)),
    ]
    args = [x1, *[m for triple in mixes for m in triple], w_out, g, w_gu, w_gu, w_down]
    if final:
        in_specs.append(_const_spec((1, D_MODEL), (0, 0)))
        args.append(g_final)
        out_specs = [pl.BlockSpec((tm, D_MODEL), lambda i: (jnp.minimum(i, TILES_PROMPT - 1), 0)),
                     pl.BlockSpec((tm, D_MODEL), lambda i: (jnp.maximum(i - TILES_PROMPT, 0), 0))]
        out_shape = [jax.ShapeDtypeStruct((ROWS_PROMPT, D_MODEL), F32),
                     jax.ShapeDtypeStruct((ROWS_REST, D_MODEL), F32)]
    else:
        out_specs = pl.BlockSpec((tm, D_MODEL), row)
        out_shape = jax.ShapeDtypeStruct((ROWS_PAD, D_MODEL), F32)
    return pl.pallas_call(
        functools.partial(_post_kernel, len(mixes), final),
        grid=(N_TILES,),
        in_specs=in_specs,
        out_specs=out_specs,
        out_shape=out_shape,
        compiler_params=_cparams(("arbitrary",)),
        name="post",
    )(*args)


S5_HALF_IN = S5_WIDTH // 2
S5_HALF_CH = S5_CH // 2
SUBLANES = 8


def _s5_kernel(chained, tc, u_ref, h0re_ref, h0im_ref, bcat_ref, cre_ref, cim_ref, ast_re_ref, ast_im_ref,
               apw_re_ref, apw_im_ref, d_ref, wglu_ref, out_ref, hre_out, him_out, xre, xim, *carry):
    nb = tc // SUBLANES
    u = u_ref[...]
    ub = u.astype(BF16)
    for hf in range(2):
        xh = _dot(ub[:, hf * S5_HALF_IN:(hf + 1) * S5_HALF_IN], bcat_ref[hf])
        xre[:, hf * S5_HALF_CH:(hf + 1) * S5_HALF_CH] = xh[:, :S5_HALF_CH]
        xim[:, hf * S5_HALF_CH:(hf + 1) * S5_HALF_CH] = xh[:, S5_HALF_CH:]

    sr = xre[...].reshape(nb, SUBLANES, S5_CH)
    si = xim[...].reshape(nb, SUBLANES, S5_CH)
    rowi = lax.broadcasted_iota(jnp.int32, (nb, SUBLANES, S5_CH), 1)
    for step, d in enumerate((1, 2, 4)):
        ar = ast_re_ref[step:step + 1, :][None]
        ai = ast_im_ref[step:step + 1, :][None]
        pr = pltpu.roll(sr, d, 1)
        pi = pltpu.roll(si, d, 1)
        keep = rowi >= d
        sr, si = (sr + jnp.where(keep, ar * pr - ai * pi, 0.0),
                  si + jnp.where(keep, ar * pi + ai * pr, 0.0))
    apr = apw_re_ref[...]
    api = apw_im_ref[...]
    if chained:
        cre, cim = carry
        xre[...] = sr.reshape(tc, S5_CH)
        xim[...] = si.reshape(tc, S5_CH)

        @pl.when(pl.program_id(1) == 0)
        def _():
            cre[...] = h0re_ref[...]
            cim[...] = h0im_ref[...]

        def group(r, c):
            hr, hi = c
            rows = pl.ds(pl.multiple_of(r * SUBLANES, SUBLANES), SUBLANES)
            nr = xre[rows, :] + apr * hr - api * hi
            ni = xim[rows, :] + apr * hi + api * hr
            xre[rows, :] = nr
            xim[rows, :] = ni
            return nr[SUBLANES - 1:, :], ni[SUBLANES - 1:, :]

        hr, hi = lax.fori_loop(0, nb, group, (cre[...], cim[...]), unroll=min(nb, 4))
        cre[...] = hr
        cim[...] = hi
        hre_out[...] = hr
        him_out[...] = hi
    else:
        h0r = h0re_ref[...].reshape(nb, SUBLANES, S5_CH)
        h0i = h0im_ref[...].reshape(nb, SUBLANES, S5_CH)
        fr = (sr + apr[None] * h0r - api[None] * h0i).reshape(tc, S5_CH)
        fi = (si + apr[None] * h0i + api[None] * h0r).reshape(tc, S5_CH)
        xre[...] = fr
        xim[...] = fi
        hre_out[...] = fr
        him_out[...] = fi

    ys = []
    for hf in range(2):
        cols = slice(hf * S5_HALF_CH, (hf + 1) * S5_HALF_CH)
        ys.append(_dot(xre[:, cols].astype(BF16), cre_ref[hf]) + _dot(xim[:, cols].astype(BF16), cim_ref[hf]))
    y = jnp.concatenate(ys, axis=1) + d_ref[...] * u
    z = jax.nn.gelu(y)
    _store_rows(out_ref, z * jax.nn.sigmoid(_dot(z.astype(BF16), wglu_ref[...])))


def _s5_call(p, row0, n_seq, seq_len, h0re, h0im, prm, chained, tc, out_rows=None):
    n_rows = n_seq * seq_len
    blk0 = row0 // tc
    assert row0 % tc == 0 and n_rows % tc == 0
    out_blk = tc if out_rows is None else out_rows
    assert out_rows is None or n_rows == tc
    consts = [prm["bcat"], prm["cre"], prm["cim"], prm["ast_re"], prm["ast_im"], prm["apw_re"], prm["apw_im"],
              prm["d"], prm["wglu"]]
    const_specs = [_const_spec(c.shape, (0,) * c.ndim) for c in consts]
    scratch = [pltpu.VMEM((tc, S5_CH), F32), pltpu.VMEM((tc, S5_CH), F32)]
    if chained:
        n_chunk = seq_len // tc
        grid = (n_seq, n_chunk)
        u_spec = pl.BlockSpec((tc, S5_WIDTH), lambda b, c: (blk0 + b * n_chunk + c, 0))
        h_specs = [_const_spec((1, S5_CH), (0, 0))] * 2
        out_specs = [pl.BlockSpec((out_blk, S5_WIDTH), lambda b, c: (b * n_chunk + c, 0)),
                     pl.BlockSpec((None, 1, S5_CH), lambda b, c: (b, 0, 0)),
                     pl.BlockSpec((None, 1, S5_CH), lambda b, c: (b, 0, 0))]
        out_shape = [jax.ShapeDtypeStruct((n_rows // tc * out_blk, S5_WIDTH), BF16),
                     jax.ShapeDtypeStruct((n_seq, 1, S5_CH), F32), jax.ShapeDtypeStruct((n_seq, 1, S5_CH), F32)]
        scratch += [pltpu.VMEM((1, S5_CH), F32), pltpu.VMEM((1, S5_CH), F32)]
        sem = ("arbitrary", "arbitrary")
    else:
        assert seq_len == SUBLANES
        grid = (n_rows // tc,)
        u_spec = pl.BlockSpec((tc, S5_WIDTH), lambda i: (blk0 + i, 0))
        h_specs = [pl.BlockSpec((tc, S5_CH), lambda i: (i, 0))] * 2
        out_specs = [pl.BlockSpec((tc, S5_WIDTH), lambda i: (i, 0)),
                     pl.BlockSpec((tc, S5_CH), lambda i: (i, 0)), pl.BlockSpec((tc, S5_CH), lambda i: (i, 0))]
        out_shape = [jax.ShapeDtypeStruct((n_rows, S5_WIDTH), BF16),
                     jax.ShapeDtypeStruct((n_rows, S5_CH), F32), jax.ShapeDtypeStruct((n_rows, S5_CH), F32)]
        sem = ("arbitrary",)
    return pl.pallas_call(
        functools.partial(_s5_kernel, chained, tc),
        grid=grid,
        in_specs=[u_spec, *h_specs, *const_specs],
        out_specs=out_specs,
        out_shape=out_shape,
        scratch_shapes=scratch,
        compiler_params=_cparams(sem),
        name="s5",
    )(p, h0re, h0im, *consts)


def _s5_params(a_re, a_im, log_dt, b_re, b_im, c_re, c_im, d_skip, w_glu):
    dt = jnp.exp(log_dt)[:, None]
    mag = jnp.exp(dt * a_re)
    abar_re, abar_im = mag * jnp.cos(dt * a_im), mag * jnp.sin(dt * a_im)
    den = a_re * a_re + a_im * a_im
    num_re = abar_re - 1.0
    f_re = (num_re * a_re + abar_im * a_im) / den
    f_im = (abar_im * a_re - num_re * a_im) / den
    bbar_re = f_re[..., None] * b_re - f_im[..., None] * b_im
    bbar_im = f_re[..., None] * b_im + f_im[..., None] * b_re

    def block_diag_in(w):
        w = w.reshape(2, S5_GROUPS // 2, S5_STATE, S5_GROUP)
        eye = jnp.eye(S5_GROUPS // 2, dtype=F32)
        return jnp.einsum("hgpn,gk->hgnkp", w, eye).reshape(2, S5_HALF_IN, S5_HALF_CH)

    def block_diag_out(w):
        w = w.reshape(2, S5_GROUPS // 2, S5_GROUP, S5_STATE)
        eye = jnp.eye(S5_GROUPS // 2, dtype=F32)
        return jnp.einsum("hgnp,gk->hgpkn", w, eye).reshape(2, S5_HALF_CH, S5_HALF_IN)

    bcat = jnp.concatenate([block_diag_in(bbar_re), block_diag_in(bbar_im)], axis=-1).astype(BF16)
    ar, ai = abar_re.reshape(1, S5_CH), abar_im.reshape(1, S5_CH)
    pows_re, pows_im = [ar], [ai]
    for _ in range(SUBLANES - 1):
        pr, pi = pows_re[-1], pows_im[-1]
        pows_re.append(pr * ar - pi * ai)
        pows_im.append(pr * ai + pi * ar)
    zeros = jnp.zeros((SUBLANES - 3, S5_CH), F32)
    return dict(
        bcat=bcat,
        cre=block_diag_out(c_re).astype(BF16),
        cim=block_diag_out(-c_im).astype(BF16),
        ast_re=jnp.concatenate([pows_re[0], pows_re[1], pows_re[3], zeros], axis=0),
        ast_im=jnp.concatenate([pows_im[0], pows_im[1], pows_im[3], zeros], axis=0),
        apw_re=jnp.concatenate(pows_re, axis=0),
        apw_im=jnp.concatenate(pows_im, axis=0),
        d=d_skip.reshape(1, S5_WIDTH),
        wglu=w_glu.astype(BF16),
    )


def _ret_tables(t):
    gam = np.log(1.0 - 2.0 ** (-5.0 - np.arange(RET_HEADS, dtype=np.float64)))
    tt = np.arange(t, dtype=np.float64)
    dq = np.exp(gam[:, None] * (tt[None, :] + 1.0))
    dq = np.broadcast_to(dq.reshape(RET_HEADS * t, 1), (RET_HEADS * t, RET_DV))
    diff = tt[:, None] - tt[None, :]
    dm = np.where(diff >= 0, np.exp(gam[:, None, None] * np.maximum(diff, 0.0)[None]), 0.0)
    dk = np.exp(gam[:, None] * (t - 1.0 - tt[None, :]))
    dk = np.repeat(dk.T, RET_DK, axis=1)
    ds = np.repeat(np.exp(gam * t), RET_DK)[:, None] * np.ones((1, RET_DV))
    hm = np.repeat(np.eye(RET_HEADS), RET_DK, axis=1)
    f = lambda a: jnp.asarray(np.ascontiguousarray(a), F32)
    return dict(dq=f(dq), dm=f(dm.reshape(RET_HEADS * t, t)), dk=f(dk), ds=f(ds),
                hm=f(np.concatenate([hm, np.zeros((SUBLANES - RET_HEADS, RET_QK))], axis=0)))


def _rope_tables(pos):
    half = RET_DK // 2
    inv_freq = 1.0 / (ROPE_BASE ** (jnp.arange(half, dtype=F32) / half))
    ang = pos.astype(F32)[:, None] * inv_freq[None, :]
    cos, sin = jnp.cos(ang), jnp.sin(ang)
    zero = jnp.zeros_like(sin)
    tile = lambda a, b: jnp.tile(jnp.concatenate([a, b], axis=1), (1, RET_HEADS))
    return tile(cos, cos), tile(-sin, zero), tile(zero, sin)


def _ret_chunk(t, q, k, v, g, s, cos, s_up, s_dn, dq, dm, dk, ds, hm):
    half = RET_DK // 2

    def rope(x):
        return x * cos + pltpu.roll(x, RET_QK - half, 1) * s_up + pltpu.roll(x, half, 1) * s_dn

    qr = rope(q)
    kr = rope(k) * (RET_DK ** -0.5)
    qs = jnp.concatenate([qr * hm[h:h + 1, :] for h in range(RET_HEADS)], axis=0).astype(BF16)
    inter = _dot(qs, s.astype(BF16)) * dq
    prob = (_dot_nt(qs, kr.astype(BF16)) * dm).astype(BF16)
    vb = v.astype(BF16)
    outs = []
    for h in range(RET_HEADS):
        rows = slice(h * t, (h + 1) * t)
        cols = slice(h * RET_DV, (h + 1) * RET_DV)
        o = inter[rows] + _dot(prob[rows], vb[:, cols])
        mu = jnp.mean(o, axis=-1, keepdims=True)
        oc = o - mu
        var = jnp.mean(oc * oc, axis=-1, keepdims=True)
        gh = g[:, cols]
        outs.append(oc * lax.rsqrt(var + EPS) * (gh * jax.nn.sigmoid(gh)))
    kd = kr * dk
    ks = jnp.concatenate([kd * hm[h:h + 1, :] for h in range(RET_HEADS)], axis=0).astype(BF16)
    vs = jnp.concatenate([vb[:, h * RET_DV:(h + 1) * RET_DV] for h in range(RET_HEADS)], axis=0)
    s_new = s * ds + _dot_tn(ks, vs)
    return jnp.concatenate(outs, axis=1), s_new


def _ret_kernel(chained, t, n_sub, q_ref, k_ref, v_ref, g_ref, s0_ref, cos_ref, sup_ref, sdn_ref, dq_ref, dm_ref,
                dk_ref, ds_ref, hm_ref, *rest):
    out_ref, st_ref, *scratch = rest[-3:] if chained else rest[-2:]
    tabs = (dq_ref[...], dm_ref[...], dk_ref[...], ds_ref[...], hm_ref[...])
    if chained:
        (s_scr,) = scratch

        @pl.when(pl.program_id(1) == 0)
        def _():
            s_scr[...] = s0_ref[...]

        o, s_new = _ret_chunk(t, q_ref[...], k_ref[...], v_ref[...], g_ref[...], s_scr[...],
                              cos_ref[...], sup_ref[...], sdn_ref[...], *tabs)
        _store_rows(out_ref, o)
        s_scr[...] = s_new
        st_ref[...] = s_new
    else:
        def one(i, carry):
            rows = pl.ds(pl.multiple_of(i * t, t), t)
            o, s_new = _ret_chunk(t, q_ref[rows, :], k_ref[rows, :], v_ref[rows, :], g_ref[rows, :], s0_ref[i],
                                  cos_ref[...], sup_ref[...], sdn_ref[...], *tabs)
            out_ref[rows, :] = o.astype(out_ref.dtype)
            st_ref[i] = s_new
            return carry

        lax.fori_loop(0, n_sub, one, 0)


def _stacked_state_io(n_sub, sd, layer_i, st_prev, n_in):
    spec = pl.BlockSpec((None, n_sub, *sd), lambda i: (layer_i, i, 0, 0))
    if st_prev is None:
        return spec, [], [], {}
    return spec, [pl.BlockSpec(memory_space=pl.ANY)], [st_prev], {n_in: 1}


def _ret_call(p, row0, n_seq, seq_len, s0, pos0, chained, t, n_sub=1, out_rows=None, layer_i=0, st_prev=None):
    n_rows = n_seq * seq_len
    rows_blk = t if chained else t * n_sub
    blk0 = row0 // rows_blk
    assert row0 % rows_blk == 0
    out_blk = rows_blk if out_rows is None else out_rows
    assert out_rows is None or n_rows == rows_blk
    tabs = _ret_tables(t)
    consts = [tabs[n] for n in ("dq", "dm", "dk", "ds", "hm")]
    const_specs = [_const_spec(c.shape, (0, 0)) for c in consts]
    cos, s_up, s_dn = _rope_tables(pos0 + jnp.arange(seq_len, dtype=jnp.int32))
    sd = RET_QK, RET_DV
    if chained:
        n_chunk = seq_len // t
        grid = (n_seq, n_chunk)
        rmap = lambda cb: (lambda b, c: (blk0 + b * n_chunk + c, cb))
        data_specs = [pl.BlockSpec((t, RET_QK), rmap(2)), pl.BlockSpec((t, RET_QK), rmap(3)),
                      pl.BlockSpec((t, RET_WIDTH), rmap(2)), pl.BlockSpec((t, RET_WIDTH), rmap(3)),
                      _const_spec(sd, (0, 0))]
        rope_specs = [pl.BlockSpec((t, RET_QK), lambda b, c: (c, 0))] * 3
        out_specs = [pl.BlockSpec((out_blk, RET_WIDTH), lambda b, c: (b * n_chunk + c, 0)),
                     pl.BlockSpec((None, *sd), lambda b, c: (b, 0, 0))]
        st_shape = (n_seq, *sd)
        extra_specs, extra_args, aliases = [], [], {}
        scratch = [pltpu.VMEM(sd, F32)]
        sem = ("arbitrary", "arbitrary")
    else:
        assert seq_len == t and n_seq % n_sub == 0
        grid = (n_seq // n_sub,)
        rmap = lambda cb: (lambda i: (blk0 + i, cb))
        st_spec, extra_specs, extra_args, aliases = _stacked_state_io(n_sub, sd, layer_i, st_prev, 13)
        data_specs = [pl.BlockSpec((rows_blk, RET_QK), rmap(2)), pl.BlockSpec((rows_blk, RET_QK), rmap(3)),
                      pl.BlockSpec((rows_blk, RET_WIDTH), rmap(2)), pl.BlockSpec((rows_blk, RET_WIDTH), rmap(3)),
                      st_spec]
        rope_specs = [_const_spec((t, RET_QK), (0, 0))] * 3
        out_specs = [pl.BlockSpec((rows_blk, RET_WIDTH), lambda i: (i, 0)), st_spec]
        st_shape = s0.shape
        scratch = []
        sem = ("arbitrary",)
    in_specs = [*data_specs, *rope_specs, *const_specs, *extra_specs]
    assert not aliases or list(aliases) == [len(in_specs) - 1]
    return pl.pallas_call(
        functools.partial(_ret_kernel, chained, t, n_sub),
        grid=grid,
        in_specs=in_specs,
        out_specs=out_specs,
        out_shape=[jax.ShapeDtypeStruct((n_rows // rows_blk * out_blk, RET_WIDTH), BF16),
                   jax.ShapeDtypeStruct(st_shape, F32)],
        scratch_shapes=scratch,
        input_output_aliases=aliases,
        compiler_params=_cparams(sem),
        name="retention",
    )(p, p, p, p, s0, cos, s_up, s_dn, *consts, *extra_args)


GLA_SEL_ROWS = LANES


def _gla_tables(t, n_blk):
    cg = t * n_blk
    blk = np.arange(cg) // t
    same = blk[:, None] == blk[None, :]
    tri = same & (np.arange(cg)[:, None] >= np.arange(cg)[None, :])
    sel = np.arange(GLA_SEL_ROWS)[:, None] == blk[None, :]
    cum = np.concatenate([tri, same, sel], axis=0).astype(np.float32)
    ones = (np.arange(t * GLA_DK)[:, None] // GLA_DK == np.arange(LANES)[None, :]).astype(np.float32)
    hm = np.repeat(np.eye(GLA_HEADS), GLA_DK, axis=1)
    hm = np.concatenate([hm, np.zeros((SUBLANES - GLA_HEADS, GLA_QK))], axis=0).astype(np.float32)
    return jnp.asarray(cum, BF16), jnp.asarray(ones, BF16), jnp.asarray(hm, F32)


def _gla_kernel(chained, t, n_blk, q_ref, k_ref, v_ref, r_ref, lr_ref, s0_ref, wa_ref, ba_ref, ng_ref, cum_ref,
                ones_ref, hm_ref, *rest):
    out_ref, st_ref, rt_scr, o_scr, *scratch = rest[-5:] if chained else rest[-4:]
    cg = t * n_blk
    nh, dk, dv = GLA_HEADS, GLA_DK, GLA_DV
    q = q_ref[...]
    ksc = k_ref[...] * (dk ** -0.5)
    vb = v_ref[...].astype(BF16)
    hm = hm_ref[...]
    la = jax.nn.log_sigmoid(_dot(lr_ref[...].astype(BF16), wa_ref[...]) + ba_ref[...]) / GLA_TAU
    cums = _dot_exact01(cum_ref[...], la)
    b = cums[:cg]
    bl = cums[cg:2 * cg]
    a_cols = jnp.exp(cums[2 * cg:]).T
    qe = q * jnp.exp(b)
    ke = ksc * jnp.exp(bl - b)

    row_t = lax.broadcasted_iota(jnp.int32, (t, GLA_QK), 0)
    for j in range(n_blk):
        rows = slice(j * t, (j + 1) * t)
        qj, kj, bj = q[rows], ksc[rows], b[rows]
        for s in range(t):
            e = jnp.exp(jnp.where(row_t >= s, bj - bj[s:s + 1, :], NEG_BIG))
            prod = (qj * kj[s:s + 1, :]) * e
            for h in range(nh):
                rt_scr[(j * nh + h) * t:(j * nh + h + 1) * t, s * dk:(s + 1) * dk] = prod[:, h * dk:(h + 1) * dk]
    scores = _dot(rt_scr[...].astype(BF16), ones_ref[...])
    intra = []
    for h in range(nh):
        pieces = []
        for j in range(n_blk):
            piece = scores[(j * nh + h) * t:(j * nh + h + 1) * t, :]
            pieces.append(pltpu.roll(piece, j * t, 1) if j else piece)
        pfull = jnp.concatenate(pieces, axis=0) if n_blk > 1 else pieces[0]
        intra.append(_dot(pfull[:, :cg].astype(BF16), vb[:, h * dv:(h + 1) * dv]))

    if chained:
        (s_scr,) = scratch

        @pl.when(pl.program_id(1) == 0)
        def _():
            s_scr[...] = s0_ref[...]

        state = s_scr[...]
    for j in range(n_blk):
        rows = slice(j * t, (j + 1) * t)
        if not chained:
            state = s0_ref[j]
        qs = jnp.concatenate([qe[rows] * hm[h:h + 1, :] for h in range(nh)], axis=0).astype(BF16)
        inter = _dot(qs, state.astype(BF16))
        for h in range(nh):
            o_scr[rows, h * dv:(h + 1) * dv] = inter[h * t:(h + 1) * t] + intra[h][rows]
        ks = jnp.concatenate([ke[rows] * hm[h:h + 1, :] for h in range(nh)], axis=0).astype(BF16)
        vs = jnp.concatenate([vb[rows, h * dv:(h + 1) * dv] for h in range(nh)], axis=0)
        state = state * a_cols[:, j:j + 1] + _dot_tn(ks, vs)
        if not chained:
            st_ref[j] = state
    if chained:
        s_scr[...] = state
        st_ref[...] = state

    o = o_scr[...]
    r = r_ref[...]
    gated = []
    for h in range(nh):
        cols = slice(h * dv, (h + 1) * dv)
        oh = o[:, cols]
        on = oh * lax.rsqrt(jnp.mean(oh * oh, axis=-1, keepdims=True) + EPS) * ng_ref[...]
        rh = r[:, cols]
        gated.append(on * (rh * jax.nn.sigmoid(rh)))
    _store_rows(out_ref, jnp.concatenate(gated, axis=1))


def _gla_call(p, row0, n_seq, seq_len, s0, wa, ba, ng, chained, t, n_blk, out_rows=None, layer_i=0,
              st_prev=None):
    n_rows = n_seq * seq_len
    cg = t * n_blk
    blk0 = row0 // cg
    assert row0 % cg == 0 and cg <= LANES and n_blk <= GLA_SEL_ROWS
    out_blk = cg if out_rows is None else out_rows
    assert out_rows is None or n_rows == cg
    cum, ones, hm = _gla_tables(t, n_blk)
    consts = [wa, ba, ng, cum, ones, hm]
    const_specs = [_const_spec(c.shape, (0,) * c.ndim) for c in consts]
    sd = GLA_QK, GLA_DV
    if chained:
        n_chunk = seq_len // cg
        grid = (n_seq, n_chunk)
        rmap = lambda cb: (lambda b, c: (blk0 + b * n_chunk + c, cb))
        s_spec = _const_spec(sd, (0, 0))
        out_specs = [pl.BlockSpec((out_blk, GLA_V), lambda b, c: (b * n_chunk + c, 0)),
                     pl.BlockSpec((None, *sd), lambda b, c: (b, 0, 0))]
        st_shape = (n_seq, *sd)
        extra_specs, extra_args, aliases = [], [], {}
        scratch = [pltpu.VMEM(sd, F32)]
        sem = ("arbitrary", "arbitrary")
    else:
        assert seq_len == t and n_seq % n_blk == 0
        grid = (n_seq // n_blk,)
        rmap = lambda cb: (lambda i: (blk0 + i, cb))
        s_spec, extra_specs, extra_args, aliases = _stacked_state_io(n_blk, sd, layer_i, st_prev, 12)
        out_specs = [pl.BlockSpec((cg, GLA_V), lambda i: (i, 0)), s_spec]
        st_shape = s0.shape
        scratch = []
        sem = ("arbitrary",)
    data_specs = [pl.BlockSpec((cg, GLA_QK), rmap(0)), pl.BlockSpec((cg, GLA_QK), rmap(1)),
                  pl.BlockSpec((cg, GLA_V), rmap(1)), pl.BlockSpec((cg, GLA_V), rmap(2)),
                  pl.BlockSpec((cg, LANES), rmap((2 * GLA_QK + 2 * GLA_V) // LANES)), s_spec]
    in_specs = [*data_specs, *const_specs, *extra_specs]
    assert not aliases or list(aliases) == [len(in_specs) - 1]
    return pl.pallas_call(
        functools.partial(_gla_kernel, chained, t, n_blk),
        grid=grid,
        in_specs=in_specs,
        out_specs=out_specs,
        out_shape=[jax.ShapeDtypeStruct((n_rows // cg * out_blk, GLA_V), BF16),
                   jax.ShapeDtypeStruct(st_shape, F32)],
        scratch_shapes=[pltpu.VMEM((n_blk * GLA_HEADS * t, t * GLA_DK), F32), pltpu.VMEM((cg, GLA_V), F32),
                        *scratch],
        input_output_aliases=aliases,
        compiler_params=_cparams(sem),
        name="gla",
    )(p, p, p, p, p, s0, *consts, *extra_args)


S5_TC_PROMPT = 256
S5_TC_SAMPLE = 256
RET_T_PROMPT = 256
RET_SUB_SAMPLE = 16
GLA_T_PROMPT = 16
GLA_BLK_PROMPT = 8
GLA_BLK_SAMPLE = 8


def _mixer_ab(p, i, prm, st_s5_re, st_s5_im, st_ret, s_ret_prev):
    zeros_h = jnp.zeros((1, S5_CH), F32)
    a_m, hre_m, him_m = _s5_call(p, ROW0_META, 1, N_META, zeros_h, zeros_h, prm, True, N_META, out_rows=DENSE_TM)
    a_p, hre_p, him_p = _s5_call(p, 0, BATCH, SEQ, hre_m[0], him_m[0], prm, True, S5_TC_PROMPT)
    h0re = jnp.repeat(st_s5_re[i].reshape(DEC_BATCH, S5_CH), DEC_SEQ, axis=0)
    h0im = jnp.repeat(st_s5_im[i].reshape(DEC_BATCH, S5_CH), DEC_SEQ, axis=0)
    a_s, hre_s, him_s = _s5_call(p, ROW0_SAMPLE, DEC_BATCH, DEC_SEQ, h0re, h0im, prm, False, S5_TC_SAMPLE)
    last = lambda h: h.reshape(DEC_BATCH, DEC_SEQ, S5_CH)[:, DEC_SEQ - 1].reshape(DEC_BATCH, S5_GROUPS, S5_STATE)

    zeros_s = jnp.zeros((RET_QK, RET_DV), F32)
    b_m, s_m = _ret_call(p, ROW0_META, 1, N_META, zeros_s, 0, True, N_META, out_rows=DENSE_TM)
    b_p, s_p = _ret_call(p, 0, BATCH, SEQ, s_m[0], N_META, True, RET_T_PROMPT)
    b_s, s_ret = _ret_call(p, ROW0_SAMPLE, DEC_BATCH, DEC_SEQ, st_ret, PAST_LEN, False, DEC_SEQ, RET_SUB_SAMPLE,
                           layer_i=i, st_prev=s_ret_prev)
    states = dict(
        p_s5_re=hre_p.reshape(BATCH, S5_GROUPS, S5_STATE), p_s5_im=him_p.reshape(BATCH, S5_GROUPS, S5_STATE),
        p_ret=s_p.reshape(BATCH, RET_HEADS, RET_DK, RET_DV), s_s5_re=last(hre_s), s_s5_im=last(him_s))
    return [(a_p, a_s, a_m), (b_p, b_s, b_m)], states, s_ret


def _mixer_gla(p, i, wa, ba, ng, st_gla, s_gla_prev):
    zeros_s = jnp.zeros((GLA_QK, GLA_DV), F32)
    o_m, s_m = _gla_call(p, ROW0_META, 1, N_META, zeros_s, wa, ba, ng, True, N_META, 1, out_rows=DENSE_TM)
    o_p, s_p = _gla_call(p, 0, BATCH, SEQ, s_m[0], wa, ba, ng, True, GLA_T_PROMPT, GLA_BLK_PROMPT)
    o_s, s_gla = _gla_call(p, ROW0_SAMPLE, DEC_BATCH, DEC_SEQ, st_gla, wa, ba, ng, False, DEC_SEQ, GLA_BLK_SAMPLE,
                           layer_i=i, st_prev=s_gla_prev)
    states = dict(p_gla=s_p.reshape(BATCH, GLA_HEADS, GLA_DK, GLA_DV))
    return [(o_p, o_s, o_m)], states, s_gla


def kernel(x_prompt, x_sample, state_s5_re, state_s5_im, state_ret, state_gla, meta_tokens, norm_ffn1, norm_mix,
           norm_ffn2, norm_final, ffn1_w_gu, ffn1_w_down, ffn2_w_gu, ffn2_w_down, ab_w_in, ab_w_out, s5_a_re,
           s5_a_im, s5_log_dt, s5_b_re, s5_b_im, s5_c_re, s5_c_im, s5_d, s5_w_glu, gla_w_in, gla_w_alpha2,
           gla_b_alpha, gla_norm, gla_w_out):
    bf = lambda w: w.astype(BF16)
    ffn1_gu, ffn1_dn, ffn2_gu, ffn2_dn = bf(ffn1_w_gu), bf(ffn1_w_down), bf(ffn2_w_gu), bf(ffn2_w_down)
    ab_in, ab_out, gla_out = bf(ab_w_in), bf(ab_w_out), bf(gla_w_out)
    gla_in = bf(jnp.pad(gla_w_in, ((0, 0), (0, 0), (0, GLA_IN_PAD - GLA_IN))))
    gla_wa = bf(jnp.pad(gla_w_alpha2, ((0, 0), (0, LANES - GLA_LOWRANK), (0, 0))))
    n1 = norm_ffn1.reshape(DEPTH, 1, D_MODEL)
    nm = norm_mix.reshape(DEPTH, 1, D_MODEL)
    n2 = norm_ffn2.reshape(DEPTH, 1, D_MODEL)

    meta_pad = jnp.pad(meta_tokens.astype(x_prompt.dtype), ((0, DENSE_TM - N_META), (0, 0)))
    xs = [x_prompt.reshape(ROWS_PROMPT, D_MODEL), x_sample.reshape(ROWS_SAMPLE, D_MODEL), meta_pad]
    st_ret = state_ret.reshape(N_EVEN, DEC_BATCH, RET_QK, RET_DV)
    st_gla = state_gla.reshape(N_ODD, DEC_BATCH, GLA_QK, GLA_DV)
    s_ret = s_gla = None
    collected = {}
    for layer in range(DEPTH):
        i = layer // 2
        last = layer == DEPTH - 1
        g_final = norm_final.reshape(1, D_MODEL) if last else None
        if layer % 2 == 0:
            x1, p = _pre_call(xs, n1, ffn1_gu, ffn1_dn, nm, ab_in, layer, i)
            prm = _s5_params(s5_a_re[i], s5_a_im[i], s5_log_dt[i], s5_b_re[i], s5_b_im[i], s5_c_re[i],
                             s5_c_im[i], s5_d[i], s5_w_glu[i])
            mixes, states, s_ret = _mixer_ab(p, i, prm, state_s5_re, state_s5_im, st_ret, s_ret)
            x = _post_call(x1, mixes, ab_out, n2, ffn2_gu, ffn2_dn, layer, i, g_final)
        else:
            x1, p = _pre_call(xs, n1, ffn1_gu, ffn1_dn, nm, gla_in, layer, i)
            mixes, states, s_gla = _mixer_gla(p, i, gla_wa[i], gla_b_alpha[i].reshape(1, GLA_QK),
                                              gla_norm[i].reshape(1, GLA_DV), st_gla, s_gla)
            x = _post_call(x1, mixes, gla_out, n2, ffn2_gu, ffn2_dn, layer, i, g_final)
        xs = [x]
        for name, val in states.items():
            collected.setdefault(name, []).append(val)
    out = {name: jnp.stack(vals) for name, vals in collected.items()}
    y_prompt, y_rest = x
    return (y_prompt.reshape(BATCH, SEQ, D_MODEL), y_rest[:ROWS_SAMPLE].reshape(DEC_BATCH, DEC_SEQ, D_MODEL),
            out["p_s5_re"], out["p_s5_im"], out["p_ret"], out["p_gla"], out["s_s5_re"], out["s_s5_im"],
            s_ret.reshape(N_EVEN, DEC_BATCH, RET_HEADS, RET_DK, RET_DV),
            s_gla.reshape(N_ODD, DEC_BATCH, GLA_HEADS, GLA_DK, GLA_DV))
```

```python
import functools
import math

import numpy as np
import jax
import jax.numpy as jnp
from jax import lax
from jax.experimental import pallas as pl
from jax.experimental.pallas import tpu as pltpu

F32 = jnp.float32
BF16 = jnp.bfloat16

D_MODEL = 1024
BATCH = 8
SEQ = 2048
DEPTH = 4
DEC_BATCH = 128
DEC_SEQ = 8
PAST_LEN = 16384
N_META = 16
N_EVEN = (DEPTH + 1) // 2
N_ODD = DEPTH // 2
S5_WIDTH = D_MODEL // 2
S5_GROUP = 16
S5_GROUPS = S5_WIDTH // S5_GROUP
S5_STATE = 64
S5_CH = S5_GROUPS * S5_STATE
RET_HEADS = 4
RET_DK = D_MODEL // 16
RET_DV = 2 * RET_DK
RET_QK = RET_HEADS * RET_DK
RET_WIDTH = RET_HEADS * RET_DV
AB_IN = S5_WIDTH + 2 * RET_QK + 2 * RET_WIDTH
AB_OUT = S5_WIDTH + RET_WIDTH
GLA_HEADS = 4
GLA_DK = D_MODEL // (2 * GLA_HEADS)
GLA_DV = D_MODEL // GLA_HEADS
GLA_QK = GLA_HEADS * GLA_DK
GLA_V = GLA_HEADS * GLA_DV
GLA_LOWRANK = 16
GLA_TAU = 16.0
GLA_IN = 2 * GLA_QK + 2 * GLA_V + GLA_LOWRANK
LANES = 128
GLA_IN_PAD = 2 * GLA_QK + 2 * GLA_V + LANES
D_FF = 128 * ((8 * D_MODEL // 3 + 127) // 128)
EPS = 1e-6
ROPE_BASE = 10000.0
NEG_BIG = -1e30

ROWS_PROMPT = BATCH * SEQ
ROWS_SAMPLE = DEC_BATCH * DEC_SEQ
ROW0_SAMPLE = ROWS_PROMPT
ROW0_META = ROWS_PROMPT + ROWS_SAMPLE
ROWS = ROW0_META + N_META

VMEM_LIMIT = 56 * 1024 * 1024
DENSE_TM = 256
TILES_PROMPT = ROWS_PROMPT // DENSE_TM
TILES_SAMPLE = ROWS_SAMPLE // DENSE_TM
N_TILES = TILES_PROMPT + TILES_SAMPLE + 1
ROWS_PAD = N_TILES * DENSE_TM
ROWS_REST = ROWS_PAD - ROWS_PROMPT


def _cparams(sem):
    return pltpu.CompilerParams(dimension_semantics=sem, vmem_limit_bytes=VMEM_LIMIT)


def _dot(a, b):
    return jnp.dot(a, b, preferred_element_type=F32)


def _dot_tn(a, b):
    return lax.dot_general(a, b, (((0,), (0,)), ((), ())), preferred_element_type=F32)


def _dot_nt(a, b):
    return lax.dot_general(a, b, (((1,), (1,)), ((), ())), preferred_element_type=F32)


def _dot_exact01(m_bf, x):
    h1 = x.astype(BF16)
    r1 = x - h1.astype(F32)
    h2 = r1.astype(BF16)
    r2 = r1 - h2.astype(F32)
    h3 = r2.astype(BF16)
    return _dot(m_bf, h1) + _dot(m_bf, h2) + _dot(m_bf, h3)


def _rms(x, g):
    return x * lax.rsqrt(jnp.mean(x * x, axis=-1, keepdims=True) + EPS) * g


def _swiglu_half(x, g, wg, wu, wd):
    h = _rms(x, g).astype(BF16)
    gate = _dot(h, wg)
    up = _dot(h, wu)
    act = (gate * jax.nn.sigmoid(gate) * up).astype(BF16)
    return x + 0.5 * _dot(act, wd)


def _store_rows(out_ref, val):
    rows = val.shape[0]
    out_ref[0:rows, :] = val.astype(out_ref.dtype)
    if out_ref.shape[0] > rows:
        out_ref[rows:, :] = jnp.zeros((out_ref.shape[0] - rows, out_ref.shape[1]), out_ref.dtype)


def _const_spec(shape, index):
    return pl.BlockSpec(shape, lambda *_: index, pipeline_mode=pl.Buffered(1))


def _group_specs(width):
    tm = DENSE_TM
    return [pl.BlockSpec((tm, width), lambda i: (jnp.minimum(i, TILES_PROMPT - 1), 0)),
            pl.BlockSpec((tm, width), lambda i: (jnp.clip(i - TILES_PROMPT, 0, TILES_SAMPLE - 1), 0)),
            pl.BlockSpec((tm, width), lambda i: (0, 0))]


def _pick_group(refs):
    if len(refs) == 1:
        return refs[0][...]
    i = pl.program_id(0)
    return jnp.where(i < TILES_PROMPT, refs[0][...],
                     jnp.where(i < TILES_PROMPT + TILES_SAMPLE, refs[1][...], refs[2][...]))


def _pre_kernel(n_x, *refs):
    x_refs = refs[:n_x]
    g1_ref, wg_ref, wu_ref, wd_ref, g2_ref, win_ref, x1_ref, p_ref = refs[n_x:]
    x1 = _swiglu_half(_pick_group(x_refs), g1_ref[...], wg_ref[...], wu_ref[...], wd_ref[...])
    x1_ref[...] = x1
    p_ref[...] = _dot(_rms(x1, g2_ref[...]).astype(BF16), win_ref[...])


def _pre_call(xs, g1, w_gu, w_down, g2, w_in, layer, mix_idx):
    n_in = w_in.shape[-1]
    tm = DENSE_TM
    row = lambda i: (i, 0)
    x_specs = [pl.BlockSpec((tm, D_MODEL), row)] if len(xs) == 1 else _group_specs(D_MODEL)
    return pl.pallas_call(
        functools.partial(_pre_kernel, len(xs)),
        grid=(N_TILES,),
        in_specs=[
            *x_specs,
            _const_spec((None, 1, D_MODEL), (layer, 0, 0)),
            _const_spec((None, D_MODEL, D_FF), (layer, 0, 0)),
            _const_spec((None, D_MODEL, D_FF), (layer, 0, 1)),
            _const_spec((None, D_FF, D_MODEL), (layer, 0, 0)),
            _const_spec((None, 1, D_MODEL), (layer, 0, 0)),
            _const_spec((None, D_MODEL, n_in), (mix_idx, 0, 0)),
        ],
        out_specs=[pl.BlockSpec((tm, D_MODEL), row), pl.BlockSpec((tm, n_in), row)],
        out_shape=[jax.ShapeDtypeStruct((ROWS_PAD, D_MODEL), F32), jax.ShapeDtypeStruct((ROWS_PAD, n_in), F32)],
        compiler_params=_cparams(("arbitrary",)),
        name="pre",
    )(*xs, g1, w_gu, w_gu, w_down, g2, w_in)


def _post_kernel(n_mix, final, *refs):
    x1_ref = refs[0]
    mix_refs = refs[1:1 + 3 * n_mix]
    wout_ref, g_ref, wg_ref, wu_ref, wd_ref = refs[1 + 3 * n_mix:6 + 3 * n_mix]
    rest = refs[6 + 3 * n_mix:]
    x2 = x1_ref[...]
    width = AB_OUT // n_mix
    for i in range(n_mix):
        x2 = x2 + _dot(_pick_group(mix_refs[3 * i:3 * i + 3]), wout_ref[i * width:(i + 1) * width, :])
    y = _swiglu_half(x2, g_ref[...], wg_ref[...], wu_ref[...], wd_ref[...])
    if final:
        gf_ref, yp_ref, yr_ref = rest
        y = _rms(y, gf_ref[...])
        step = pl.program_id(0)

        @pl.when(step < TILES_PROMPT)
        def _():
            yp_ref[...] = y

        @pl.when(step >= TILES_PROMPT)
        def _():
            yr_ref[...] = y
    else:
        (o_ref,) = rest
        o_ref[...] = y


def _post_call(x1, mixes, w_out, g, w_gu, w_down, layer, mix_idx, g_final):
    tm = DENSE_TM
    row = lambda i: (i, 0)
    final = g_final is not None
    in_specs = [pl.BlockSpec((tm, D_MODEL), row)]
    for triple in mixes:
        in_specs += _group_specs(triple[0].shape[1])
    in_specs += [
        _const_spec((None, AB_OUT, D_MODEL), (mix_idx, 0, 0)),
        _const_spec((None, 1, D_MODEL), (layer, 0, 0)),
        _const_spec((None, D_MODEL, D_FF), (layer, 0, 0)),
        _const_spec((None, D_MODEL, D_FF), (layer, 0, 1)),
        _const_spec((None, D_FF, D_MODEL), (layer, 0, 0)),
    ]
    args = [x1, *[m for triple in mixes for m in triple], w_out, g, w_gu, w_gu, w_down]
    if final:
        in_specs.append(_const_spec((1, D_MODEL), (0, 0)))
        args.append(g_final)
        out_specs = [pl.BlockSpec((tm, D_MODEL), lambda i: (jnp.minimum(i, TILES_PROMPT - 1), 0)),
                     pl.BlockSpec((tm, D_MODEL), lambda i: (jnp.maximum(i - TILES_PROMPT, 0), 0))]
        out_shape = [jax.ShapeDtypeStruct((ROWS_PROMPT, D_MODEL), F32),
                     jax.ShapeDtypeStruct((ROWS_REST, D_MODEL), F32)]
    else:
        out_specs = pl.BlockSpec((tm, D_MODEL), row)
        out_shape = jax.ShapeDtypeStruct((ROWS_PAD, D_MODEL), F32)
    return pl.pallas_call(
        functools.partial(_post_kernel, len(mixes), final),
        grid=(N_TILES,),
        in_specs=in_specs,
        out_specs=out_specs,
        out_shape=out_shape,
        compiler_params=_cparams(("arbitrary",)),
        name="post",
    )(*args)


S5_HALF_IN = S5_WIDTH // 2
S5_HALF_CH = S5_CH // 2
SUBLANES = 8


def _s5_kernel(chained, tc, u_ref, h0re_ref, h0im_ref, bcat_ref, cre_ref, cim_ref, ast_re_ref, ast_im_ref,
               apw_re_ref, apw_im_ref, d_ref, wglu_ref, out_ref, hre_out, him_out, xre, xim, *carry):
    nb = tc // SUBLANES
    u = u_ref[...]
    ub = u.astype(BF16)
    for hf in range(2):
        xh = _dot(ub[:, hf * S5_HALF_IN:(hf + 1) * S5_HALF_IN], bcat_ref[hf])
        xre[:, hf * S5_HALF_CH:(hf + 1) * S5_HALF_CH] = xh[:, :S5_HALF_CH]
        xim[:, hf * S5_HALF_CH:(hf + 1) * S5_HALF_CH] = xh[:, S5_HALF_CH:]

    sr = xre[...].reshape(nb, SUBLANES, S5_CH)
    si = xim[...].reshape(nb, SUBLANES, S5_CH)
    rowi = lax.broadcasted_iota(jnp.int32, (nb, SUBLANES, S5_CH), 1)
    for step, d in enumerate((1, 2, 4)):
        ar = ast_re_ref[step:step + 1, :][None]
        ai = ast_im_ref[step:step + 1, :][None]
        pr = pltpu.roll(sr, d, 1)
        pi = pltpu.roll(si, d, 1)
        keep = rowi >= d
        sr, si = (sr + jnp.where(keep, ar * pr - ai * pi, 0.0),
                  si + jnp.where(keep, ar * pi + ai * pr, 0.0))
    apr = apw_re_ref[...]
    api = apw_im_ref[...]
    if chained:
        cre, cim = carry
        xre[...] = sr.reshape(tc, S5_CH)
        xim[...] = si.reshape(tc, S5_CH)

        @pl.when(pl.program_id(1) == 0)
        def _():
            cre[...] = h0re_ref[...]
            cim[...] = h0im_ref[...]

        def group(r, c):
            hr, hi = c
            rows = pl.ds(pl.multiple_of(r * SUBLANES, SUBLANES), SUBLANES)
            nr = xre[rows, :] + apr * hr - api * hi
            ni = xim[rows, :] + apr * hi + api * hr
            xre[rows, :] = nr
            xim[rows, :] = ni
            return nr[SUBLANES - 1:, :], ni[SUBLANES - 1:, :]

        hr, hi = lax.fori_loop(0, nb, group, (cre[...], cim[...]), unroll=min(nb, 4))
        cre[...] = hr
        cim[...] = hi
        hre_out[...] = hr
        him_out[...] = hi
    else:
        h0r = h0re_ref[...].reshape(nb, SUBLANES, S5_CH)
        h0i = h0im_ref[...].reshape(nb, SUBLANES, S5_CH)
        fr = (sr + apr[None] * h0r - api[None] * h0i).reshape(tc, S5_CH)
        fi = (si + apr[None] * h0i + api[None] * h0r).reshape(tc, S5_CH)
        xre[...] = fr
        xim[...] = fi
        hre_out[...] = fr
        him_out[...] = fi

    ys = []
    for hf in range(2):
        cols = slice(hf * S5_HALF_CH, (hf + 1) * S5_HALF_CH)
        ys.append(_dot(xre[:, cols].astype(BF16), cre_ref[hf]) + _dot(xim[:, cols].astype(BF16), cim_ref[hf]))
    y = jnp.concatenate(ys, axis=1) + d_ref[...] * u
    z = jax.nn.gelu(y)
    _store_rows(out_ref, z * jax.nn.sigmoid(_dot(z.astype(BF16), wglu_ref[...])))


def _s5_call(p, row0, n_seq, seq_len, h0re, h0im, prm, chained, tc, out_rows=None):
    n_rows = n_seq * seq_len
    blk0 = row0 // tc
    assert row0 % tc == 0 and n_rows % tc == 0
    out_blk = tc if out_rows is None else out_rows
    assert out_rows is None or n_rows == tc
    consts = [prm["bcat"], prm["cre"], prm["cim"], prm["ast_re"], prm["ast_im"], prm["apw_re"], prm["apw_im"],
              prm["d"], prm["wglu"]]
    const_specs = [_const_spec(c.shape, (0,) * c.ndim) for c in consts]
    scratch = [pltpu.VMEM((tc, S5_CH), F32), pltpu.VMEM((tc, S5_CH), F32)]
    if chained:
        n_chunk = seq_len // tc
        grid = (n_seq, n_chunk)
        u_spec = pl.BlockSpec((tc, S5_WIDTH), lambda b, c: (blk0 + b * n_chunk + c, 0))
        h_specs = [_const_spec((1, S5_CH), (0, 0))] * 2
        out_specs = [pl.BlockSpec((out_blk, S5_WIDTH), lambda b, c: (b * n_chunk + c, 0)),
                     pl.BlockSpec((None, 1, S5_CH), lambda b, c: (b, 0, 0)),
                     pl.BlockSpec((None, 1, S5_CH), lambda b, c: (b, 0, 0))]
        out_shape = [jax.ShapeDtypeStruct((n_rows // tc * out_blk, S5_WIDTH), BF16),
                     jax.ShapeDtypeStruct((n_seq, 1, S5_CH), F32), jax.ShapeDtypeStruct((n_seq, 1, S5_CH), F32)]
        scratch += [pltpu.VMEM((1, S5_CH), F32), pltpu.VMEM((1, S5_CH), F32)]
        sem = ("arbitrary", "arbitrary")
    else:
        assert seq_len == SUBLANES
        grid = (n_rows // tc,)
        u_spec = pl.BlockSpec((tc, S5_WIDTH), lambda i: (blk0 + i, 0))
        h_specs = [pl.BlockSpec((tc, S5_CH), lambda i: (i, 0))] * 2
        out_specs = [pl.BlockSpec((tc, S5_WIDTH), lambda i: (i, 0)),
                     pl.BlockSpec((tc, S5_CH), lambda i: (i, 0)), pl.BlockSpec((tc, S5_CH), lambda i: (i, 0))]
        out_shape = [jax.ShapeDtypeStruct((n_rows, S5_WIDTH), BF16),
                     jax.ShapeDtypeStruct((n_rows, S5_CH), F32), jax.ShapeDtypeStruct((n_rows, S5_CH), F32)]
        sem = ("arbitrary",)
    return pl.pallas_call(
        functools.partial(_s5_kernel, chained, tc),
        grid=grid,
        in_specs=[u_spec, *h_specs, *const_specs],
        out_specs=out_specs,
        out_shape=out_shape,
        scratch_shapes=scratch,
        compiler_params=_cparams(sem),
        name="s5",
    )(p, h0re, h0im, *consts)


def _s5_params(a_re, a_im, log_dt, b_re, b_im, c_re, c_im, d_skip, w_glu):
    dt = jnp.exp(log_dt)[:, None]
    mag = jnp.exp(dt * a_re)
    abar_re, abar_im = mag * jnp.cos(dt * a_im), mag * jnp.sin(dt * a_im)
    den = a_re * a_re + a_im * a_im
    num_re = abar_re - 1.0
    f_re = (num_re * a_re + abar_im * a_im) / den
    f_im = (abar_im * a_re - num_re * a_im) / den
    bbar_re = f_re[..., None] * b_re - f_im[..., None] * b_im
    bbar_im = f_re[..., None] * b_im + f_im[..., None] * b_re

    def block_diag_in(w):
        w = w.reshape(2, S5_GROUPS // 2, S5_STATE, S5_GROUP)
        eye = jnp.eye(S5_GROUPS // 2, dtype=F32)
        return jnp.einsum("hgpn,gk->hgnkp", w, eye).reshape(2, S5_HALF_IN, S5_HALF_CH)

    def block_diag_out(w):
        w = w.reshape(2, S5_GROUPS // 2, S5_GROUP, S5_STATE)
        eye = jnp.eye(S5_GROUPS // 2, dtype=F32)
        return jnp.einsum("hgnp,gk->hgpkn", w, eye).reshape(2, S5_HALF_CH, S5_HALF_IN)

    bcat = jnp.concatenate([block_diag_in(bbar_re), block_diag_in(bbar_im)], axis=-1).astype(BF16)
    ar, ai = abar_re.reshape(1, S5_CH), abar_im.reshape(1, S5_CH)
    pows_re, pows_im = [ar], [ai]
    for _ in range(SUBLANES - 1):
        pr, pi = pows_re[-1], pows_im[-1]
        pows_re.append(pr * ar - pi * ai)
        pows_im.append(pr * ai + pi * ar)
    zeros = jnp.zeros((SUBLANES - 3, S5_CH), F32)
    return dict(
        bcat=bcat,
        cre=block_diag_out(c_re).astype(BF16),
        cim=block_diag_out(-c_im).astype(BF16),
        ast_re=jnp.concatenate([pows_re[0], pows_re[1], pows_re[3], zeros], axis=0),
        ast_im=jnp.concatenate([pows_im[0], pows_im[1], pows_im[3], zeros], axis=0),
        apw_re=jnp.concatenate(pows_re, axis=0),
        apw_im=jnp.concatenate(pows_im, axis=0),
        d=d_skip.reshape(1, S5_WIDTH),
        wglu=w_glu.astype(BF16),
    )


def _ret_tables(t):
    gam = np.log(1.0 - 2.0 ** (-5.0 - np.arange(RET_HEADS, dtype=np.float64)))
    tt = np.arange(t, dtype=np.float64)
    dq = np.exp(gam[:, None] * (tt[None, :] + 1.0))
    dq = np.broadcast_to(dq.reshape(RET_HEADS * t, 1), (RET_HEADS * t, RET_DV))
    diff = tt[:, None] - tt[None, :]
    dm = np.where(diff >= 0, np.exp(gam[:, None, None] * np.maximum(diff, 0.0)[None]), 0.0)
    dk = np.exp(gam[:, None] * (t - 1.0 - tt[None, :]))
    dk = np.repeat(dk.T, RET_DK, axis=1)
    ds = np.repeat(np.exp(gam * t), RET_DK)[:, None] * np.ones((1, RET_DV))
    hm = np.repeat(np.eye(RET_HEADS), RET_DK, axis=1)
    f = lambda a: jnp.asarray(np.ascontiguousarray(a), F32)
    return dict(dq=f(dq), dm=f(dm.reshape(RET_HEADS * t, t)), dk=f(dk), ds=f(ds),
                hm=f(np.concatenate([hm, np.zeros((SUBLANES - RET_HEADS, RET_QK))], axis=0)))


def _rope_tables(pos):
    half = RET_DK // 2
    inv_freq = 1.0 / (ROPE_BASE ** (jnp.arange(half, dtype=F32) / half))
    ang = pos.astype(F32)[:, None] * inv_freq[None, :]
    cos, sin = jnp.cos(ang), jnp.sin(ang)
    zero = jnp.zeros_like(sin)
    tile = lambda a, b: jnp.tile(jnp.concatenate([a, b], axis=1), (1, RET_HEADS))
    return tile(cos, cos), tile(-sin, zero), tile(zero, sin)


def _ret_chunk(t, q, k, v, g, s, cos, s_up, s_dn, dq, dm, dk, ds, hm):
    half = RET_DK // 2

    def rope(x):
        return x * cos + pltpu.roll(x, RET_QK - half, 1) * s_up + pltpu.roll(x, half, 1) * s_dn

    qr = rope(q)
    kr = rope(k) * (RET_DK ** -0.5)
    qs = jnp.concatenate([qr * hm[h:h + 1, :] for h in range(RET_HEADS)], axis=0).astype(BF16)
    inter = _dot(qs, s.astype(BF16)) * dq
    prob = (_dot_nt(qs, kr.astype(BF16)) * dm).astype(BF16)
    vb = v.astype(BF16)
    outs = []
    for h in range(RET_HEADS):
        rows = slice(h * t, (h + 1) * t)
        cols = slice(h * RET_DV, (h + 1) * RET_DV)
        o = inter[rows] + _dot(prob[rows], vb[:, cols])
        mu = jnp.mean(o, axis=-1, keepdims=True)
        oc = o - mu
        var = jnp.mean(oc * oc, axis=-1, keepdims=True)
        gh = g[:, cols]
        outs.append(oc * lax.rsqrt(var + EPS) * (gh * jax.nn.sigmoid(gh)))
    kd = kr * dk
    ks = jnp.concatenate([kd * hm[h:h + 1, :] for h in range(RET_HEADS)], axis=0).astype(BF16)
    vs = jnp.concatenate([vb[:, h * RET_DV:(h + 1) * RET_DV] for h in range(RET_HEADS)], axis=0)
    s_new = s * ds + _dot_tn(ks, vs)
    return jnp.concatenate(outs, axis=1), s_new


def _ret_kernel(chained, t, n_sub, q_ref, k_ref, v_ref, g_ref, s0_ref, cos_ref, sup_ref, sdn_ref, dq_ref, dm_ref,
                dk_ref, ds_ref, hm_ref, *rest):
    out_ref, st_ref, *scratch = rest[-3:] if chained else rest[-2:]
    tabs = (dq_ref[...], dm_ref[...], dk_ref[...], ds_ref[...], hm_ref[...])
    if chained:
        (s_scr,) = scratch

        @pl.when(pl.program_id(1) == 0)
        def _():
            s_scr[...] = s0_ref[...]

        o, s_new = _ret_chunk(t, q_ref[...], k_ref[...], v_ref[...], g_ref[...], s_scr[...],
                              cos_ref[...], sup_ref[...], sdn_ref[...], *tabs)
        _store_rows(out_ref, o)
        s_scr[...] = s_new
        st_ref[...] = s_new
    else:
        def one(i, carry):
            rows = pl.ds(pl.multiple_of(i * t, t), t)
            o, s_new = _ret_chunk(t, q_ref[rows, :], k_ref[rows, :], v_ref[rows, :], g_ref[rows, :], s0_ref[i],
                                  cos_ref[...], sup_ref[...], sdn_ref[...], *tabs)
            out_ref[rows, :] = o.astype(out_ref.dtype)
            st_ref[i] = s_new
            return carry

        lax.fori_loop(0, n_sub, one, 0)


def _stacked_state_io(n_sub, sd, layer_i, st_prev, n_in):
    spec = pl.BlockSpec((None, n_sub, *sd), lambda i: (layer_i, i, 0, 0))
    if st_prev is None:
        return spec, [], [], {}
    return spec, [pl.BlockSpec(memory_space=pl.ANY)], [st_prev], {n_in: 1}


def _ret_call(p, row0, n_seq, seq_len, s0, pos0, chained, t, n_sub=1, out_rows=None, layer_i=0, st_prev=None):
    n_rows = n_seq * seq_len
    rows_blk = t if chained else t * n_sub
    blk0 = row0 // rows_blk
    assert row0 % rows_blk == 0
    out_blk = rows_blk if out_rows is None else out_rows
    assert out_rows is None or n_rows == rows_blk
    tabs = _ret_tables(t)
    consts = [tabs[n] for n in ("dq", "dm", "dk", "ds", "hm")]
    const_specs = [_const_spec(c.shape, (0, 0)) for c in consts]
    cos, s_up, s_dn = _rope_tables(pos0 + jnp.arange(seq_len, dtype=jnp.int32))
    sd = RET_QK, RET_DV
    if chained:
        n_chunk = seq_len // t
        grid = (n_seq, n_chunk)
        rmap = lambda cb: (lambda b, c: (blk0 + b * n_chunk + c, cb))
        data_specs = [pl.BlockSpec((t, RET_QK), rmap(2)), pl.BlockSpec((t, RET_QK), rmap(3)),
                      pl.BlockSpec((t, RET_WIDTH), rmap(2)), pl.BlockSpec((t, RET_WIDTH), rmap(3)),
                      _const_spec(sd, (0, 0))]
        rope_specs = [pl.BlockSpec((t, RET_QK), lambda b, c: (c, 0))] * 3
        out_specs = [pl.BlockSpec((out_blk, RET_WIDTH), lambda b, c: (b * n_chunk + c, 0)),
                     pl.BlockSpec((None, *sd), lambda b, c: (b, 0, 0))]
        st_shape = (n_seq, *sd)
        extra_specs, extra_args, aliases = [], [], {}
        scratch = [pltpu.VMEM(sd, F32)]
        sem = ("arbitrary", "arbitrary")
    else:
        assert seq_len == t and n_seq % n_sub == 0
        grid = (n_seq // n_sub,)
        rmap = lambda cb: (lambda i: (blk0 + i, cb))
        st_spec, extra_specs, extra_args, aliases = _stacked_state_io(n_sub, sd, layer_i, st_prev, 13)
        data_specs = [pl.BlockSpec((rows_blk, RET_QK), rmap(2)), pl.BlockSpec((rows_blk, RET_QK), rmap(3)),
                      pl.BlockSpec((rows_blk, RET_WIDTH), rmap(2)), pl.BlockSpec((rows_blk, RET_WIDTH), rmap(3)),
                      st_spec]
        rope_specs = [_const_spec((t, RET_QK), (0, 0))] * 3
        out_specs = [pl.BlockSpec((rows_blk, RET_WIDTH), lambda i: (i, 0)), st_spec]
        st_shape = s0.shape
        scratch = []
        sem = ("arbitrary",)
    in_specs = [*data_specs, *rope_specs, *const_specs, *extra_specs]
    assert not aliases or list(aliases) == [len(in_specs) - 1]
    return pl.pallas_call(
        functools.partial(_ret_kernel, chained, t, n_sub),
        grid=grid,
        in_specs=in_specs,
        out_specs=out_specs,
        out_shape=[jax.ShapeDtypeStruct((n_rows // rows_blk * out_blk, RET_WIDTH), BF16),
                   jax.ShapeDtypeStruct(st_shape, F32)],
        scratch_shapes=scratch,
        input_output_aliases=aliases,
        compiler_params=_cparams(sem),
        name="retention",
    )(p, p, p, p, s0, cos, s_up, s_dn, *consts, *extra_args)


def _gla_tables(t, n_blk):
    cg = t * n_blk
    blk = np.arange(cg) // t
    same = blk[:, None] == blk[None, :]
    cum = (same & (np.arange(cg)[:, None] >= np.arange(cg)[None, :])).astype(np.float32)
    ones = (np.arange(t * GLA_DK)[:, None] // GLA_DK == np.arange(LANES)[None, :]).astype(np.float32)
    hm = np.repeat(np.eye(GLA_HEADS), GLA_DK, axis=1)
    hm = np.concatenate([hm, np.zeros((SUBLANES - GLA_HEADS, GLA_QK))], axis=0).astype(np.float32)
    return jnp.asarray(cum, BF16), jnp.asarray(ones, BF16), jnp.asarray(hm, F32)


LOG2E = 1.4426950408889634


def _gla_kernel(chained, t, n_blk, n_par, *refs):
    data = [refs[5 * s:5 * s + 5] for s in range(n_par)]
    s0_ref, wa_ref, ba_ref, ng_ref, cum_ref, ones_ref, hm_ref = refs[5 * n_par:5 * n_par + 7]
    rest = refs[5 * n_par + 7:]
    out_ref, st_ref, rt_scr, o_scr, *scratch = rest[-5:] if chained else rest[-4:]
    cg = t * n_blk
    nh, dk, dv = GLA_HEADS, GLA_DK, GLA_DV
    par = range(n_par)
    if chained:
        (s_scr,) = scratch

        @pl.when(pl.program_id(1) == 0)
        def _():
            for s in par:
                s_scr[s] = s0_ref[...]

    hm = hm_ref[...]
    q, ksc, vb, b, qe, ke, a_cols = [], [], [], [], [], [], []
    for s in par:
        q_ref, k_ref, v_ref, _, lr_ref = data[s]
        q.append(q_ref[...])
        ksc.append(k_ref[...] * (dk ** -0.5))
        vb.append(v_ref[...].astype(BF16))
        la = jax.nn.log_sigmoid(_dot(lr_ref[...].astype(BF16), wa_ref[...]) + ba_ref[...]) / GLA_TAU
        b.append(_dot_exact01(cum_ref[...], la) * LOG2E)
        last = b[s].reshape(n_blk, t, GLA_QK)[:, t - 1:t, :]
        bl = jnp.broadcast_to(last, (n_blk, t, GLA_QK)).reshape(cg, GLA_QK)
        tot = jnp.concatenate([last.reshape(n_blk, GLA_QK), jnp.zeros((LANES - n_blk, GLA_QK), F32)], axis=0)
        a_cols.append(jnp.exp2(tot).T)
        qe.append(q[s] * jnp.exp2(b[s]))
        ke.append(ksc[s] * jnp.exp2(bl - b[s]))

    row_t = {lo: lo + lax.broadcasted_iota(jnp.int32, (t - lo, GLA_QK), 0) for lo in range(0, t, SUBLANES)}
    for j in range(n_blk):
        rows = slice(j * t, (j + 1) * t)
        for s in par:
            qj, kj, bj = q[s][rows], ksc[s][rows], b[s][rows]
            for i in range(t):
                lo = i // SUBLANES * SUBLANES
                e = jnp.exp2(jnp.where(row_t[lo] >= i, bj[lo:] - bj[i:i + 1, :], NEG_BIG))
                prod = (qj[lo:] * kj[i:i + 1, :]) * e
                for h in range(nh):
                    ph = prod[:, h * dk:(h + 1) * dk]
                    if lo:
                        ph = jnp.concatenate([jnp.zeros((lo, dk), F32), ph], axis=0)
                    rt_scr[s, (j * nh + h) * t:(j * nh + h + 1) * t, i * dk:(i + 1) * dk] = ph.astype(rt_scr.dtype)
    intra = []
    for s in par:
        scores = _dot(rt_scr[s].astype(BF16), ones_ref[...])
        per_head = []
        for h in range(nh):
            pieces = []
            for j in range(n_blk):
                piece = scores[(j * nh + h) * t:(j * nh + h + 1) * t, :]
                pieces.append(pltpu.roll(piece, j * t, 1) if j else piece)
            pfull = jnp.concatenate(pieces, axis=0) if n_blk > 1 else pieces[0]
            per_head.append(_dot(pfull[:, :cg].astype(BF16), vb[s][:, h * dv:(h + 1) * dv]))
        intra.append(per_head)

    state = [s_scr[s] for s in par] if chained else [None]
    for j in range(n_blk):
        rows = slice(j * t, (j + 1) * t)
        for s in par:
            if not chained:
                state[s] = s0_ref[j]
            qs = jnp.concatenate([qe[s][rows] * hm[h:h + 1, :] for h in range(nh)], axis=0).astype(BF16)
            inter = _dot(qs, state[s].astype(BF16))
            for h in range(nh):
                o_scr[s, rows, h * dv:(h + 1) * dv] = inter[h * t:(h + 1) * t] + intra[s][h][rows]
            ks = jnp.concatenate([ke[s][rows] * hm[h:h + 1, :] for h in range(nh)], axis=0).astype(BF16)
            vs = jnp.concatenate([vb[s][rows, h * dv:(h + 1) * dv] for h in range(nh)], axis=0)
            state[s] = state[s] * a_cols[s][:, j:j + 1] + _dot_tn(ks, vs)
            if not chained:
                st_ref[j] = state[s]
    for s in par:
        if chained:
            s_scr[s] = state[s]
            st_ref[s] = state[s]
        o = o_scr[s]
        r = data[s][3][...]
        gated = []
        for h in range(nh):
            cols = slice(h * dv, (h + 1) * dv)
            oh = o[:, cols]
            on = oh * lax.rsqrt(jnp.mean(oh * oh, axis=-1, keepdims=True) + EPS) * ng_ref[...]
            rh = r[:, cols]
            gated.append(on * (rh * jax.nn.sigmoid(rh)))
        _store_rows(out_ref.at[s] if chained else out_ref, jnp.concatenate(gated, axis=1))


def _gla_call(p, row0, n_seq, seq_len, s0, wa, ba, ng, chained, t, n_blk, n_par=1, out_rows=None, layer_i=0,
              st_prev=None):
    cg = t * n_blk
    blk0 = row0 // cg
    assert row0 % cg == 0 and cg <= LANES
    out_blk = cg if out_rows is None else out_rows
    assert out_rows is None or seq_len == cg
    cum, ones, hm = _gla_tables(t, n_blk)
    consts = [wa, ba, ng, cum, ones, hm]
    const_specs = [_const_spec(c.shape, (0,) * c.ndim) for c in consts]
    sd = GLA_QK, GLA_DV
    lr_col = (2 * GLA_QK + 2 * GLA_V) // LANES
    rt_dtype = BF16 if t % (2 * SUBLANES) == 0 else F32
    if chained:
        assert n_seq % n_par == 0
        n_chunk = seq_len // cg
        grid = (n_seq // n_par, n_chunk)
        rmap = lambda s, cb: (lambda b, c: (blk0 + (b * n_par + s) * n_chunk + c, cb))
        s_spec = _const_spec(sd, (0, 0))
        out_specs = [pl.BlockSpec((n_par, out_blk, GLA_V), lambda b, c: (b, c, 0)),
                     pl.BlockSpec((n_par, *sd), lambda b, c: (b, 0, 0))]
        out_shape = [jax.ShapeDtypeStruct((n_seq, n_chunk * out_blk, GLA_V), BF16),
                     jax.ShapeDtypeStruct((n_seq, *sd), F32)]
        extra_specs, extra_args, aliases = [], [], {}
        scratch = [pltpu.VMEM((n_par, *sd), F32)]
        sem = ("arbitrary", "arbitrary")
    else:
        assert seq_len == t and n_seq % n_blk == 0 and n_par == 1
        grid = (n_seq // n_blk,)
        rmap = lambda s, cb: (lambda i: (blk0 + i, cb))
        s_spec, extra_specs, extra_args, aliases = _stacked_state_io(n_blk, sd, layer_i, st_prev, 12)
        out_specs = [pl.BlockSpec((cg, GLA_V), lambda i: (i, 0)), s_spec]
        out_shape = [jax.ShapeDtypeStruct((n_seq * seq_len, GLA_V), BF16), jax.ShapeDtypeStruct(s0.shape, F32)]
        scratch = []
        sem = ("arbitrary",)
    data_specs = []
    for s in range(n_par):
        data_specs += [pl.BlockSpec((cg, GLA_QK), rmap(s, 0)), pl.BlockSpec((cg, GLA_QK), rmap(s, 1)),
                       pl.BlockSpec((cg, GLA_V), rmap(s, 1)), pl.BlockSpec((cg, GLA_V), rmap(s, 2)),
                       pl.BlockSpec((cg, LANES), rmap(s, lr_col))]
    in_specs = [*data_specs, s_spec, *const_specs, *extra_specs]
    assert not aliases or list(aliases) == [len(in_specs) - 1]
    out, st = pl.pallas_call(
        functools.partial(_gla_kernel, chained, t, n_blk, n_par),
        grid=grid,
        in_specs=in_specs,
        out_specs=out_specs,
        out_shape=out_shape,
        scratch_shapes=[pltpu.VMEM((n_par, n_blk * GLA_HEADS * t, t * GLA_DK), rt_dtype),
                        pltpu.VMEM((n_par, cg, GLA_V), F32), *scratch],
        input_output_aliases=aliases,
        compiler_params=_cparams(sem),
        name="gla",
    )(*([p] * (5 * n_par)), s0, *consts, *extra_args)
    return out.reshape(-1, GLA_V), st


S5_TC_PROMPT = 256
S5_TC_SAMPLE = 256
RET_T_PROMPT = 256
RET_SUB_SAMPLE = 16
GLA_T_PROMPT = 16
GLA_BLK_PROMPT = 8
GLA_BLK_SAMPLE = 8
GLA_PAR_PROMPT = 2


def _mixer_ab(p, i, prm, st_s5_re, st_s5_im, st_ret, s_ret_prev):
    zeros_h = jnp.zeros((1, S5_CH), F32)
    a_m, hre_m, him_m = _s5_call(p, ROW0_META, 1, N_META, zeros_h, zeros_h, prm, True, N_META, out_rows=DENSE_TM)
    a_p, hre_p, him_p = _s5_call(p, 0, BATCH, SEQ, hre_m[0], him_m[0], prm, True, S5_TC_PROMPT)
    h0re = jnp.repeat(st_s5_re[i].reshape(DEC_BATCH, S5_CH), DEC_SEQ, axis=0)
    h0im = jnp.repeat(st_s5_im[i].reshape(DEC_BATCH, S5_CH), DEC_SEQ, axis=0)
    a_s, hre_s, him_s = _s5_call(p, ROW0_SAMPLE, DEC_BATCH, DEC_SEQ, h0re, h0im, prm, False, S5_TC_SAMPLE)
    last = lambda h: h.reshape(DEC_BATCH, DEC_SEQ, S5_CH)[:, DEC_SEQ - 1].reshape(DEC_BATCH, S5_GROUPS, S5_STATE)

    zeros_s = jnp.zeros((RET_QK, RET_DV), F32)
    b_m, s_m = _ret_call(p, ROW0_META, 1, N_META, zeros_s, 0, True, N_META, out_rows=DENSE_TM)
    b_p, s_p = _ret_call(p, 0, BATCH, SEQ, s_m[0], N_META, True, RET_T_PROMPT)
    b_s, s_ret = _ret_call(p, ROW0_SAMPLE, DEC_BATCH, DEC_SEQ, st_ret, PAST_LEN, False, DEC_SEQ, RET_SUB_SAMPLE,
                           layer_i=i, st_prev=s_ret_prev)
    states = dict(
        p_s5_re=hre_p.reshape(BATCH, S5_GROUPS, S5_STATE), p_s5_im=him_p.reshape(BATCH, S5_GROUPS, S5_STATE),
        p_ret=s_p.reshape(BATCH, RET_HEADS, RET_DK, RET_DV), s_s5_re=last(hre_s), s_s5_im=last(him_s))
    return [(a_p, a_s, a_m), (b_p, b_s, b_m)], states, s_ret


def _mixer_gla(p, i, wa, ba, ng, st_gla, s_gla_prev):
    zeros_s = jnp.zeros((GLA_QK, GLA_DV), F32)
    o_m, s_m = _gla_call(p, ROW0_META, 1, N_META, zeros_s, wa, ba, ng, True, N_META, 1, out_rows=DENSE_TM)
    o_p, s_p = _gla_call(p, 0, BATCH, SEQ, s_m[0], wa, ba, ng, True, GLA_T_PROMPT, GLA_BLK_PROMPT,
                         n_par=GLA_PAR_PROMPT)
    o_s, s_gla = _gla_call(p, ROW0_SAMPLE, DEC_BATCH, DEC_SEQ, st_gla, wa, ba, ng, False, DEC_SEQ, GLA_BLK_SAMPLE,
                           layer_i=i, st_prev=s_gla_prev)
    states = dict(p_gla=s_p.reshape(BATCH, GLA_HEADS, GLA_DK, GLA_DV))
    return [(o_p, o_s, o_m)], states, s_gla


def kernel(x_prompt, x_sample, state_s5_re, state_s5_im, state_ret, state_gla, meta_tokens, norm_ffn1, norm_mix,
           norm_ffn2, norm_final, ffn1_w_gu, ffn1_w_down, ffn2_w_gu, ffn2_w_down, ab_w_in, ab_w_out, s5_a_re,
           s5_a_im, s5_log_dt, s5_b_re, s5_b_im, s5_c_re, s5_c_im, s5_d, s5_w_glu, gla_w_in, gla_w_alpha2,
           gla_b_alpha, gla_norm, gla_w_out):
    bf = lambda w: w.astype(BF16)
    ffn1_gu, ffn1_dn, ffn2_gu, ffn2_dn = bf(ffn1_w_gu), bf(ffn1_w_down), bf(ffn2_w_gu), bf(ffn2_w_down)
    ab_in, ab_out, gla_out = bf(ab_w_in), bf(ab_w_out), bf(gla_w_out)
    gla_in = bf(jnp.pad(gla_w_in, ((0, 0), (0, 0), (0, GLA_IN_PAD - GLA_IN))))
    gla_wa = bf(jnp.pad(gla_w_alpha2, ((0, 0), (0, LANES - GLA_LOWRANK), (0, 0))))
    n1 = norm_ffn1.reshape(DEPTH, 1, D_MODEL)
    nm = norm_mix.reshape(DEPTH, 1, D_MODEL)
    n2 = norm_ffn2.reshape(DEPTH, 1, D_MODEL)

    meta_pad = jnp.pad(meta_tokens.astype(x_prompt.dtype), ((0, DENSE_TM - N_META), (0, 0)))
    xs = [x_prompt.reshape(ROWS_PROMPT, D_MODEL), x_sample.reshape(ROWS_SAMPLE, D_MODEL), meta_pad]
    st_ret = state_ret.reshape(N_EVEN, DEC_BATCH, RET_QK, RET_DV)
    st_gla = state_gla.reshape(N_ODD, DEC_BATCH, GLA_QK, GLA_DV)
    s_ret = s_gla = None
    collected = {}
    for layer in range(DEPTH):
        i = layer // 2
        last = layer == DEPTH - 1
        g_final = norm_final.reshape(1, D_MODEL) if last else None
        if layer % 2 == 0:
            x1, p = _pre_call(xs, n1, ffn1_gu, ffn1_dn, nm, ab_in, layer, i)
            prm = _s5_params(s5_a_re[i], s5_a_im[i], s5_log_dt[i], s5_b_re[i], s5_b_im[i], s5_c_re[i],
                             s5_c_im[i], s5_d[i], s5_w_glu[i])
            mixes, states, s_ret = _mixer_ab(p, i, prm, state_s5_re, state_s5_im, st_ret, s_ret)
            x = _post_call(x1, mixes, ab_out, n2, ffn2_gu, ffn2_dn, layer, i, g_final)
        else:
            x1, p = _pre_call(xs, n1, ffn1_gu, ffn1_dn, nm, gla_in, layer, i)
            mixes, states, s_gla = _mixer_gla(p, i, gla_wa[i], gla_b_alpha[i].reshape(1, GLA_QK),
                                              gla_norm[i].reshape(1, GLA_DV), st_gla, s_gla)
            x = _post_call(x1, mixes, gla_out, n2, ffn2_gu, ffn2_dn, layer, i, g_final)
        xs = [x]
        for name, val in states.items():
            collected.setdefault(name, []).append(val)
    out = {name: jnp.stack(vals) for name, vals in collected.items()}
    y_prompt, y_rest = x
    return (y_prompt.reshape(BATCH, SEQ, D_MODEL), y_rest[:ROWS_SAMPLE].reshape(DEC_BATCH, DEC_SEQ, D_MODEL),
            out["p_s5_re"], out["p_s5_im"], out["p_ret"], out["p_gla"], out["s_s5_re"], out["s_s5_im"],
            s_ret.reshape(N_EVEN, DEC_BATCH, RET_HEADS, RET_DK, RET_DV),
            s_gla.reshape(N_ODD, DEC_BATCH, GLA_HEADS, GLA_DK, GLA_DV))
```

```python
import functools
import math

import numpy as np
import jax
import jax.numpy as jnp
from jax import lax
from jax.experimental import pallas as pl
from jax.experimental.pallas import tpu as pltpu

F32 = jnp.float32
BF16 = jnp.bfloat16

D_MODEL = 1024
BATCH = 8
SEQ = 2048
DEPTH = 4
DEC_BATCH = 128
DEC_SEQ = 8
PAST_LEN = 16384
N_META = 16
N_EVEN = (DEPTH + 1) // 2
N_ODD = DEPTH // 2
S5_WIDTH = D_MODEL // 2
S5_GROUP = 16
S5_GROUPS = S5_WIDTH // S5_GROUP
S5_STATE = 64
S5_CH = S5_GROUPS * S5_STATE
RET_HEADS = 4
RET_DK = D_MODEL // 16
RET_DV = 2 * RET_DK
RET_QK = RET_HEADS * RET_DK
RET_WIDTH = RET_HEADS * RET_DV
AB_IN = S5_WIDTH + 2 * RET_QK + 2 * RET_WIDTH
AB_OUT = S5_WIDTH + RET_WIDTH
GLA_HEADS = 4
GLA_DK = D_MODEL // (2 * GLA_HEADS)
GLA_DV = D_MODEL // GLA_HEADS
GLA_QK = GLA_HEADS * GLA_DK
GLA_V = GLA_HEADS * GLA_DV
GLA_LOWRANK = 16
GLA_TAU = 16.0
GLA_IN = 2 * GLA_QK + 2 * GLA_V + GLA_LOWRANK
LANES = 128
GLA_IN_PAD = 2 * GLA_QK + 2 * GLA_V + LANES
D_FF = 128 * ((8 * D_MODEL // 3 + 127) // 128)
EPS = 1e-6
ROPE_BASE = 10000.0
NEG_BIG = -1e30

ROWS_PROMPT = BATCH * SEQ
ROWS_SAMPLE = DEC_BATCH * DEC_SEQ
ROW0_SAMPLE = ROWS_PROMPT
ROW0_META = ROWS_PROMPT + ROWS_SAMPLE
ROWS = ROW0_META + N_META

VMEM_LIMIT = 56 * 1024 * 1024
DENSE_TM = 256
TILES_PROMPT = ROWS_PROMPT // DENSE_TM
TILES_SAMPLE = ROWS_SAMPLE // DENSE_TM
N_TILES = TILES_PROMPT + TILES_SAMPLE + 1
ROWS_PAD = N_TILES * DENSE_TM
ROWS_REST = ROWS_PAD - ROWS_PROMPT


def _cparams(sem):
    return pltpu.CompilerParams(dimension_semantics=sem, vmem_limit_bytes=VMEM_LIMIT)


def _dot(a, b):
    return jnp.dot(a, b, preferred_element_type=F32)


def _dot_tn(a, b):
    return lax.dot_general(a, b, (((0,), (0,)), ((), ())), preferred_element_type=F32)


def _dot_nt(a, b):
    return lax.dot_general(a, b, (((1,), (1,)), ((), ())), preferred_element_type=F32)


def _dot_exact01(m_bf, x):
    h1 = x.astype(BF16)
    r1 = x - h1.astype(F32)
    h2 = r1.astype(BF16)
    r2 = r1 - h2.astype(F32)
    h3 = r2.astype(BF16)
    return _dot(m_bf, h1) + _dot(m_bf, h2) + _dot(m_bf, h3)


def _rms(x, g):
    return x * lax.rsqrt(jnp.mean(x * x, axis=-1, keepdims=True) + EPS) * g


def _swiglu_half(x, g, wg, wu, wd):
    h = _rms(x, g).astype(BF16)
    gate = _dot(h, wg)
    up = _dot(h, wu)
    act = (gate * jax.nn.sigmoid(gate) * up).astype(BF16)
    return x + 0.5 * _dot(act, wd)


def _store_rows(out_ref, val):
    rows = val.shape[0]
    out_ref[0:rows, :] = val.astype(out_ref.dtype)
    if out_ref.shape[0] > rows:
        out_ref[rows:, :] = jnp.zeros((out_ref.shape[0] - rows, out_ref.shape[1]), out_ref.dtype)


def _const_spec(shape, index):
    return pl.BlockSpec(shape, lambda *_: index, pipeline_mode=pl.Buffered(1))


def _group_specs(width):
    tm = DENSE_TM
    return [pl.BlockSpec((tm, width), lambda i: (jnp.minimum(i, TILES_PROMPT - 1), 0)),
            pl.BlockSpec((tm, width), lambda i: (jnp.clip(i - TILES_PROMPT, 0, TILES_SAMPLE - 1), 0)),
            pl.BlockSpec((tm, width), lambda i: (0, 0))]


def _pick_group(refs):
    if len(refs) == 1:
        return refs[0][...]
    i = pl.program_id(0)
    return jnp.where(i < TILES_PROMPT, refs[0][...],
                     jnp.where(i < TILES_PROMPT + TILES_SAMPLE, refs[1][...], refs[2][...]))


def _pre_kernel(n_x, *refs):
    x_refs = refs[:n_x]
    g1_ref, wg_ref, wu_ref, wd_ref, g2_ref, win_ref, x1_ref, p_ref = refs[n_x:]
    x1 = _swiglu_half(_pick_group(x_refs), g1_ref[...], wg_ref[...], wu_ref[...], wd_ref[...])
    x1_ref[...] = x1
    p_ref[...] = _dot(_rms(x1, g2_ref[...]).astype(BF16), win_ref[...])


def _pre_call(xs, g1, w_gu, w_down, g2, w_in, layer, mix_idx):
    n_in = w_in.shape[-1]
    tm = DENSE_TM
    row = lambda i: (i, 0)
    x_specs = [pl.BlockSpec((tm, D_MODEL), row)] if len(xs) == 1 else _group_specs(D_MODEL)
    return pl.pallas_call(
        functools.partial(_pre_kernel, len(xs)),
        grid=(N_TILES,),
        in_specs=[
            *x_specs,
            _const_spec((None, 1, D_MODEL), (layer, 0, 0)),
            _const_spec((None, D_MODEL, D_FF), (layer, 0, 0)),
            _const_spec((None, D_MODEL, D_FF), (layer, 0, 1)),
            _const_spec((None, D_FF, D_MODEL), (layer, 0, 0)),
            _const_spec((None, 1, D_MODEL), (layer, 0, 0)),
            _const_spec((None, D_MODEL, n_in), (mix_idx, 0, 0)),
        ],
        out_specs=[pl.BlockSpec((tm, D_MODEL), row), pl.BlockSpec((tm, n_in), row)],
        out_shape=[jax.ShapeDtypeStruct((ROWS_PAD, D_MODEL), F32), jax.ShapeDtypeStruct((ROWS_PAD, n_in), F32)],
        compiler_params=_cparams(("arbitrary",)),
        name="pre",
    )(*xs, g1, w_gu, w_gu, w_down, g2, w_in)


def _post_kernel(n_mix, final, *refs):
    x1_ref = refs[0]
    mix_refs = refs[1:1 + 3 * n_mix]
    wout_ref, g_ref, wg_ref, wu_ref, wd_ref = refs[1 + 3 * n_mix:6 + 3 * n_mix]
    rest = refs[6 + 3 * n_mix:]
    x2 = x1_ref[...]
    width = AB_OUT // n_mix
    for i in range(n_mix):
        x2 = x2 + _dot(_pick_group(mix_refs[3 * i:3 * i + 3]), wout_ref[i * width:(i + 1) * width, :])
    y = _swiglu_half(x2, g_ref[...], wg_ref[...], wu_ref[...], wd_ref[...])
    if final:
        gf_ref, yp_ref, yr_ref = rest
        y = _rms(y, gf_ref[...])
        step = pl.program_id(0)

        @pl.when(step < TILES_PROMPT)
        def _():
            yp_ref[...] = y

        @pl.when(step >= TILES_PROMPT)
        def _():
            yr_ref[...] = y
    else:
        (o_ref,) = rest
        o_ref[...] = y


def _post_call(x1, mixes, w_out, g, w_gu, w_down, layer, mix_idx, g_final):
    tm = DENSE_TM
    row = lambda i: (i, 0)
    final = g_final is not None
    in_specs = [pl.BlockSpec((tm, D_MODEL), row)]
    for triple in mixes:
        in_specs += _group_specs(triple[0].shape[1])
    in_specs += [
        _const_spec((None, AB_OUT, D_MODEL), (mix_idx, 0, 0)),
        _const_spec((None, 1, D_MODEL), (layer, 0, 0)),
        _const_spec((None, D_MODEL, D_FF), (layer, 0, 0)),
        _const_spec((None, D_MODEL, D_FF), (layer, 0, 1)),
        _const_spec((None, D_FF, D_MODEL), (layer, 0, 0)),
    ]
    args = [x1, *[m for triple in mixes for m in triple], w_out, g, w_gu, w_gu, w_down]
    if final:
        in_specs.append(_const_spec((1, D_MODEL), (0, 0)))
        args.append(g_final)
        out_specs = [pl.BlockSpec((tm, D_MODEL), lambda i: (jnp.minimum(i, TILES_PROMPT - 1), 0)),
                     pl.BlockSpec((tm, D_MODEL), lambda i: (jnp.maximum(i - TILES_PROMPT, 0), 0))]
        out_shape = [jax.ShapeDtypeStruct((ROWS_PROMPT, D_MODEL), F32),
                     jax.ShapeDtypeStruct((ROWS_REST, D_MODEL), F32)]
    else:
        out_specs = pl.BlockSpec((tm, D_MODEL), row)
        out_shape = jax.ShapeDtypeStruct((ROWS_PAD, D_MODEL), F32)
    return pl.pallas_call(
        functools.partial(_post_kernel, len(mixes), final),
        grid=(N_TILES,),
        in_specs=in_specs,
        out_specs=out_specs,
        out_shape=out_shape,
        compiler_params=_cparams(("arbitrary",)),
        name="post",
    )(*args)


S5_HALF_IN = S5_WIDTH // 2
S5_HALF_CH = S5_CH // 2
SUBLANES = 8


def _s5_kernel(chained, tc, u_ref, h0re_ref, h0im_ref, bcat_ref, cre_ref, cim_ref, ast_re_ref, ast_im_ref,
               apw_re_ref, apw_im_ref, d_ref, wglu_ref, out_ref, hre_out, him_out, xre, xim, *carry):
    nb = tc // SUBLANES
    u = u_ref[...]
    ub = u.astype(BF16)
    for hf in range(2):
        xh = _dot(ub[:, hf * S5_HALF_IN:(hf + 1) * S5_HALF_IN], bcat_ref[hf])
        xre[:, hf * S5_HALF_CH:(hf + 1) * S5_HALF_CH] = xh[:, :S5_HALF_CH]
        xim[:, hf * S5_HALF_CH:(hf + 1) * S5_HALF_CH] = xh[:, S5_HALF_CH:]

    sr = xre[...].reshape(nb, SUBLANES, S5_CH)
    si = xim[...].reshape(nb, SUBLANES, S5_CH)
    rowi = lax.broadcasted_iota(jnp.int32, (nb, SUBLANES, S5_CH), 1)
    for step, d in enumerate((1, 2, 4)):
        ar = ast_re_ref[step:step + 1, :][None]
        ai = ast_im_ref[step:step + 1, :][None]
        pr = pltpu.roll(sr, d, 1)
        pi = pltpu.roll(si, d, 1)
        keep = rowi >= d
        sr, si = (sr + jnp.where(keep, ar * pr - ai * pi, 0.0),
                  si + jnp.where(keep, ar * pi + ai * pr, 0.0))
    apr = apw_re_ref[...]
    api = apw_im_ref[...]
    if chained:
        cre, cim = carry
        xre[...] = sr.reshape(tc, S5_CH)
        xim[...] = si.reshape(tc, S5_CH)

        @pl.when(pl.program_id(1) == 0)
        def _():
            cre[...] = h0re_ref[...]
            cim[...] = h0im_ref[...]

        def group(r, c):
            hr, hi = c
            rows = pl.ds(pl.multiple_of(r * SUBLANES, SUBLANES), SUBLANES)
            nr = xre[rows, :] + apr * hr - api * hi
            ni = xim[rows, :] + apr * hi + api * hr
            xre[rows, :] = nr
            xim[rows, :] = ni
            return nr[SUBLANES - 1:, :], ni[SUBLANES - 1:, :]

        hr, hi = lax.fori_loop(0, nb, group, (cre[...], cim[...]), unroll=min(nb, 4))
        cre[...] = hr
        cim[...] = hi
        hre_out[...] = hr
        him_out[...] = hi
    else:
        h0r = h0re_ref[...].reshape(nb, SUBLANES, S5_CH)
        h0i = h0im_ref[...].reshape(nb, SUBLANES, S5_CH)
        fr = (sr + apr[None] * h0r - api[None] * h0i).reshape(tc, S5_CH)
        fi = (si + apr[None] * h0i + api[None] * h0r).reshape(tc, S5_CH)
        xre[...] = fr
        xim[...] = fi
        hre_out[...] = fr
        him_out[...] = fi

    ys = []
    for hf in range(2):
        cols = slice(hf * S5_HALF_CH, (hf + 1) * S5_HALF_CH)
        ys.append(_dot(xre[:, cols].astype(BF16), cre_ref[hf]) + _dot(xim[:, cols].astype(BF16), cim_ref[hf]))
    y = jnp.concatenate(ys, axis=1) + d_ref[...] * u
    z = jax.nn.gelu(y)
    _store_rows(out_ref, z * jax.nn.sigmoid(_dot(z.astype(BF16), wglu_ref[...])))


def _s5_call(p, row0, n_seq, seq_len, h0re, h0im, prm, chained, tc, out_rows=None):
    n_rows = n_seq * seq_len
    blk0 = row0 // tc
    assert row0 % tc == 0 and n_rows % tc == 0
    out_blk = tc if out_rows is None else out_rows
    assert out_rows is None or n_rows == tc
    consts = [prm["bcat"], prm["cre"], prm["cim"], prm["ast_re"], prm["ast_im"], prm["apw_re"], prm["apw_im"],
              prm["d"], prm["wglu"]]
    const_specs = [_const_spec(c.shape, (0,) * c.ndim) for c in consts]
    scratch = [pltpu.VMEM((tc, S5_CH), F32), pltpu.VMEM((tc, S5_CH), F32)]
    if chained:
        n_chunk = seq_len // tc
        grid = (n_seq, n_chunk)
        u_spec = pl.BlockSpec((tc, S5_WIDTH), lambda b, c: (blk0 + b * n_chunk + c, 0))
        h_specs = [_const_spec((1, S5_CH), (0, 0))] * 2
        out_specs = [pl.BlockSpec((out_blk, S5_WIDTH), lambda b, c: (b * n_chunk + c, 0)),
                     pl.BlockSpec((None, 1, S5_CH), lambda b, c: (b, 0, 0)),
                     pl.BlockSpec((None, 1, S5_CH), lambda b, c: (b, 0, 0))]
        out_shape = [jax.ShapeDtypeStruct((n_rows // tc * out_blk, S5_WIDTH), BF16),
                     jax.ShapeDtypeStruct((n_seq, 1, S5_CH), F32), jax.ShapeDtypeStruct((n_seq, 1, S5_CH), F32)]
        scratch += [pltpu.VMEM((1, S5_CH), F32), pltpu.VMEM((1, S5_CH), F32)]
        sem = ("arbitrary", "arbitrary")
    else:
        assert seq_len == SUBLANES
        grid = (n_rows // tc,)
        u_spec = pl.BlockSpec((tc, S5_WIDTH), lambda i: (blk0 + i, 0))
        h_specs = [pl.BlockSpec((tc, S5_CH), lambda i: (i, 0))] * 2
        out_specs = [pl.BlockSpec((tc, S5_WIDTH), lambda i: (i, 0)),
                     pl.BlockSpec((tc, S5_CH), lambda i: (i, 0)), pl.BlockSpec((tc, S5_CH), lambda i: (i, 0))]
        out_shape = [jax.ShapeDtypeStruct((n_rows, S5_WIDTH), BF16),
                     jax.ShapeDtypeStruct((n_rows, S5_CH), F32), jax.ShapeDtypeStruct((n_rows, S5_CH), F32)]
        sem = ("arbitrary",)
    return pl.pallas_call(
        functools.partial(_s5_kernel, chained, tc),
        grid=grid,
        in_specs=[u_spec, *h_specs, *const_specs],
        out_specs=out_specs,
        out_shape=out_shape,
        scratch_shapes=scratch,
        compiler_params=_cparams(sem),
        name="s5",
    )(p, h0re, h0im, *consts)


S5_SLABS = S5_WIDTH // LANES


def _s5_batch_kernel(n_seq, tt, *refs):
    u_refs = refs[:n_seq]
    (h0re_ref, h0im_ref, bcat_ref, cre_ref, cim_ref, are_ref, aim_ref, d_ref, wglu_ref,
     out_ref, hre_out, him_out, u_tb, o_tb, xre, xim, cre, cim) = refs[n_seq:]
    @pl.when(pl.program_id(0) == 0)
    def _():
        cre[...] = jnp.broadcast_to(h0re_ref[...], (n_seq, S5_CH))
        cim[...] = jnp.broadcast_to(h0im_ref[...], (n_seq, S5_CH))

    for b in range(n_seq):
        ub = u_refs[b][...]
        for sl in range(S5_SLABS):
            u_tb[sl, pl.ds(b, tt, stride=n_seq), :] = ub[:, sl * LANES:(sl + 1) * LANES]
    u = jnp.concatenate([u_tb[sl] for sl in range(S5_SLABS)], axis=1)
    ubf = u.astype(BF16)
    for hf in range(2):
        xh = _dot(ubf[:, hf * S5_HALF_IN:(hf + 1) * S5_HALF_IN], bcat_ref[hf])
        xre[:, hf * S5_HALF_CH:(hf + 1) * S5_HALF_CH] = xh[:, :S5_HALF_CH]
        xim[:, hf * S5_HALF_CH:(hf + 1) * S5_HALF_CH] = xh[:, S5_HALF_CH:]

    for hf in range(2):
        cols = slice(hf * S5_HALF_CH, (hf + 1) * S5_HALF_CH)
        ar = are_ref[:, cols]
        ai = aim_ref[:, cols]

        def step(t, c):
            hr, hi = c
            rw = pl.ds(pl.multiple_of(t * n_seq, n_seq), n_seq)
            nr = xre[rw, cols] + (ar * hr - ai * hi)
            ni = xim[rw, cols] + (ar * hi + ai * hr)
            xre[rw, cols] = nr
            xim[rw, cols] = ni
            return nr, ni

        hr, hi = lax.fori_loop(0, tt, step, (cre[:, cols], cim[:, cols]), unroll=4)
        cre[:, cols] = hr
        cim[:, cols] = hi
        hre_out[:, cols] = hr
        him_out[:, cols] = hi

    ys = []
    for hf in range(2):
        cols = slice(hf * S5_HALF_CH, (hf + 1) * S5_HALF_CH)
        ys.append(_dot(xre[:, cols].astype(BF16), cre_ref[hf]) + _dot(xim[:, cols].astype(BF16), cim_ref[hf]))
    y = jnp.concatenate(ys, axis=1) + d_ref[...] * u
    z = jax.nn.gelu(y)
    o = z * jax.nn.sigmoid(_dot(z.astype(BF16), wglu_ref[...]))
    for sl in range(S5_SLABS):
        o_tb[sl] = o[:, sl * LANES:(sl + 1) * LANES]
    for b in range(n_seq):
        ob = jnp.concatenate([o_tb[sl, pl.ds(b, tt, stride=n_seq), :] for sl in range(S5_SLABS)], axis=1)
        out_ref[b] = ob.astype(out_ref.dtype)


def _s5_batch_call(p, row0, n_seq, seq_len, h0re, h0im, prm, tt):
    assert n_seq == SUBLANES and seq_len % tt == 0 and row0 % tt == 0
    n_chunk = seq_len // tt
    blk0 = row0 // tt
    a8 = lambda a: jnp.broadcast_to(a[:1], (SUBLANES, S5_CH))
    consts = [prm["bcat"], prm["cre"], prm["cim"], a8(prm["apw_re"]), a8(prm["apw_im"]), prm["d"], prm["wglu"]]
    const_specs = [_const_spec(c.shape, (0,) * c.ndim) for c in consts]
    u_specs = [pl.BlockSpec((tt, S5_WIDTH), functools.partial(lambda b, c: (blk0 + b * n_chunk + c, 0), b))
               for b in range(n_seq)]
    rows = n_seq * tt
    out, hre, him = pl.pallas_call(
        functools.partial(_s5_batch_kernel, n_seq, tt),
        grid=(n_chunk,),
        in_specs=[*u_specs, _const_spec((1, S5_CH), (0, 0)), _const_spec((1, S5_CH), (0, 0)), *const_specs],
        out_specs=[pl.BlockSpec((n_seq, tt, S5_WIDTH), lambda c: (0, c, 0)),
                   pl.BlockSpec((n_seq, S5_CH), lambda c: (0, 0)), pl.BlockSpec((n_seq, S5_CH), lambda c: (0, 0))],
        out_shape=[jax.ShapeDtypeStruct((n_seq, seq_len, S5_WIDTH), BF16),
                   jax.ShapeDtypeStruct((n_seq, S5_CH), F32), jax.ShapeDtypeStruct((n_seq, S5_CH), F32)],
        scratch_shapes=[pltpu.VMEM((S5_SLABS, rows, LANES), F32), pltpu.VMEM((S5_SLABS, rows, LANES), F32),
                        pltpu.VMEM((rows, S5_CH), F32), pltpu.VMEM((rows, S5_CH), F32),
                        pltpu.VMEM((n_seq, S5_CH), F32), pltpu.VMEM((n_seq, S5_CH), F32)],
        compiler_params=_cparams(("arbitrary",)),
        name="s5_batch",
    )(*([p] * n_seq), h0re, h0im, *consts)
    return out.reshape(n_seq * seq_len, S5_WIDTH), hre, him


def _s5_params(a_re, a_im, log_dt, b_re, b_im, c_re, c_im, d_skip, w_glu):
    dt = jnp.exp(log_dt)[:, None]
    mag = jnp.exp(dt * a_re)
    abar_re, abar_im = mag * jnp.cos(dt * a_im), mag * jnp.sin(dt * a_im)
    den = a_re * a_re + a_im * a_im
    num_re = abar_re - 1.0
    f_re = (num_re * a_re + abar_im * a_im) / den
    f_im = (abar_im * a_re - num_re * a_im) / den
    bbar_re = f_re[..., None] * b_re - f_im[..., None] * b_im
    bbar_im = f_re[..., None] * b_im + f_im[..., None] * b_re

    def block_diag_in(w):
        w = w.reshape(2, S5_GROUPS // 2, S5_STATE, S5_GROUP)
        eye = jnp.eye(S5_GROUPS // 2, dtype=F32)
        return jnp.einsum("hgpn,gk->hgnkp", w, eye).reshape(2, S5_HALF_IN, S5_HALF_CH)

    def block_diag_out(w):
        w = w.reshape(2, S5_GROUPS // 2, S5_GROUP, S5_STATE)
        eye = jnp.eye(S5_GROUPS // 2, dtype=F32)
        return jnp.einsum("hgnp,gk->hgpkn", w, eye).reshape(2, S5_HALF_CH, S5_HALF_IN)

    bcat = jnp.concatenate([block_diag_in(bbar_re), block_diag_in(bbar_im)], axis=-1).astype(BF16)
    ar, ai = abar_re.reshape(1, S5_CH), abar_im.reshape(1, S5_CH)
    pows_re, pows_im = [ar], [ai]
    for _ in range(SUBLANES - 1):
        pr, pi = pows_re[-1], pows_im[-1]
        pows_re.append(pr * ar - pi * ai)
        pows_im.append(pr * ai + pi * ar)
    zeros = jnp.zeros((SUBLANES - 3, S5_CH), F32)
    return dict(
        bcat=bcat,
        cre=block_diag_out(c_re).astype(BF16),
        cim=block_diag_out(-c_im).astype(BF16),
        ast_re=jnp.concatenate([pows_re[0], pows_re[1], pows_re[3], zeros], axis=0),
        ast_im=jnp.concatenate([pows_im[0], pows_im[1], pows_im[3], zeros], axis=0),
        apw_re=jnp.concatenate(pows_re, axis=0),
        apw_im=jnp.concatenate(pows_im, axis=0),
        d=d_skip.reshape(1, S5_WIDTH),
        wglu=w_glu.astype(BF16),
    )


def _ret_tables(t):
    gam = np.log(1.0 - 2.0 ** (-5.0 - np.arange(RET_HEADS, dtype=np.float64)))
    tt = np.arange(t, dtype=np.float64)
    dq = np.exp(gam[:, None] * (tt[None, :] + 1.0))
    dq = np.broadcast_to(dq.reshape(RET_HEADS * t, 1), (RET_HEADS * t, RET_DV))
    diff = tt[:, None] - tt[None, :]
    dm = np.where(diff >= 0, np.exp(gam[:, None, None] * np.maximum(diff, 0.0)[None]), 0.0)
    dk = np.exp(gam[:, None] * (t - 1.0 - tt[None, :]))
    dk = np.repeat(dk.T, RET_DK, axis=1)
    ds = np.repeat(np.exp(gam * t), RET_DK)[:, None] * np.ones((1, RET_DV))
    hm = np.repeat(np.eye(RET_HEADS), RET_DK, axis=1)
    f = lambda a: jnp.asarray(np.ascontiguousarray(a), F32)
    return dict(dq=f(dq), dm=f(dm.reshape(RET_HEADS * t, t)), dk=f(dk), ds=f(ds),
                hm=f(np.concatenate([hm, np.zeros((SUBLANES - RET_HEADS, RET_QK))], axis=0)))


def _rope_tables(pos):
    half = RET_DK // 2
    inv_freq = 1.0 / (ROPE_BASE ** (jnp.arange(half, dtype=F32) / half))
    ang = pos.astype(F32)[:, None] * inv_freq[None, :]
    cos, sin = jnp.cos(ang), jnp.sin(ang)
    zero = jnp.zeros_like(sin)
    tile = lambda a, b: jnp.tile(jnp.concatenate([a, b], axis=1), (1, RET_HEADS))
    return tile(cos, cos), tile(-sin, zero), tile(zero, sin)


def _ret_chunk(t, q, k, v, g, s, cos, s_up, s_dn, dq, dm, dk, ds, hm):
    half = RET_DK // 2

    def rope(x):
        return x * cos + pltpu.roll(x, RET_QK - half, 1) * s_up + pltpu.roll(x, half, 1) * s_dn

    qr = rope(q)
    kr = rope(k) * (RET_DK ** -0.5)
    qs = jnp.concatenate([qr * hm[h:h + 1, :] for h in range(RET_HEADS)], axis=0).astype(BF16)
    inter = _dot(qs, s.astype(BF16)) * dq
    prob = (_dot_nt(qs, kr.astype(BF16)) * dm).astype(BF16)
    vb = v.astype(BF16)
    outs = []
    for h in range(RET_HEADS):
        rows = slice(h * t, (h + 1) * t)
        cols = slice(h * RET_DV, (h + 1) * RET_DV)
        o = inter[rows] + _dot(prob[rows], vb[:, cols])
        mu = jnp.mean(o, axis=-1, keepdims=True)
        oc = o - mu
        var = jnp.mean(oc * oc, axis=-1, keepdims=True)
        gh = g[:, cols]
        outs.append(oc * lax.rsqrt(var + EPS) * (gh * jax.nn.sigmoid(gh)))
    kd = kr * dk
    ks = jnp.concatenate([kd * hm[h:h + 1, :] for h in range(RET_HEADS)], axis=0).astype(BF16)
    vs = jnp.concatenate([vb[:, h * RET_DV:(h + 1) * RET_DV] for h in range(RET_HEADS)], axis=0)
    s_new = s * ds + _dot_tn(ks, vs)
    return jnp.concatenate(outs, axis=1), s_new


def _ret_kernel(chained, t, n_sub, q_ref, k_ref, v_ref, g_ref, s0_ref, cos_ref, sup_ref, sdn_ref, dq_ref, dm_ref,
                dk_ref, ds_ref, hm_ref, *rest):
    out_ref, st_ref, *scratch = rest[-3:] if chained else rest[-2:]
    tabs = (dq_ref[...], dm_ref[...], dk_ref[...], ds_ref[...], hm_ref[...])
    if chained:
        (s_scr,) = scratch

        @pl.when(pl.program_id(1) == 0)
        def _():
            s_scr[...] = s0_ref[...]

        o, s_new = _ret_chunk(t, q_ref[...], k_ref[...], v_ref[...], g_ref[...], s_scr[...],
                              cos_ref[...], sup_ref[...], sdn_ref[...], *tabs)
        _store_rows(out_ref, o)
        s_scr[...] = s_new
        st_ref[...] = s_new
    else:
        def one(i, carry):
            rows = pl.ds(pl.multiple_of(i * t, t), t)
            o, s_new = _ret_chunk(t, q_ref[rows, :], k_ref[rows, :], v_ref[rows, :], g_ref[rows, :], s0_ref[i],
                                  cos_ref[...], sup_ref[...], sdn_ref[...], *tabs)
            out_ref[rows, :] = o.astype(out_ref.dtype)
            st_ref[i] = s_new
            return carry

        lax.fori_loop(0, n_sub, one, 0, unroll=8)


def _stacked_state_io(n_sub, sd, layer_i, st_prev, n_in):
    spec = pl.BlockSpec((None, n_sub, *sd), lambda i: (layer_i, i, 0, 0))
    if st_prev is None:
        return spec, [], [], {}
    return spec, [pl.BlockSpec(memory_space=pl.ANY)], [st_prev], {n_in: 1}


def _ret_call(p, row0, n_seq, seq_len, s0, pos0, chained, t, n_sub=1, out_rows=None, layer_i=0, st_prev=None):
    n_rows = n_seq * seq_len
    rows_blk = t if chained else t * n_sub
    blk0 = row0 // rows_blk
    assert row0 % rows_blk == 0
    out_blk = rows_blk if out_rows is None else out_rows
    assert out_rows is None or n_rows == rows_blk
    tabs = _ret_tables(t)
    consts = [tabs[n] for n in ("dq", "dm", "dk", "ds", "hm")]
    const_specs = [_const_spec(c.shape, (0, 0)) for c in consts]
    cos, s_up, s_dn = _rope_tables(pos0 + jnp.arange(seq_len, dtype=jnp.int32))
    sd = RET_QK, RET_DV
    if chained:
        n_chunk = seq_len // t
        grid = (n_seq, n_chunk)
        rmap = lambda cb: (lambda b, c: (blk0 + b * n_chunk + c, cb))
        data_specs = [pl.BlockSpec((t, RET_QK), rmap(2)), pl.BlockSpec((t, RET_QK), rmap(3)),
                      pl.BlockSpec((t, RET_WIDTH), rmap(2)), pl.BlockSpec((t, RET_WIDTH), rmap(3)),
                      _const_spec(sd, (0, 0))]
        rope_specs = [pl.BlockSpec((t, RET_QK), lambda b, c: (c, 0))] * 3
        out_specs = [pl.BlockSpec((out_blk, RET_WIDTH), lambda b, c: (b * n_chunk + c, 0)),
                     pl.BlockSpec((None, *sd), lambda b, c: (b, 0, 0))]
        st_shape = (n_seq, *sd)
        extra_specs, extra_args, aliases = [], [], {}
        scratch = [pltpu.VMEM(sd, F32)]
        sem = ("arbitrary", "arbitrary")
    else:
        assert seq_len == t and n_seq % n_sub == 0
        grid = (n_seq // n_sub,)
        rmap = lambda cb: (lambda i: (blk0 + i, cb))
        st_spec, extra_specs, extra_args, aliases = _stacked_state_io(n_sub, sd, layer_i, st_prev, 13)
        data_specs = [pl.BlockSpec((rows_blk, RET_QK), rmap(2)), pl.BlockSpec((rows_blk, RET_QK), rmap(3)),
                      pl.BlockSpec((rows_blk, RET_WIDTH), rmap(2)), pl.BlockSpec((rows_blk, RET_WIDTH), rmap(3)),
                      st_spec]
        rope_specs = [_const_spec((t, RET_QK), (0, 0))] * 3
        out_specs = [pl.BlockSpec((rows_blk, RET_WIDTH), lambda i: (i, 0)), st_spec]
        st_shape = s0.shape
        scratch = []
        sem = ("arbitrary",)
    in_specs = [*data_specs, *rope_specs, *const_specs, *extra_specs]
    assert not aliases or list(aliases) == [len(in_specs) - 1]
    return pl.pallas_call(
        functools.partial(_ret_kernel, chained, t, n_sub),
        grid=grid,
        in_specs=in_specs,
        out_specs=out_specs,
        out_shape=[jax.ShapeDtypeStruct((n_rows // rows_blk * out_blk, RET_WIDTH), BF16),
                   jax.ShapeDtypeStruct(st_shape, F32)],
        scratch_shapes=scratch,
        input_output_aliases=aliases,
        compiler_params=_cparams(sem),
        name="retention",
    )(p, p, p, p, s0, cos, s_up, s_dn, *consts, *extra_args)


def _gla_tables(t, n_blk):
    cg = t * n_blk
    blk = np.arange(cg) // t
    same = blk[:, None] == blk[None, :]
    cum = (same & (np.arange(cg)[:, None] >= np.arange(cg)[None, :])).astype(np.float32)
    ones = (np.arange(t * GLA_DK)[:, None] // GLA_DK == np.arange(LANES)[None, :]).astype(np.float32)
    hm = np.repeat(np.eye(GLA_HEADS), GLA_DK, axis=1)
    hm = np.concatenate([hm, np.zeros((SUBLANES - GLA_HEADS, GLA_QK))], axis=0).astype(np.float32)
    return jnp.asarray(cum, BF16), jnp.asarray(ones, BF16), jnp.asarray(hm, F32)


LOG2E = 1.4426950408889634


def _gla_kernel(chained, t, n_blk, n_par, *refs):
    data = [refs[5 * s:5 * s + 5] for s in range(n_par)]
    s0_ref, wa_ref, ba_ref, ng_ref, cum_ref, ones_ref, hm_ref = refs[5 * n_par:5 * n_par + 7]
    rest = refs[5 * n_par + 7:]
    out_ref, st_ref, rt_scr, o_scr, *scratch = rest[-5:] if chained else rest[-4:]
    cg = t * n_blk
    nh, dk, dv = GLA_HEADS, GLA_DK, GLA_DV
    par = range(n_par)
    if chained:
        (s_scr,) = scratch

        @pl.when(pl.program_id(1) == 0)
        def _():
            for s in par:
                s_scr[s] = s0_ref[...]

    hm = hm_ref[...]
    q, ksc, vb, b, qe, ke, a_cols = [], [], [], [], [], [], []
    for s in par:
        q_ref, k_ref, v_ref, _, lr_ref = data[s]
        q.append(q_ref[...])
        ksc.append(k_ref[...] * (dk ** -0.5))
        vb.append(v_ref[...].astype(BF16))
        la = jax.nn.log_sigmoid(_dot(lr_ref[...].astype(BF16), wa_ref[...]) + ba_ref[...]) / GLA_TAU
        b.append(_dot_exact01(cum_ref[...], la) * LOG2E)
        last = b[s].reshape(n_blk, t, GLA_QK)[:, t - 1:t, :]
        bl = jnp.broadcast_to(last, (n_blk, t, GLA_QK)).reshape(cg, GLA_QK)
        tot = jnp.concatenate([last.reshape(n_blk, GLA_QK), jnp.zeros((LANES - n_blk, GLA_QK), F32)], axis=0)
        a_cols.append(jnp.exp2(tot).T)
        qe.append(q[s] * jnp.exp2(b[s]))
        ke.append(ksc[s] * jnp.exp2(bl - b[s]))

    row_t = {lo: lo + lax.broadcasted_iota(jnp.int32, (t - lo, GLA_QK), 0) for lo in range(0, t, SUBLANES)}
    for j in range(n_blk):
        rows = slice(j * t, (j + 1) * t)
        for s in par:
            qj, kj, bj = q[s][rows], ksc[s][rows], b[s][rows]
            for i in range(t):
                lo = i // SUBLANES * SUBLANES
                e = jnp.exp2(jnp.where(row_t[lo] >= i, bj[lo:] - bj[i:i + 1, :], NEG_BIG))
                prod = (qj[lo:] * kj[i:i + 1, :]) * e
                for h in range(nh):
                    ph = prod[:, h * dk:(h + 1) * dk]
                    if lo:
                        ph = jnp.concatenate([jnp.zeros((lo, dk), F32), ph], axis=0)
                    rt_scr[s, (j * nh + h) * t:(j * nh + h + 1) * t, i * dk:(i + 1) * dk] = ph.astype(rt_scr.dtype)
    intra = []
    for s in par:
        scores = _dot(rt_scr[s].astype(BF16), ones_ref[...])
        per_head = []
        for h in range(nh):
            pieces = []
            for j in range(n_blk):
                piece = scores[(j * nh + h) * t:(j * nh + h + 1) * t, :]
                pieces.append(pltpu.roll(piece, j * t, 1) if j else piece)
            pfull = jnp.concatenate(pieces, axis=0) if n_blk > 1 else pieces[0]
            per_head.append(_dot(pfull[:, :cg].astype(BF16), vb[s][:, h * dv:(h + 1) * dv]))
        intra.append(per_head)

    state = [s_scr[s] for s in par] if chained else [None]
    for j in range(n_blk):
        rows = slice(j * t, (j + 1) * t)
        for s in par:
            if not chained:
                state[s] = s0_ref[j]
            qs = jnp.concatenate([qe[s][rows] * hm[h:h + 1, :] for h in range(nh)], axis=0).astype(BF16)
            inter = _dot(qs, state[s].astype(BF16))
            for h in range(nh):
                o_scr[s, rows, h * dv:(h + 1) * dv] = inter[h * t:(h + 1) * t] + intra[s][h][rows]
            ks = jnp.concatenate([ke[s][rows] * hm[h:h + 1, :] for h in range(nh)], axis=0).astype(BF16)
            vs = jnp.concatenate([vb[s][rows, h * dv:(h + 1) * dv] for h in range(nh)], axis=0)
            state[s] = state[s] * a_cols[s][:, j:j + 1] + _dot_tn(ks, vs)
            if not chained:
                st_ref[j] = state[s]
    for s in par:
        if chained:
            s_scr[s] = state[s]
            st_ref[s] = state[s]
        o = o_scr[s]
        r = data[s][3][...]
        gated = []
        for h in range(nh):
            cols = slice(h * dv, (h + 1) * dv)
            oh = o[:, cols]
            on = oh * lax.rsqrt(jnp.mean(oh * oh, axis=-1, keepdims=True) + EPS) * ng_ref[...]
            rh = r[:, cols]
            gated.append(on * (rh * jax.nn.sigmoid(rh)))
        _store_rows(out_ref.at[s] if chained else out_ref, jnp.concatenate(gated, axis=1))


def _gla_call(p, row0, n_seq, seq_len, s0, wa, ba, ng, chained, t, n_blk, n_par=1, out_rows=None, layer_i=0,
              st_prev=None):
    cg = t * n_blk
    blk0 = row0 // cg
    assert row0 % cg == 0 and cg <= LANES
    out_blk = cg if out_rows is None else out_rows
    assert out_rows is None or seq_len == cg
    cum, ones, hm = _gla_tables(t, n_blk)
    consts = [wa, ba, ng, cum, ones, hm]
    const_specs = [_const_spec(c.shape, (0,) * c.ndim) for c in consts]
    sd = GLA_QK, GLA_DV
    lr_col = (2 * GLA_QK + 2 * GLA_V) // LANES
    rt_dtype = BF16 if t % (2 * SUBLANES) == 0 else F32
    if chained:
        assert n_seq % n_par == 0
        n_chunk = seq_len // cg
        grid = (n_seq // n_par, n_chunk)
        rmap = lambda s, cb: (lambda b, c: (blk0 + (b * n_par + s) * n_chunk + c, cb))
        s_spec = _const_spec(sd, (0, 0))
        out_specs = [pl.BlockSpec((n_par, out_blk, GLA_V), lambda b, c: (b, c, 0)),
                     pl.BlockSpec((n_par, *sd), lambda b, c: (b, 0, 0))]
        out_shape = [jax.ShapeDtypeStruct((n_seq, n_chunk * out_blk, GLA_V), BF16),
                     jax.ShapeDtypeStruct((n_seq, *sd), F32)]
        extra_specs, extra_args, aliases = [], [], {}
        scratch = [pltpu.VMEM((n_par, *sd), F32)]
        sem = ("arbitrary", "arbitrary")
    else:
        assert seq_len == t and n_seq % n_blk == 0 and n_par == 1
        grid = (n_seq // n_blk,)
        rmap = lambda s, cb: (lambda i: (blk0 + i, cb))
        s_spec, extra_specs, extra_args, aliases = _stacked_state_io(n_blk, sd, layer_i, st_prev, 12)
        out_specs = [pl.BlockSpec((cg, GLA_V), lambda i: (i, 0)), s_spec]
        out_shape = [jax.ShapeDtypeStruct((n_seq * seq_len, GLA_V), BF16), jax.ShapeDtypeStruct(s0.shape, F32)]
        scratch = []
        sem = ("arbitrary",)
    data_specs = []
    for s in range(n_par):
        data_specs += [pl.BlockSpec((cg, GLA_QK), rmap(s, 0)), pl.BlockSpec((cg, GLA_QK), rmap(s, 1)),
                       pl.BlockSpec((cg, GLA_V), rmap(s, 1)), pl.BlockSpec((cg, GLA_V), rmap(s, 2)),
                       pl.BlockSpec((cg, LANES), rmap(s, lr_col))]
    in_specs = [*data_specs, s_spec, *const_specs, *extra_specs]
    assert not aliases or list(aliases) == [len(in_specs) - 1]
    out, st = pl.pallas_call(
        functools.partial(_gla_kernel, chained, t, n_blk, n_par),
        grid=grid,
        in_specs=in_specs,
        out_specs=out_specs,
        out_shape=out_shape,
        scratch_shapes=[pltpu.VMEM((n_par, n_blk * GLA_HEADS * t, t * GLA_DK), rt_dtype),
                        pltpu.VMEM((n_par, cg, GLA_V), F32), *scratch],
        input_output_aliases=aliases,
        compiler_params=_cparams(sem),
        name="gla",
    )(*([p] * (5 * n_par)), s0, *consts, *extra_args)
    return out.reshape(-1, GLA_V), st


S5_TT_PROMPT = 128
S5_TC_SAMPLE = 256
RET_T_PROMPT = 256
RET_SUB_SAMPLE = 16
GLA_T_PROMPT = 16
GLA_BLK_PROMPT = 8
GLA_BLK_SAMPLE = 8
GLA_PAR_PROMPT = 2


def _mixer_ab(p, i, prm, st_s5_re, st_s5_im, st_ret, s_ret_prev):
    zeros_h = jnp.zeros((1, S5_CH), F32)
    a_m, hre_m, him_m = _s5_call(p, ROW0_META, 1, N_META, zeros_h, zeros_h, prm, True, N_META, out_rows=DENSE_TM)
    a_p, hre_p, him_p = _s5_batch_call(p, 0, BATCH, SEQ, hre_m[0], him_m[0], prm, S5_TT_PROMPT)
    h0re = jnp.repeat(st_s5_re[i].reshape(DEC_BATCH, S5_CH), DEC_SEQ, axis=0)
    h0im = jnp.repeat(st_s5_im[i].reshape(DEC_BATCH, S5_CH), DEC_SEQ, axis=0)
    a_s, hre_s, him_s = _s5_call(p, ROW0_SAMPLE, DEC_BATCH, DEC_SEQ, h0re, h0im, prm, False, S5_TC_SAMPLE)
    last = lambda h: h.reshape(DEC_BATCH, DEC_SEQ, S5_CH)[:, DEC_SEQ - 1].reshape(DEC_BATCH, S5_GROUPS, S5_STATE)

    zeros_s = jnp.zeros((RET_QK, RET_DV), F32)
    b_m, s_m = _ret_call(p, ROW0_META, 1, N_META, zeros_s, 0, True, N_META, out_rows=DENSE_TM)
    b_p, s_p = _ret_call(p, 0, BATCH, SEQ, s_m[0], N_META, True, RET_T_PROMPT)
    b_s, s_ret = _ret_call(p, ROW0_SAMPLE, DEC_BATCH, DEC_SEQ, st_ret, PAST_LEN, False, DEC_SEQ, RET_SUB_SAMPLE,
                           layer_i=i, st_prev=s_ret_prev)
    states = dict(
        p_s5_re=hre_p.reshape(BATCH, S5_GROUPS, S5_STATE), p_s5_im=him_p.reshape(BATCH, S5_GROUPS, S5_STATE),
        p_ret=s_p.reshape(BATCH, RET_HEADS, RET_DK, RET_DV), s_s5_re=last(hre_s), s_s5_im=last(him_s))
    return [(a_p, a_s, a_m), (b_p, b_s, b_m)], states, s_ret


def _mixer_gla(p, i, wa, ba, ng, st_gla, s_gla_prev):
    zeros_s = jnp.zeros((GLA_QK, GLA_DV), F32)
    o_m, s_m = _gla_call(p, ROW0_META, 1, N_META, zeros_s, wa, ba, ng, True, N_META, 1, out_rows=DENSE_TM)
    o_p, s_p = _gla_call(p, 0, BATCH, SEQ, s_m[0], wa, ba, ng, True, GLA_T_PROMPT, GLA_BLK_PROMPT,
                         n_par=GLA_PAR_PROMPT)
    o_s, s_gla = _gla_call(p, ROW0_SAMPLE, DEC_BATCH, DEC_SEQ, st_gla, wa, ba, ng, False, DEC_SEQ, GLA_BLK_SAMPLE,
                           layer_i=i, st_prev=s_gla_prev)
    states = dict(p_gla=s_p.reshape(BATCH, GLA_HEADS, GLA_DK, GLA_DV))
    return [(o_p, o_s, o_m)], states, s_gla


def kernel(x_prompt, x_sample, state_s5_re, state_s5_im, state_ret, state_gla, meta_tokens, norm_ffn1, norm_mix,
           norm_ffn2, norm_final, ffn1_w_gu, ffn1_w_down, ffn2_w_gu, ffn2_w_down, ab_w_in, ab_w_out, s5_a_re,
           s5_a_im, s5_log_dt, s5_b_re, s5_b_im, s5_c_re, s5_c_im, s5_d, s5_w_glu, gla_w_in, gla_w_alpha2,
           gla_b_alpha, gla_norm, gla_w_out):
    bf = lambda w: w.astype(BF16)
    ffn1_gu, ffn1_dn, ffn2_gu, ffn2_dn = bf(ffn1_w_gu), bf(ffn1_w_down), bf(ffn2_w_gu), bf(ffn2_w_down)
    ab_in, ab_out, gla_out = bf(ab_w_in), bf(ab_w_out), bf(gla_w_out)
    gla_in = bf(jnp.pad(gla_w_in, ((0, 0), (0, 0), (0, GLA_IN_PAD - GLA_IN))))
    gla_wa = bf(jnp.pad(gla_w_alpha2, ((0, 0), (0, LANES - GLA_LOWRANK), (0, 0))))
    n1 = norm_ffn1.reshape(DEPTH, 1, D_MODEL)
    nm = norm_mix.reshape(DEPTH, 1, D_MODEL)
    n2 = norm_ffn2.reshape(DEPTH, 1, D_MODEL)

    meta_pad = jnp.pad(meta_tokens.astype(x_prompt.dtype), ((0, DENSE_TM - N_META), (0, 0)))
    xs = [x_prompt.reshape(ROWS_PROMPT, D_MODEL), x_sample.reshape(ROWS_SAMPLE, D_MODEL), meta_pad]
    st_ret = state_ret.reshape(N_EVEN, DEC_BATCH, RET_QK, RET_DV)
    st_gla = state_gla.reshape(N_ODD, DEC_BATCH, GLA_QK, GLA_DV)
    s_ret = s_gla = None
    collected = {}
    for layer in range(DEPTH):
        i = layer // 2
        last = layer == DEPTH - 1
        g_final = norm_final.reshape(1, D_MODEL) if last else None
        if layer % 2 == 0:
            x1, p = _pre_call(xs, n1, ffn1_gu, ffn1_dn, nm, ab_in, layer, i)
            prm = _s5_params(s5_a_re[i], s5_a_im[i], s5_log_dt[i], s5_b_re[i], s5_b_im[i], s5_c_re[i],
                             s5_c_im[i], s5_d[i], s5_w_glu[i])
            mixes, states, s_ret = _mixer_ab(p, i, prm, state_s5_re, state_s5_im, st_ret, s_ret)
            x = _post_call(x1, mixes, ab_out, n2, ffn2_gu, ffn2_dn, layer, i, g_final)
        else:
            x1, p = _pre_call(xs, n1, ffn1_gu, ffn1_dn, nm, gla_in, layer, i)
            mixes, states, s_gla = _mixer_gla(p, i, gla_wa[i], gla_b_alpha[i].reshape(1, GLA_QK),
                                              gla_norm[i].reshape(1, GLA_DV), st_gla, s_gla)
            x = _post_call(x1, mixes, gla_out, n2, ffn2_gu, ffn2_dn, layer, i, g_final)
        xs = [x]
        for name, val in states.items():
            collected.setdefault(name, []).append(val)
    out = {name: jnp.stack(vals) for name, vals in collected.items()}
    y_prompt, y_rest = x
    return (y_prompt.reshape(BATCH, SEQ, D_MODEL), y_rest[:ROWS_SAMPLE].reshape(DEC_BATCH, DEC_SEQ, D_MODEL),
            out["p_s5_re"], out["p_s5_im"], out["p_ret"], out["p_gla"], out["s_s5_re"], out["s_s5_im"],
            s_ret.reshape(N_EVEN, DEC_BATCH, RET_HEADS, RET_DK, RET_DV),
            s_gla.reshape(N_ODD, DEC_BATCH, GLA_HEADS, GLA_DK, GLA_DV))
```

```python
import functools
import math

import numpy as np
import jax
import jax.numpy as jnp
from jax import lax
from jax.experimental import pallas as pl
from jax.experimental.pallas import tpu as pltpu

F32 = jnp.float32
BF16 = jnp.bfloat16

D_MODEL = 1024
BATCH = 8
SEQ = 2048
DEPTH = 4
DEC_BATCH = 128
DEC_SEQ = 8
PAST_LEN = 16384
N_META = 16
N_EVEN = (DEPTH + 1) // 2
N_ODD = DEPTH // 2
S5_WIDTH = D_MODEL // 2
S5_GROUP = 16
S5_GROUPS = S5_WIDTH // S5_GROUP
S5_STATE = 64
S5_CH = S5_GROUPS * S5_STATE
RET_HEADS = 4
RET_DK = D_MODEL // 16
RET_DV = 2 * RET_DK
RET_QK = RET_HEADS * RET_DK
RET_WIDTH = RET_HEADS * RET_DV
AB_IN = S5_WIDTH + 2 * RET_QK + 2 * RET_WIDTH
AB_OUT = S5_WIDTH + RET_WIDTH
GLA_HEADS = 4
GLA_DK = D_MODEL // (2 * GLA_HEADS)
GLA_DV = D_MODEL // GLA_HEADS
GLA_QK = GLA_HEADS * GLA_DK
GLA_V = GLA_HEADS * GLA_DV
GLA_LOWRANK = 16
GLA_TAU = 16.0
GLA_IN = 2 * GLA_QK + 2 * GLA_V + GLA_LOWRANK
LANES = 128
GLA_IN_PAD = 2 * GLA_QK + 2 * GLA_V + LANES
D_FF = 128 * ((8 * D_MODEL // 3 + 127) // 128)
EPS = 1e-6
ROPE_BASE = 10000.0
NEG_BIG = -1e30

ROWS_PROMPT = BATCH * SEQ
ROWS_SAMPLE = DEC_BATCH * DEC_SEQ
ROW0_SAMPLE = ROWS_PROMPT
ROW0_META = ROWS_PROMPT + ROWS_SAMPLE
ROWS = ROW0_META + N_META

VMEM_LIMIT = 56 * 1024 * 1024
DENSE_TM = 256
TILES_PROMPT = ROWS_PROMPT // DENSE_TM
TILES_SAMPLE = ROWS_SAMPLE // DENSE_TM
N_TILES = TILES_PROMPT + TILES_SAMPLE + 1
ROWS_PAD = N_TILES * DENSE_TM
ROWS_REST = ROWS_PAD - ROWS_PROMPT


def _cparams(sem):
    return pltpu.CompilerParams(dimension_semantics=sem, vmem_limit_bytes=VMEM_LIMIT)


def _dot(a, b):
    return jnp.dot(a, b, preferred_element_type=F32)


def _dot_tn(a, b):
    return lax.dot_general(a, b, (((0,), (0,)), ((), ())), preferred_element_type=F32)


def _dot_nt(a, b):
    return lax.dot_general(a, b, (((1,), (1,)), ((), ())), preferred_element_type=F32)


def _dot_exact01(m_bf, x):
    h1 = x.astype(BF16)
    r1 = x - h1.astype(F32)
    h2 = r1.astype(BF16)
    r2 = r1 - h2.astype(F32)
    h3 = r2.astype(BF16)
    return _dot(m_bf, h1) + _dot(m_bf, h2) + _dot(m_bf, h3)


def _rms(x, g):
    return x * lax.rsqrt(jnp.mean(x * x, axis=-1, keepdims=True) + EPS) * g


def _swiglu_half(x, g, wg, wu, wd):
    h = _rms(x, g).astype(BF16)
    gate = _dot(h, wg)
    up = _dot(h, wu)
    act = (gate * jax.nn.sigmoid(gate) * up).astype(BF16)
    return x + 0.5 * _dot(act, wd)


def _store_rows(out_ref, val):
    rows = val.shape[0]
    out_ref[0:rows, :] = val.astype(out_ref.dtype)
    if out_ref.shape[0] > rows:
        out_ref[rows:, :] = jnp.zeros((out_ref.shape[0] - rows, out_ref.shape[1]), out_ref.dtype)


def _const_spec(shape, index):
    return pl.BlockSpec(shape, lambda *_: index, pipeline_mode=pl.Buffered(1))


N_WCHUNK = 16


def _tile(g):
    return jnp.maximum(g - N_WCHUNK, 0)


def _row_spec(width):
    return pl.BlockSpec((DENSE_TM, width), lambda g: (_tile(g), 0))


def _group_specs(width):
    tm = DENSE_TM
    return [pl.BlockSpec((tm, width), lambda g: (jnp.minimum(_tile(g), TILES_PROMPT - 1), 0)),
            pl.BlockSpec((tm, width), lambda g: (jnp.clip(_tile(g) - TILES_PROMPT, 0, TILES_SAMPLE - 1), 0)),
            pl.BlockSpec((tm, width), lambda g: (0, 0))]


def _pick_group(refs):
    if len(refs) == 1:
        return refs[0][...]
    i = pl.program_id(0) - N_WCHUNK
    return jnp.where(i < TILES_PROMPT, refs[0][...],
                     jnp.where(i < TILES_PROMPT + TILES_SAMPLE, refs[1][...], refs[2][...]))


def _chunk_spec(w, lead):
    _, rows, cols = w.shape
    assert rows % (N_WCHUNK * 2 * SUBLANES) == 0
    return pl.BlockSpec((None, rows // N_WCHUNK, cols), lambda g: (lead, jnp.minimum(g, N_WCHUNK - 1), 0))


def _stage(dst, chunk, g, cols=None):
    rows = chunk.shape[0]
    r = pl.ds(pl.multiple_of(g * rows, rows), rows)
    val = chunk[...] if cols is None else chunk[:, cols]
    width = val.shape[1]
    dst[r, 0:width] = val.astype(BF16)
    if dst.shape[1] > width:
        dst[r, width:] = jnp.zeros((rows, dst.shape[1] - width), BF16)


def _pre_kernel(n_x, *refs):
    x_refs = refs[:n_x]
    g1_ref, wgu_ref, wd_ref, g2_ref, win_ref, x1_ref, p_ref, wg_s, wu_s, wd_s, win_s = refs[n_x:]
    g = pl.program_id(0)

    @pl.when(g < N_WCHUNK)
    def _():
        _stage(wg_s, wgu_ref, g, slice(0, D_FF))
        _stage(wu_s, wgu_ref, g, slice(D_FF, 2 * D_FF))
        _stage(wd_s, wd_ref, g)
        _stage(win_s, win_ref, g)

    @pl.when(g >= N_WCHUNK)
    def _():
        x1 = _swiglu_half(_pick_group(x_refs), g1_ref[...], wg_s[...], wu_s[...], wd_s[...])
        x1_ref[...] = x1
        p_ref[...] = _dot(_rms(x1, g2_ref[...]).astype(BF16), win_s[...])


def _ffn_scratch():
    return [pltpu.VMEM((D_MODEL, D_FF), BF16), pltpu.VMEM((D_MODEL, D_FF), BF16), pltpu.VMEM((D_FF, D_MODEL), BF16)]


def _pre_call(xs, g1, w_gu, w_down, g2, w_in, layer, mix_idx, n_in):
    x_specs = [_row_spec(D_MODEL)] if len(xs) == 1 else _group_specs(D_MODEL)
    return pl.pallas_call(
        functools.partial(_pre_kernel, len(xs)),
        grid=(N_WCHUNK + N_TILES,),
        in_specs=[
            *x_specs,
            _const_spec((None, 1, D_MODEL), (layer, 0, 0)),
            _chunk_spec(w_gu, layer),
            _chunk_spec(w_down, layer),
            _const_spec((None, 1, D_MODEL), (layer, 0, 0)),
            _chunk_spec(w_in, mix_idx),
        ],
        out_specs=[_row_spec(D_MODEL), _row_spec(n_in)],
        out_shape=[jax.ShapeDtypeStruct((ROWS_PAD, D_MODEL), F32), jax.ShapeDtypeStruct((ROWS_PAD, n_in), F32)],
        scratch_shapes=[*_ffn_scratch(), pltpu.VMEM((D_MODEL, n_in), BF16)],
        compiler_params=_cparams(("arbitrary",)),
        name="pre",
    )(*xs, g1, w_gu, w_down, g2, w_in)


def _post_kernel(n_mix, final, *refs):
    x1_ref = refs[0]
    mix_refs = refs[1:1 + 3 * n_mix]
    wout_ref, g_ref, wgu_ref, wd_ref = refs[1 + 3 * n_mix:5 + 3 * n_mix]
    rest = refs[5 + 3 * n_mix:]
    wout_s, wg_s, wu_s, wd_s = rest[-4:]
    g = pl.program_id(0)

    @pl.when(g < N_WCHUNK)
    def _():
        _stage(wout_s, wout_ref, g)
        _stage(wg_s, wgu_ref, g, slice(0, D_FF))
        _stage(wu_s, wgu_ref, g, slice(D_FF, 2 * D_FF))
        _stage(wd_s, wd_ref, g)

    @pl.when(g >= N_WCHUNK)
    def _():
        x2 = x1_ref[...]
        width = AB_OUT // n_mix
        for i in range(n_mix):
            x2 = x2 + _dot(_pick_group(mix_refs[3 * i:3 * i + 3]), wout_s[i * width:(i + 1) * width, :])
        y = _swiglu_half(x2, g_ref[...], wg_s[...], wu_s[...], wd_s[...])
        if final:
            gf_ref, yp_ref, yr_ref = rest[:3]
            y = _rms(y, gf_ref[...])

            @pl.when(g < N_WCHUNK + TILES_PROMPT)
            def _():
                yp_ref[...] = y

            @pl.when(g >= N_WCHUNK + TILES_PROMPT)
            def _():
                yr_ref[...] = y
        else:
            rest[0][...] = y


def _post_call(x1, mixes, w_out, g, w_gu, w_down, layer, mix_idx, g_final):
    tm = DENSE_TM
    final = g_final is not None
    in_specs = [_row_spec(D_MODEL)]
    for triple in mixes:
        in_specs += _group_specs(triple[0].shape[1])
    in_specs += [
        _chunk_spec(w_out, mix_idx),
        _const_spec((None, 1, D_MODEL), (layer, 0, 0)),
        _chunk_spec(w_gu, layer),
        _chunk_spec(w_down, layer),
    ]
    args = [x1, *[m for triple in mixes for m in triple], w_out, g, w_gu, w_down]
    if final:
        in_specs.append(_const_spec((1, D_MODEL), (0, 0)))
        args.append(g_final)
        out_specs = [pl.BlockSpec((tm, D_MODEL), lambda s: (jnp.minimum(_tile(s), TILES_PROMPT - 1), 0)),
                     pl.BlockSpec((tm, D_MODEL), lambda s: (jnp.maximum(_tile(s) - TILES_PROMPT, 0), 0))]
        out_shape = [jax.ShapeDtypeStruct((ROWS_PROMPT, D_MODEL), F32),
                     jax.ShapeDtypeStruct((ROWS_REST, D_MODEL), F32)]
    else:
        out_specs = _row_spec(D_MODEL)
        out_shape = jax.ShapeDtypeStruct((ROWS_PAD, D_MODEL), F32)
    return pl.pallas_call(
        functools.partial(_post_kernel, len(mixes), final),
        grid=(N_WCHUNK + N_TILES,),
        in_specs=in_specs,
        out_specs=out_specs,
        out_shape=out_shape,
        scratch_shapes=[pltpu.VMEM((AB_OUT, D_MODEL), BF16), *_ffn_scratch()],
        compiler_params=_cparams(("arbitrary",)),
        name="post",
    )(*args)


S5_HALF_IN = S5_WIDTH // 2
S5_HALF_CH = S5_CH // 2
SUBLANES = 8


def _s5_kernel(chained, tc, u_ref, h0re_ref, h0im_ref, bcat_ref, cre_ref, cim_ref, ast_re_ref, ast_im_ref,
               apw_re_ref, apw_im_ref, d_ref, wglu_ref, out_ref, hre_out, him_out, xre, xim, *carry):
    nb = tc // SUBLANES
    u = u_ref[...]
    ub = u.astype(BF16)
    for hf in range(2):
        xh = _dot(ub[:, hf * S5_HALF_IN:(hf + 1) * S5_HALF_IN], bcat_ref[hf])
        xre[:, hf * S5_HALF_CH:(hf + 1) * S5_HALF_CH] = xh[:, :S5_HALF_CH]
        xim[:, hf * S5_HALF_CH:(hf + 1) * S5_HALF_CH] = xh[:, S5_HALF_CH:]

    sr = xre[...].reshape(nb, SUBLANES, S5_CH)
    si = xim[...].reshape(nb, SUBLANES, S5_CH)
    rowi = lax.broadcasted_iota(jnp.int32, (nb, SUBLANES, S5_CH), 1)
    for step, d in enumerate((1, 2, 4)):
        ar = ast_re_ref[step:step + 1, :][None]
        ai = ast_im_ref[step:step + 1, :][None]
        pr = pltpu.roll(sr, d, 1)
        pi = pltpu.roll(si, d, 1)
        keep = rowi >= d
        sr, si = (sr + jnp.where(keep, ar * pr - ai * pi, 0.0),
                  si + jnp.where(keep, ar * pi + ai * pr, 0.0))
    apr = apw_re_ref[...]
    api = apw_im_ref[...]
    if chained:
        cre, cim = carry
        xre[...] = sr.reshape(tc, S5_CH)
        xim[...] = si.reshape(tc, S5_CH)

        @pl.when(pl.program_id(1) == 0)
        def _():
            cre[...] = h0re_ref[...]
            cim[...] = h0im_ref[...]

        def group(r, c):
            hr, hi = c
            rows = pl.ds(pl.multiple_of(r * SUBLANES, SUBLANES), SUBLANES)
            nr = xre[rows, :] + apr * hr - api * hi
            ni = xim[rows, :] + apr * hi + api * hr
            xre[rows, :] = nr
            xim[rows, :] = ni
            return nr[SUBLANES - 1:, :], ni[SUBLANES - 1:, :]

        hr, hi = lax.fori_loop(0, nb, group, (cre[...], cim[...]), unroll=min(nb, 4))
        cre[...] = hr
        cim[...] = hi
        hre_out[...] = hr
        him_out[...] = hi
    else:
        h0r = h0re_ref[...].reshape(nb, SUBLANES, S5_CH)
        h0i = h0im_ref[...].reshape(nb, SUBLANES, S5_CH)
        fr = (sr + apr[None] * h0r - api[None] * h0i).reshape(tc, S5_CH)
        fi = (si + apr[None] * h0i + api[None] * h0r).reshape(tc, S5_CH)
        xre[...] = fr
        xim[...] = fi
        hre_out[...] = fr
        him_out[...] = fi

    ys = []
    for hf in range(2):
        cols = slice(hf * S5_HALF_CH, (hf + 1) * S5_HALF_CH)
        ys.append(_dot(xre[:, cols].astype(BF16), cre_ref[hf]) + _dot(xim[:, cols].astype(BF16), cim_ref[hf]))
    y = jnp.concatenate(ys, axis=1) + d_ref[...] * u
    z = jax.nn.gelu(y)
    _store_rows(out_ref, z * jax.nn.sigmoid(_dot(z.astype(BF16), wglu_ref[...])))


def _s5_call(p, row0, n_seq, seq_len, h0re, h0im, prm, chained, tc, out_rows=None):
    n_rows = n_seq * seq_len
    blk0 = row0 // tc
    assert row0 % tc == 0 and n_rows % tc == 0
    out_blk = tc if out_rows is None else out_rows
    assert out_rows is None or n_rows == tc
    consts = [prm["bcat"], prm["cre"], prm["cim"], prm["ast_re"], prm["ast_im"], prm["apw_re"], prm["apw_im"],
              prm["d"], prm["wglu"]]
    const_specs = [_const_spec(c.shape, (0,) * c.ndim) for c in consts]
    scratch = [pltpu.VMEM((tc, S5_CH), F32), pltpu.VMEM((tc, S5_CH), F32)]
    if chained:
        n_chunk = seq_len // tc
        grid = (n_seq, n_chunk)
        u_spec = pl.BlockSpec((tc, S5_WIDTH), lambda b, c: (blk0 + b * n_chunk + c, 0))
        h_specs = [_const_spec((1, S5_CH), (0, 0))] * 2
        out_specs = [pl.BlockSpec((out_blk, S5_WIDTH), lambda b, c: (b * n_chunk + c, 0)),
                     pl.BlockSpec((None, 1, S5_CH), lambda b, c: (b, 0, 0)),
                     pl.BlockSpec((None, 1, S5_CH), lambda b, c: (b, 0, 0))]
        out_shape = [jax.ShapeDtypeStruct((n_rows // tc * out_blk, S5_WIDTH), BF16),
                     jax.ShapeDtypeStruct((n_seq, 1, S5_CH), F32), jax.ShapeDtypeStruct((n_seq, 1, S5_CH), F32)]
        scratch += [pltpu.VMEM((1, S5_CH), F32), pltpu.VMEM((1, S5_CH), F32)]
        sem = ("arbitrary", "arbitrary")
    else:
        assert seq_len == SUBLANES
        grid = (n_rows // tc,)
        u_spec = pl.BlockSpec((tc, S5_WIDTH), lambda i: (blk0 + i, 0))
        h_specs = [pl.BlockSpec((tc, S5_CH), lambda i: (i, 0))] * 2
        out_specs = [pl.BlockSpec((tc, S5_WIDTH), lambda i: (i, 0)),
                     pl.BlockSpec((tc, S5_CH), lambda i: (i, 0)), pl.BlockSpec((tc, S5_CH), lambda i: (i, 0))]
        out_shape = [jax.ShapeDtypeStruct((n_rows, S5_WIDTH), BF16),
                     jax.ShapeDtypeStruct((n_rows, S5_CH), F32), jax.ShapeDtypeStruct((n_rows, S5_CH), F32)]
        sem = ("arbitrary",)
    return pl.pallas_call(
        functools.partial(_s5_kernel, chained, tc),
        grid=grid,
        in_specs=[u_spec, *h_specs, *const_specs],
        out_specs=out_specs,
        out_shape=out_shape,
        scratch_shapes=scratch,
        compiler_params=_cparams(sem),
        name="s5",
    )(p, h0re, h0im, *consts)


S5_SLABS = S5_WIDTH // LANES


def _s5_batch_kernel(n_seq, tt, *refs):
    u_refs = refs[:n_seq]
    (h0re_ref, h0im_ref, bcat_ref, cre_ref, cim_ref, are_ref, aim_ref, d_ref, wglu_ref,
     out_ref, hre_out, him_out, u_tb, o_tb, xre, xim, cre, cim) = refs[n_seq:]
    @pl.when(pl.program_id(0) == 0)
    def _():
        cre[...] = jnp.broadcast_to(h0re_ref[...], (n_seq, S5_CH))
        cim[...] = jnp.broadcast_to(h0im_ref[...], (n_seq, S5_CH))

    for b in range(n_seq):
        ub = u_refs[b][...]
        for sl in range(S5_SLABS):
            u_tb[sl, pl.ds(b, tt, stride=n_seq), :] = ub[:, sl * LANES:(sl + 1) * LANES]
    u = jnp.concatenate([u_tb[sl] for sl in range(S5_SLABS)], axis=1)
    ubf = u.astype(BF16)
    for hf in range(2):
        xh = _dot(ubf[:, hf * S5_HALF_IN:(hf + 1) * S5_HALF_IN], bcat_ref[hf])
        xre[:, hf * S5_HALF_CH:(hf + 1) * S5_HALF_CH] = xh[:, :S5_HALF_CH]
        xim[:, hf * S5_HALF_CH:(hf + 1) * S5_HALF_CH] = xh[:, S5_HALF_CH:]

    for hf in range(2):
        cols = slice(hf * S5_HALF_CH, (hf + 1) * S5_HALF_CH)
        ar = are_ref[:, cols]
        ai = aim_ref[:, cols]

        def step(t, c):
            hr, hi = c
            rw = pl.ds(pl.multiple_of(t * n_seq, n_seq), n_seq)
            nr = xre[rw, cols] + (ar * hr - ai * hi)
            ni = xim[rw, cols] + (ar * hi + ai * hr)
            xre[rw, cols] = nr
            xim[rw, cols] = ni
            return nr, ni

        hr, hi = lax.fori_loop(0, tt, step, (cre[:, cols], cim[:, cols]), unroll=tt)
        cre[:, cols] = hr
        cim[:, cols] = hi
        hre_out[:, cols] = hr
        him_out[:, cols] = hi

    ys = []
    for hf in range(2):
        cols = slice(hf * S5_HALF_CH, (hf + 1) * S5_HALF_CH)
        ys.append(_dot(xre[:, cols].astype(BF16), cre_ref[hf]) + _dot(xim[:, cols].astype(BF16), cim_ref[hf]))
    y = jnp.concatenate(ys, axis=1) + d_ref[...] * u
    z = jax.nn.gelu(y)
    o = z * jax.nn.sigmoid(_dot(z.astype(BF16), wglu_ref[...]))
    for sl in range(S5_SLABS):
        o_tb[sl] = o[:, sl * LANES:(sl + 1) * LANES]
    for b in range(n_seq):
        ob = jnp.concatenate([o_tb[sl, pl.ds(b, tt, stride=n_seq), :] for sl in range(S5_SLABS)], axis=1)
        out_ref[b] = ob.astype(out_ref.dtype)


def _s5_batch_call(p, row0, n_seq, seq_len, h0re, h0im, prm, tt):
    assert n_seq == SUBLANES and seq_len % tt == 0 and row0 % tt == 0
    n_chunk = seq_len // tt
    blk0 = row0 // tt
    a8 = lambda a: jnp.broadcast_to(a[:1], (SUBLANES, S5_CH))
    consts = [prm["bcat"], prm["cre"], prm["cim"], a8(prm["apw_re"]), a8(prm["apw_im"]), prm["d"], prm["wglu"]]
    const_specs = [_const_spec(c.shape, (0,) * c.ndim) for c in consts]
    u_specs = [pl.BlockSpec((tt, S5_WIDTH), functools.partial(lambda b, c: (blk0 + b * n_chunk + c, 0), b))
               for b in range(n_seq)]
    rows = n_seq * tt
    out, hre, him = pl.pallas_call(
        functools.partial(_s5_batch_kernel, n_seq, tt),
        grid=(n_chunk,),
        in_specs=[*u_specs, _const_spec((1, S5_CH), (0, 0)), _const_spec((1, S5_CH), (0, 0)), *const_specs],
        out_specs=[pl.BlockSpec((n_seq, tt, S5_WIDTH), lambda c: (0, c, 0)),
                   pl.BlockSpec((n_seq, S5_CH), lambda c: (0, 0)), pl.BlockSpec((n_seq, S5_CH), lambda c: (0, 0))],
        out_shape=[jax.ShapeDtypeStruct((n_seq, seq_len, S5_WIDTH), BF16),
                   jax.ShapeDtypeStruct((n_seq, S5_CH), F32), jax.ShapeDtypeStruct((n_seq, S5_CH), F32)],
        scratch_shapes=[pltpu.VMEM((S5_SLABS, rows, LANES), F32), pltpu.VMEM((S5_SLABS, rows, LANES), F32),
                        pltpu.VMEM((rows, S5_CH), F32), pltpu.VMEM((rows, S5_CH), F32),
                        pltpu.VMEM((n_seq, S5_CH), F32), pltpu.VMEM((n_seq, S5_CH), F32)],
        compiler_params=_cparams(("arbitrary",)),
        name="s5_batch",
    )(*([p] * n_seq), h0re, h0im, *consts)
    return out.reshape(n_seq * seq_len, S5_WIDTH), hre, him


def _s5_params(a_re, a_im, log_dt, b_re, b_im, c_re, c_im, d_skip, w_glu):
    dt = jnp.exp(log_dt)[:, None]
    mag = jnp.exp(dt * a_re)
    abar_re, abar_im = mag * jnp.cos(dt * a_im), mag * jnp.sin(dt * a_im)
    den = a_re * a_re + a_im * a_im
    num_re = abar_re - 1.0
    f_re = (num_re * a_re + abar_im * a_im) / den
    f_im = (abar_im * a_re - num_re * a_im) / den
    bbar_re = f_re[..., None] * b_re - f_im[..., None] * b_im
    bbar_im = f_re[..., None] * b_im + f_im[..., None] * b_re

    def block_diag_in(w):
        w = w.reshape(2, S5_GROUPS // 2, S5_STATE, S5_GROUP)
        eye = jnp.eye(S5_GROUPS // 2, dtype=F32)
        return jnp.einsum("hgpn,gk->hgnkp", w, eye).reshape(2, S5_HALF_IN, S5_HALF_CH)

    def block_diag_out(w):
        w = w.reshape(2, S5_GROUPS // 2, S5_GROUP, S5_STATE)
        eye = jnp.eye(S5_GROUPS // 2, dtype=F32)
        return jnp.einsum("hgnp,gk->hgpkn", w, eye).reshape(2, S5_HALF_CH, S5_HALF_IN)

    bcat = jnp.concatenate([block_diag_in(bbar_re), block_diag_in(bbar_im)], axis=-1).astype(BF16)
    ar, ai = abar_re.reshape(1, S5_CH), abar_im.reshape(1, S5_CH)
    pows_re, pows_im = [ar], [ai]
    for _ in range(SUBLANES - 1):
        pr, pi = pows_re[-1], pows_im[-1]
        pows_re.append(pr * ar - pi * ai)
        pows_im.append(pr * ai + pi * ar)
    zeros = jnp.zeros((SUBLANES - 3, S5_CH), F32)
    return dict(
        bcat=bcat,
        cre=block_diag_out(c_re).astype(BF16),
        cim=block_diag_out(-c_im).astype(BF16),
        ast_re=jnp.concatenate([pows_re[0], pows_re[1], pows_re[3], zeros], axis=0),
        ast_im=jnp.concatenate([pows_im[0], pows_im[1], pows_im[3], zeros], axis=0),
        apw_re=jnp.concatenate(pows_re, axis=0),
        apw_im=jnp.concatenate(pows_im, axis=0),
        d=d_skip.reshape(1, S5_WIDTH),
        wglu=w_glu.astype(BF16),
    )


def _ret_tables(t):
    gam = np.log(1.0 - 2.0 ** (-5.0 - np.arange(RET_HEADS, dtype=np.float64)))
    tt = np.arange(t, dtype=np.float64)
    dq = np.exp(gam[:, None] * (tt[None, :] + 1.0))
    dq = np.broadcast_to(dq.reshape(RET_HEADS * t, 1), (RET_HEADS * t, RET_DV))
    diff = tt[:, None] - tt[None, :]
    dm = np.where(diff >= 0, np.exp(gam[:, None, None] * np.maximum(diff, 0.0)[None]), 0.0)
    dk = np.exp(gam[:, None] * (t - 1.0 - tt[None, :]))
    dk = np.repeat(dk.T, RET_DK, axis=1)
    ds = np.repeat(np.exp(gam * t), RET_DK)[:, None] * np.ones((1, RET_DV))
    hm = np.repeat(np.eye(RET_HEADS), RET_DK, axis=1)
    f = lambda a: jnp.asarray(np.ascontiguousarray(a), F32)
    return dict(dq=f(dq), dm=f(dm.reshape(RET_HEADS * t, t)), dk=f(dk), ds=f(ds),
                hm=f(np.concatenate([hm, np.zeros((SUBLANES - RET_HEADS, RET_QK))], axis=0)))


def _rope_tables(pos):
    half = RET_DK // 2
    inv_freq = 1.0 / (ROPE_BASE ** (jnp.arange(half, dtype=F32) / half))
    ang = pos.astype(F32)[:, None] * inv_freq[None, :]
    cos, sin = jnp.cos(ang), jnp.sin(ang)
    zero = jnp.zeros_like(sin)
    tile = lambda a, b: jnp.tile(jnp.concatenate([a, b], axis=1), (1, RET_HEADS))
    return tile(cos, cos), tile(-sin, zero), tile(zero, sin)


def _ret_chunk(t, q, k, v, g, s, cos, s_up, s_dn, dq, dm, dk, ds, hm):
    half = RET_DK // 2

    def rope(x):
        return x * cos + pltpu.roll(x, RET_QK - half, 1) * s_up + pltpu.roll(x, half, 1) * s_dn

    qr = rope(q)
    kr = rope(k) * (RET_DK ** -0.5)
    qs = jnp.concatenate([qr * hm[h:h + 1, :] for h in range(RET_HEADS)], axis=0).astype(BF16)
    inter = _dot(qs, s.astype(BF16)) * dq
    prob = (_dot_nt(qs, kr.astype(BF16)) * dm).astype(BF16)
    vb = v.astype(BF16)
    outs = []
    for h in range(RET_HEADS):
        rows = slice(h * t, (h + 1) * t)
        cols = slice(h * RET_DV, (h + 1) * RET_DV)
        o = inter[rows] + _dot(prob[rows], vb[:, cols])
        mu = jnp.mean(o, axis=-1, keepdims=True)
        oc = o - mu
        var = jnp.mean(oc * oc, axis=-1, keepdims=True)
        gh = g[:, cols]
        outs.append(oc * lax.rsqrt(var + EPS) * (gh * jax.nn.sigmoid(gh)))
    kd = kr * dk
    ks = jnp.concatenate([kd * hm[h:h + 1, :] for h in range(RET_HEADS)], axis=0).astype(BF16)
    vs = jnp.concatenate([vb[:, h * RET_DV:(h + 1) * RET_DV] for h in range(RET_HEADS)], axis=0)
    s_new = s * ds + _dot_tn(ks, vs)
    return jnp.concatenate(outs, axis=1), s_new


def _ret_kernel(chained, t, n_sub, q_ref, k_ref, v_ref, g_ref, s0_ref, cos_ref, sup_ref, sdn_ref, dq_ref, dm_ref,
                dk_ref, ds_ref, hm_ref, *rest):
    out_ref, st_ref, *scratch = rest[-3:] if chained else rest[-2:]
    tabs = (dq_ref[...], dm_ref[...], dk_ref[...], ds_ref[...], hm_ref[...])
    if chained:
        (s_scr,) = scratch

        @pl.when(pl.program_id(1) == 0)
        def _():
            s_scr[...] = s0_ref[...]

        o, s_new = _ret_chunk(t, q_ref[...], k_ref[...], v_ref[...], g_ref[...], s_scr[...],
                              cos_ref[...], sup_ref[...], sdn_ref[...], *tabs)
        _store_rows(out_ref, o)
        s_scr[...] = s_new
        st_ref[...] = s_new
    else:
        def one(i, carry):
            rows = pl.ds(pl.multiple_of(i * t, t), t)
            o, s_new = _ret_chunk(t, q_ref[rows, :], k_ref[rows, :], v_ref[rows, :], g_ref[rows, :], s0_ref[i],
                                  cos_ref[...], sup_ref[...], sdn_ref[...], *tabs)
            out_ref[rows, :] = o.astype(out_ref.dtype)
            st_ref[i] = s_new
            return carry

        lax.fori_loop(0, n_sub, one, 0, unroll=8)


def _stacked_state_io(n_sub, sd, layer_i, st_prev, n_in):
    spec = pl.BlockSpec((None, n_sub, *sd), lambda i: (layer_i, i, 0, 0))
    if st_prev is None:
        return spec, [], [], {}
    return spec, [pl.BlockSpec(memory_space=pl.ANY)], [st_prev], {n_in: 1}


def _ret_call(p, row0, n_seq, seq_len, s0, pos0, chained, t, n_sub=1, out_rows=None, layer_i=0, st_prev=None):
    n_rows = n_seq * seq_len
    rows_blk = t if chained else t * n_sub
    blk0 = row0 // rows_blk
    assert row0 % rows_blk == 0
    out_blk = rows_blk if out_rows is None else out_rows
    assert out_rows is None or n_rows == rows_blk
    tabs = _ret_tables(t)
    consts = [tabs[n] for n in ("dq", "dm", "dk", "ds", "hm")]
    const_specs = [_const_spec(c.shape, (0, 0)) for c in consts]
    cos, s_up, s_dn = _rope_tables(pos0 + jnp.arange(seq_len, dtype=jnp.int32))
    sd = RET_QK, RET_DV
    if chained:
        n_chunk = seq_len // t
        grid = (n_seq, n_chunk)
        rmap = lambda cb: (lambda b, c: (blk0 + b * n_chunk + c, cb))
        data_specs = [pl.BlockSpec((t, RET_QK), rmap(2)), pl.BlockSpec((t, RET_QK), rmap(3)),
                      pl.BlockSpec((t, RET_WIDTH), rmap(2)), pl.BlockSpec((t, RET_WIDTH), rmap(3)),
                      _const_spec(sd, (0, 0))]
        rope_specs = [pl.BlockSpec((t, RET_QK), lambda b, c: (c, 0))] * 3
        out_specs = [pl.BlockSpec((out_blk, RET_WIDTH), lambda b, c: (b * n_chunk + c, 0)),
                     pl.BlockSpec((None, *sd), lambda b, c: (b, 0, 0))]
        st_shape = (n_seq, *sd)
        extra_specs, extra_args, aliases = [], [], {}
        scratch = [pltpu.VMEM(sd, F32)]
        sem = ("arbitrary", "arbitrary")
    else:
        assert seq_len == t and n_seq % n_sub == 0
        grid = (n_seq // n_sub,)
        rmap = lambda cb: (lambda i: (blk0 + i, cb))
        st_spec, extra_specs, extra_args, aliases = _stacked_state_io(n_sub, sd, layer_i, st_prev, 13)
        data_specs = [pl.BlockSpec((rows_blk, RET_QK), rmap(2)), pl.BlockSpec((rows_blk, RET_QK), rmap(3)),
                      pl.BlockSpec((rows_blk, RET_WIDTH), rmap(2)), pl.BlockSpec((rows_blk, RET_WIDTH), rmap(3)),
                      st_spec]
        rope_specs = [_const_spec((t, RET_QK), (0, 0))] * 3
        out_specs = [pl.BlockSpec((rows_blk, RET_WIDTH), lambda i: (i, 0)), st_spec]
        st_shape = s0.shape
        scratch = []
        sem = ("arbitrary",)
    in_specs = [*data_specs, *rope_specs, *const_specs, *extra_specs]
    assert not aliases or list(aliases) == [len(in_specs) - 1]
    return pl.pallas_call(
        functools.partial(_ret_kernel, chained, t, n_sub),
        grid=grid,
        in_specs=in_specs,
        out_specs=out_specs,
        out_shape=[jax.ShapeDtypeStruct((n_rows // rows_blk * out_blk, RET_WIDTH), BF16),
                   jax.ShapeDtypeStruct(st_shape, F32)],
        scratch_shapes=scratch,
        input_output_aliases=aliases,
        compiler_params=_cparams(sem),
        name="retention",
    )(p, p, p, p, s0, cos, s_up, s_dn, *consts, *extra_args)


def _gla_tables(t, n_blk):
    cg = t * n_blk
    blk = np.arange(cg) // t
    same = blk[:, None] == blk[None, :]
    cum = (same & (np.arange(cg)[:, None] >= np.arange(cg)[None, :])).astype(np.float32)
    ones = (np.arange(t * GLA_DK)[:, None] // GLA_DK == np.arange(LANES)[None, :]).astype(np.float32)
    hm = np.repeat(np.eye(GLA_HEADS), GLA_DK, axis=1)
    hm = np.concatenate([hm, np.zeros((SUBLANES - GLA_HEADS, GLA_QK))], axis=0).astype(np.float32)
    return jnp.asarray(cum, BF16), jnp.asarray(ones, BF16), jnp.asarray(hm, F32)


LOG2E = 1.4426950408889634


def _gla_kernel(chained, t, n_blk, n_par, *refs):
    data = [refs[5 * s:5 * s + 5] for s in range(n_par)]
    s0_ref, wa_ref, ba_ref, ng_ref, cum_ref, ones_ref, hm_ref = refs[5 * n_par:5 * n_par + 7]
    rest = refs[5 * n_par + 7:]
    out_ref, st_ref, rt_scr, o_scr, *scratch = rest[-5:] if chained else rest[-4:]
    cg = t * n_blk
    nh, dk, dv = GLA_HEADS, GLA_DK, GLA_DV
    par = range(n_par)
    if chained:
        (s_scr,) = scratch

        @pl.when(pl.program_id(1) == 0)
        def _():
            for s in par:
                s_scr[s] = s0_ref[...]

    hm = hm_ref[...]
    q, ksc, vb, b, qe, ke, a_cols = [], [], [], [], [], [], []
    for s in par:
        q_ref, k_ref, v_ref, _, lr_ref = data[s]
        q.append(q_ref[...])
        ksc.append(k_ref[...] * (dk ** -0.5))
        vb.append(v_ref[...].astype(BF16))
        la = jax.nn.log_sigmoid(_dot(lr_ref[...].astype(BF16), wa_ref[...]) + ba_ref[...]) / GLA_TAU
        b.append(_dot_exact01(cum_ref[...], la) * LOG2E)
        last = b[s].reshape(n_blk, t, GLA_QK)[:, t - 1:t, :]
        bl = jnp.broadcast_to(last, (n_blk, t, GLA_QK)).reshape(cg, GLA_QK)
        tot = jnp.concatenate([last.reshape(n_blk, GLA_QK), jnp.zeros((LANES - n_blk, GLA_QK), F32)], axis=0)
        a_cols.append(jnp.exp2(tot).T)
        qe.append(q[s] * jnp.exp2(b[s]))
        ke.append(ksc[s] * jnp.exp2(bl - b[s]))

    row_t = {lo: lo + lax.broadcasted_iota(jnp.int32, (t - lo, GLA_QK), 0) for lo in range(0, t, SUBLANES)}
    state = [s_scr[s] for s in par] if chained else [None]
    for j in range(n_blk):
        rows = slice(j * t, (j + 1) * t)
        for s in par:
            qj, kj, bj = q[s][rows], ksc[s][rows], b[s][rows]
            for i in range(t):
                lo = i // SUBLANES * SUBLANES
                e = jnp.exp2(jnp.where(row_t[lo] >= i, bj[lo:] - bj[i:i + 1, :], NEG_BIG))
                prod = (qj[lo:] * kj[i:i + 1, :]) * e
                for h in range(nh):
                    ph = prod[:, h * dk:(h + 1) * dk]
                    if lo:
                        ph = jnp.concatenate([jnp.zeros((lo, dk), F32), ph], axis=0)
                    rt_scr[s, (j * nh + h) * t:(j * nh + h + 1) * t, i * dk:(i + 1) * dk] = ph.astype(rt_scr.dtype)
        for s in par:
            if not chained:
                state[s] = s0_ref[j]
            qs = jnp.concatenate([qe[s][rows] * hm[h:h + 1, :] for h in range(nh)], axis=0).astype(BF16)
            inter = _dot(qs, state[s].astype(BF16))
            for h in range(nh):
                o_scr[s, rows, h * dv:(h + 1) * dv] = inter[h * t:(h + 1) * t]
            ks = jnp.concatenate([ke[s][rows] * hm[h:h + 1, :] for h in range(nh)], axis=0).astype(BF16)
            vs = jnp.concatenate([vb[s][rows, h * dv:(h + 1) * dv] for h in range(nh)], axis=0)
            state[s] = state[s] * a_cols[s][:, j:j + 1] + _dot_tn(ks, vs)
            if not chained:
                st_ref[j] = state[s]
    for s in par:
        if chained:
            s_scr[s] = state[s]
            st_ref[s] = state[s]
        scores = _dot(rt_scr[s].astype(BF16), ones_ref[...])
        o = o_scr[s]
        r = data[s][3][...]
        gated = []
        for h in range(nh):
            pieces = []
            for j in range(n_blk):
                piece = scores[(j * nh + h) * t:(j * nh + h + 1) * t, :]
                pieces.append(pltpu.roll(piece, j * t, 1) if j else piece)
            pfull = jnp.concatenate(pieces, axis=0) if n_blk > 1 else pieces[0]
            cols = slice(h * dv, (h + 1) * dv)
            oh = o[:, cols] + _dot(pfull[:, :cg].astype(BF16), vb[s][:, cols])
            on = oh * lax.rsqrt(jnp.mean(oh * oh, axis=-1, keepdims=True) + EPS) * ng_ref[...]
            rh = r[:, cols]
            gated.append(on * (rh * jax.nn.sigmoid(rh)))
        _store_rows(out_ref.at[s] if chained else out_ref, jnp.concatenate(gated, axis=1))


def _gla_call(p, row0, n_seq, seq_len, s0, wa, ba, ng, chained, t, n_blk, n_par=1, out_rows=None, layer_i=0,
              st_prev=None):
    cg = t * n_blk
    blk0 = row0 // cg
    assert row0 % cg == 0 and cg <= LANES
    out_blk = cg if out_rows is None else out_rows
    assert out_rows is None or seq_len == cg
    cum, ones, hm = _gla_tables(t, n_blk)
    consts = [wa, ba, ng, cum, ones, hm]
    const_specs = [_const_spec(c.shape, (0,) * c.ndim) for c in consts]
    sd = GLA_QK, GLA_DV
    lr_col = (2 * GLA_QK + 2 * GLA_V) // LANES
    rt_dtype = BF16 if t % (2 * SUBLANES) == 0 else F32
    if chained:
        assert n_seq % n_par == 0
        n_chunk = seq_len // cg
        grid = (n_seq // n_par, n_chunk)
        rmap = lambda s, cb: (lambda b, c: (blk0 + (b * n_par + s) * n_chunk + c, cb))
        s_spec = _const_spec(sd, (0, 0))
        out_specs = [pl.BlockSpec((n_par, out_blk, GLA_V), lambda b, c: (b, c, 0)),
                     pl.BlockSpec((n_par, *sd), lambda b, c: (b, 0, 0))]
        out_shape = [jax.ShapeDtypeStruct((n_seq, n_chunk * out_blk, GLA_V), BF16),
                     jax.ShapeDtypeStruct((n_seq, *sd), F32)]
        extra_specs, extra_args, aliases = [], [], {}
        scratch = [pltpu.VMEM((n_par, *sd), F32)]
        sem = ("arbitrary", "arbitrary")
    else:
        assert seq_len == t and n_seq % n_blk == 0 and n_par == 1
        grid = (n_seq // n_blk,)
        rmap = lambda s, cb: (lambda i: (blk0 + i, cb))
        s_spec, extra_specs, extra_args, aliases = _stacked_state_io(n_blk, sd, layer_i, st_prev, 12)
        out_specs = [pl.BlockSpec((cg, GLA_V), lambda i: (i, 0)), s_spec]
        out_shape = [jax.ShapeDtypeStruct((n_seq * seq_len, GLA_V), BF16), jax.ShapeDtypeStruct(s0.shape, F32)]
        scratch = []
        sem = ("arbitrary",)
    data_specs = []
    for s in range(n_par):
        data_specs += [pl.BlockSpec((cg, GLA_QK), rmap(s, 0)), pl.BlockSpec((cg, GLA_QK), rmap(s, 1)),
                       pl.BlockSpec((cg, GLA_V), rmap(s, 1)), pl.BlockSpec((cg, GLA_V), rmap(s, 2)),
                       pl.BlockSpec((cg, LANES), rmap(s, lr_col))]
    in_specs = [*data_specs, s_spec, *const_specs, *extra_specs]
    assert not aliases or list(aliases) == [len(in_specs) - 1]
    out, st = pl.pallas_call(
        functools.partial(_gla_kernel, chained, t, n_blk, n_par),
        grid=grid,
        in_specs=in_specs,
        out_specs=out_specs,
        out_shape=out_shape,
        scratch_shapes=[pltpu.VMEM((n_par, n_blk * GLA_HEADS * t, t * GLA_DK), rt_dtype),
                        pltpu.VMEM((n_par, cg, GLA_V), F32), *scratch],
        input_output_aliases=aliases,
        compiler_params=_cparams(sem),
        name="gla",
    )(*([p] * (5 * n_par)), s0, *consts, *extra_args)
    return out.reshape(-1, GLA_V), st


S5_TT_PROMPT = 128
S5_TC_SAMPLE = 256
RET_T_PROMPT = 256
RET_SUB_SAMPLE = 16
GLA_T_PROMPT = 16
GLA_BLK_PROMPT = 8
GLA_BLK_SAMPLE = 8
GLA_PAR_PROMPT = 2


def _mixer_ab(p, i, prm, st_s5_re, st_s5_im, st_ret, s_ret_prev):
    zeros_h = jnp.zeros((1, S5_CH), F32)
    a_m, hre_m, him_m = _s5_call(p, ROW0_META, 1, N_META, zeros_h, zeros_h, prm, True, N_META, out_rows=DENSE_TM)
    a_p, hre_p, him_p = _s5_batch_call(p, 0, BATCH, SEQ, hre_m[0], him_m[0], prm, S5_TT_PROMPT)
    h0re = jnp.repeat(st_s5_re[i].reshape(DEC_BATCH, S5_CH), DEC_SEQ, axis=0)
    h0im = jnp.repeat(st_s5_im[i].reshape(DEC_BATCH, S5_CH), DEC_SEQ, axis=0)
    a_s, hre_s, him_s = _s5_call(p, ROW0_SAMPLE, DEC_BATCH, DEC_SEQ, h0re, h0im, prm, False, S5_TC_SAMPLE)
    last = lambda h: h.reshape(DEC_BATCH, DEC_SEQ, S5_CH)[:, DEC_SEQ - 1].reshape(DEC_BATCH, S5_GROUPS, S5_STATE)

    zeros_s = jnp.zeros((RET_QK, RET_DV), F32)
    b_m, s_m = _ret_call(p, ROW0_META, 1, N_META, zeros_s, 0, True, N_META, out_rows=DENSE_TM)
    b_p, s_p = _ret_call(p, 0, BATCH, SEQ, s_m[0], N_META, True, RET_T_PROMPT)
    b_s, s_ret = _ret_call(p, ROW0_SAMPLE, DEC_BATCH, DEC_SEQ, st_ret, PAST_LEN, False, DEC_SEQ, RET_SUB_SAMPLE,
                           layer_i=i, st_prev=s_ret_prev)
    states = dict(
        p_s5_re=hre_p.reshape(BATCH, S5_GROUPS, S5_STATE), p_s5_im=him_p.reshape(BATCH, S5_GROUPS, S5_STATE),
        p_ret=s_p.reshape(BATCH, RET_HEADS, RET_DK, RET_DV), s_s5_re=last(hre_s), s_s5_im=last(him_s))
    return [(a_p, a_s, a_m), (b_p, b_s, b_m)], states, s_ret


def _mixer_gla(p, i, wa, ba, ng, st_gla, s_gla_prev):
    zeros_s = jnp.zeros((GLA_QK, GLA_DV), F32)
    o_m, s_m = _gla_call(p, ROW0_META, 1, N_META, zeros_s, wa, ba, ng, True, N_META, 1, out_rows=DENSE_TM)
    o_p, s_p = _gla_call(p, 0, BATCH, SEQ, s_m[0], wa, ba, ng, True, GLA_T_PROMPT, GLA_BLK_PROMPT,
                         n_par=GLA_PAR_PROMPT)
    o_s, s_gla = _gla_call(p, ROW0_SAMPLE, DEC_BATCH, DEC_SEQ, st_gla, wa, ba, ng, False, DEC_SEQ, GLA_BLK_SAMPLE,
                           layer_i=i, st_prev=s_gla_prev)
    states = dict(p_gla=s_p.reshape(BATCH, GLA_HEADS, GLA_DK, GLA_DV))
    return [(o_p, o_s, o_m)], states, s_gla


def kernel(x_prompt, x_sample, state_s5_re, state_s5_im, state_ret, state_gla, meta_tokens, norm_ffn1, norm_mix,
           norm_ffn2, norm_final, ffn1_w_gu, ffn1_w_down, ffn2_w_gu, ffn2_w_down, ab_w_in, ab_w_out, s5_a_re,
           s5_a_im, s5_log_dt, s5_b_re, s5_b_im, s5_c_re, s5_c_im, s5_d, s5_w_glu, gla_w_in, gla_w_alpha2,
           gla_b_alpha, gla_norm, gla_w_out):
    ffn1_gu, ffn1_dn, ffn2_gu, ffn2_dn = ffn1_w_gu, ffn1_w_down, ffn2_w_gu, ffn2_w_down
    ab_in, ab_out, gla_in, gla_out = ab_w_in, ab_w_out, gla_w_in, gla_w_out
    gla_wa = jnp.pad(gla_w_alpha2, ((0, 0), (0, LANES - GLA_LOWRANK), (0, 0))).astype(BF16)
    n1 = norm_ffn1.reshape(DEPTH, 1, D_MODEL)
    nm = norm_mix.reshape(DEPTH, 1, D_MODEL)
    n2 = norm_ffn2.reshape(DEPTH, 1, D_MODEL)

    meta_pad = jnp.pad(meta_tokens.astype(x_prompt.dtype), ((0, DENSE_TM - N_META), (0, 0)))
    xs = [x_prompt.reshape(ROWS_PROMPT, D_MODEL), x_sample.reshape(ROWS_SAMPLE, D_MODEL), meta_pad]
    st_ret = state_ret.reshape(N_EVEN, DEC_BATCH, RET_QK, RET_DV)
    st_gla = state_gla.reshape(N_ODD, DEC_BATCH, GLA_QK, GLA_DV)
    s_ret = s_gla = None
    collected = {}
    for layer in range(DEPTH):
        i = layer // 2
        last = layer == DEPTH - 1
        g_final = norm_final.reshape(1, D_MODEL) if last else None
        if layer % 2 == 0:
            x1, p = _pre_call(xs, n1, ffn1_gu, ffn1_dn, nm, ab_in, layer, i, AB_IN)
            prm = _s5_params(s5_a_re[i], s5_a_im[i], s5_log_dt[i], s5_b_re[i], s5_b_im[i], s5_c_re[i],
                             s5_c_im[i], s5_d[i], s5_w_glu[i])
            mixes, states, s_ret = _mixer_ab(p, i, prm, state_s5_re, state_s5_im, st_ret, s_ret)
            x = _post_call(x1, mixes, ab_out, n2, ffn2_gu, ffn2_dn, layer, i, g_final)
        else:
            x1, p = _pre_call(xs, n1, ffn1_gu, ffn1_dn, nm, gla_in, layer, i, GLA_IN_PAD)
            mixes, states, s_gla = _mixer_gla(p, i, gla_wa[i], gla_b_alpha[i].reshape(1, GLA_QK),
                                              gla_norm[i].reshape(1, GLA_DV), st_gla, s_gla)
            x = _post_call(x1, mixes, gla_out, n2, ffn2_gu, ffn2_dn, layer, i, g_final)
        xs = [x]
        for name, val in states.items():
            collected.setdefault(name, []).append(val)
    out = {name: jnp.stack(vals) for name, vals in collected.items()}
    y_prompt, y_rest = x
    return (y_prompt.reshape(BATCH, SEQ, D_MODEL), y_rest[:ROWS_SAMPLE].reshape(DEC_BATCH, DEC_SEQ, D_MODEL),
            out["p_s5_re"], out["p_s5_im"], out["p_ret"], out["p_gla"], out["s_s5_re"], out["s_s5_im"],
            s_ret.reshape(N_EVEN, DEC_BATCH, RET_HEADS, RET_DK, RET_DV),
            s_gla.reshape(N_ODD, DEC_BATCH, GLA_HEADS, GLA_DK, GLA_DV))
```

```python
import functools
import math

import numpy as np
import jax
import jax.numpy as jnp
from jax import lax
from jax.experimental import pallas as pl
from jax.experimental.pallas import tpu as pltpu

F32 = jnp.float32
BF16 = jnp.bfloat16

D_MODEL = 1024
BATCH = 8
SEQ = 2048
DEPTH = 4
DEC_BATCH = 128
DEC_SEQ = 8
PAST_LEN = 16384
N_META = 16
N_EVEN = (DEPTH + 1) // 2
N_ODD = DEPTH // 2
S5_WIDTH = D_MODEL // 2
S5_GROUP = 16
S5_GROUPS = S5_WIDTH // S5_GROUP
S5_STATE = 64
S5_CH = S5_GROUPS * S5_STATE
RET_HEADS = 4
RET_DK = D_MODEL // 16
RET_DV = 2 * RET_DK
RET_QK = RET_HEADS * RET_DK
RET_WIDTH = RET_HEADS * RET_DV
AB_IN = S5_WIDTH + 2 * RET_QK + 2 * RET_WIDTH
AB_OUT = S5_WIDTH + RET_WIDTH
GLA_HEADS = 4
GLA_DK = D_MODEL // (2 * GLA_HEADS)
GLA_DV = D_MODEL // GLA_HEADS
GLA_QK = GLA_HEADS * GLA_DK
GLA_V = GLA_HEADS * GLA_DV
GLA_LOWRANK = 16
GLA_TAU = 16.0
GLA_IN = 2 * GLA_QK + 2 * GLA_V + GLA_LOWRANK
LANES = 128
GLA_IN_PAD = 2 * GLA_QK + 2 * GLA_V + LANES
D_FF = 128 * ((8 * D_MODEL // 3 + 127) // 128)
EPS = 1e-6
ROPE_BASE = 10000.0
NEG_BIG = -1e30

ROWS_PROMPT = BATCH * SEQ
ROWS_SAMPLE = DEC_BATCH * DEC_SEQ
ROW0_SAMPLE = ROWS_PROMPT
ROW0_META = ROWS_PROMPT + ROWS_SAMPLE
ROWS = ROW0_META + N_META

VMEM_LIMIT = 56 * 1024 * 1024
DENSE_TM = 256
TILES_PROMPT = ROWS_PROMPT // DENSE_TM
TILES_SAMPLE = ROWS_SAMPLE // DENSE_TM
N_TILES = TILES_PROMPT + TILES_SAMPLE + 1
ROWS_PAD = N_TILES * DENSE_TM
ROWS_REST = ROWS_PAD - ROWS_PROMPT


def _cparams(sem):
    return pltpu.CompilerParams(dimension_semantics=sem, vmem_limit_bytes=VMEM_LIMIT)


def _dot(a, b):
    return jnp.dot(a, b, preferred_element_type=F32)


def _dot_tn(a, b):
    return lax.dot_general(a, b, (((0,), (0,)), ((), ())), preferred_element_type=F32)


def _dot_nt(a, b):
    return lax.dot_general(a, b, (((1,), (1,)), ((), ())), preferred_element_type=F32)


def _dot_exact01(m_bf, x):
    h1 = x.astype(BF16)
    r1 = x - h1.astype(F32)
    h2 = r1.astype(BF16)
    r2 = r1 - h2.astype(F32)
    h3 = r2.astype(BF16)
    return _dot(m_bf, h1) + _dot(m_bf, h2) + _dot(m_bf, h3)


def _rms(x, g):
    return x * lax.rsqrt(jnp.mean(x * x, axis=-1, keepdims=True) + EPS) * g


def _swiglu_half(x, g, wg, wu, wd):
    h = _rms(x, g).astype(BF16)
    gate = _dot(h, wg)
    up = _dot(h, wu)
    act = (gate * jax.nn.sigmoid(gate) * up).astype(BF16)
    return x + 0.5 * _dot(act, wd)


def _store_rows(out_ref, val):
    rows = val.shape[0]
    out_ref[0:rows, :] = val.astype(out_ref.dtype)
    if out_ref.shape[0] > rows:
        out_ref[rows:, :] = jnp.zeros((out_ref.shape[0] - rows, out_ref.shape[1]), out_ref.dtype)


def _const_spec(shape, index):
    return pl.BlockSpec(shape, lambda *_: index, pipeline_mode=pl.Buffered(1))


N_WCHUNK = 16


def _tile(g):
    return jnp.maximum(g - N_WCHUNK, 0)


def _row_spec(width):
    return pl.BlockSpec((DENSE_TM, width), lambda g: (_tile(g), 0))


def _group_specs(width):
    tm = DENSE_TM
    return [pl.BlockSpec((tm, width), lambda g: (jnp.minimum(_tile(g), TILES_PROMPT - 1), 0)),
            pl.BlockSpec((tm, width), lambda g: (jnp.clip(_tile(g) - TILES_PROMPT, 0, TILES_SAMPLE - 1), 0)),
            pl.BlockSpec((tm, width), lambda g: (0, 0))]


def _pick_group(refs):
    if len(refs) == 1:
        return refs[0][...]
    i = pl.program_id(0) - N_WCHUNK
    return jnp.where(i < TILES_PROMPT, refs[0][...],
                     jnp.where(i < TILES_PROMPT + TILES_SAMPLE, refs[1][...], refs[2][...]))


def _chunk_spec(w, lead):
    _, rows, cols = w.shape
    assert rows % (N_WCHUNK * 2 * SUBLANES) == 0
    return pl.BlockSpec((None, rows // N_WCHUNK, cols), lambda g: (lead, jnp.minimum(g, N_WCHUNK - 1), 0))


def _stage(dst, chunk, g, cols=None):
    rows = chunk.shape[0]
    r = pl.ds(pl.multiple_of(g * rows, rows), rows)
    val = chunk[...] if cols is None else chunk[:, cols]
    width = val.shape[1]
    dst[r, 0:width] = val.astype(BF16)
    if dst.shape[1] > width:
        dst[r, width:] = jnp.zeros((rows, dst.shape[1] - width), BF16)


def _pre_kernel(n_x, *refs):
    x_refs = refs[:n_x]
    g1_ref, wgu_ref, wd_ref, g2_ref, win_ref, x1_ref, p_ref, wg_s, wu_s, wd_s, win_s = refs[n_x:]
    g = pl.program_id(0)

    @pl.when(g < N_WCHUNK)
    def _():
        _stage(wg_s, wgu_ref, g, slice(0, D_FF))
        _stage(wu_s, wgu_ref, g, slice(D_FF, 2 * D_FF))
        _stage(wd_s, wd_ref, g)
        _stage(win_s, win_ref, g)

    @pl.when(g >= N_WCHUNK)
    def _():
        x1 = _swiglu_half(_pick_group(x_refs), g1_ref[...], wg_s[...], wu_s[...], wd_s[...])
        x1_ref[...] = x1
        p_ref[...] = _dot(_rms(x1, g2_ref[...]).astype(BF16), win_s[...])


def _ffn_scratch():
    return [pltpu.VMEM((D_MODEL, D_FF), BF16), pltpu.VMEM((D_MODEL, D_FF), BF16), pltpu.VMEM((D_FF, D_MODEL), BF16)]


def _pre_call(xs, g1, w_gu, w_down, g2, w_in, layer, mix_idx, n_in):
    x_specs = [_row_spec(D_MODEL)] if len(xs) == 1 else _group_specs(D_MODEL)
    return pl.pallas_call(
        functools.partial(_pre_kernel, len(xs)),
        grid=(N_WCHUNK + N_TILES,),
        in_specs=[
            *x_specs,
            _const_spec((None, 1, D_MODEL), (layer, 0, 0)),
            _chunk_spec(w_gu, layer),
            _chunk_spec(w_down, layer),
            _const_spec((None, 1, D_MODEL), (layer, 0, 0)),
            _chunk_spec(w_in, mix_idx),
        ],
        out_specs=[_row_spec(D_MODEL), _row_spec(n_in)],
        out_shape=[jax.ShapeDtypeStruct((ROWS_PAD, D_MODEL), F32), jax.ShapeDtypeStruct((ROWS_PAD, n_in), F32)],
        scratch_shapes=[*_ffn_scratch(), pltpu.VMEM((D_MODEL, n_in), BF16)],
        compiler_params=_cparams(("arbitrary",)),
        name="pre",
    )(*xs, g1, w_gu, w_down, g2, w_in)


def _post_kernel(n_mix, final, *refs):
    x1_ref = refs[0]
    mix_refs = refs[1:1 + 3 * n_mix]
    wout_ref, g_ref, wgu_ref, wd_ref = refs[1 + 3 * n_mix:5 + 3 * n_mix]
    rest = refs[5 + 3 * n_mix:]
    wout_s, wg_s, wu_s, wd_s = rest[-4:]
    g = pl.program_id(0)

    @pl.when(g < N_WCHUNK)
    def _():
        _stage(wout_s, wout_ref, g)
        _stage(wg_s, wgu_ref, g, slice(0, D_FF))
        _stage(wu_s, wgu_ref, g, slice(D_FF, 2 * D_FF))
        _stage(wd_s, wd_ref, g)

    @pl.when(g >= N_WCHUNK)
    def _():
        x2 = x1_ref[...]
        width = AB_OUT // n_mix
        for i in range(n_mix):
            x2 = x2 + _dot(_pick_group(mix_refs[3 * i:3 * i + 3]), wout_s[i * width:(i + 1) * width, :])
        y = _swiglu_half(x2, g_ref[...], wg_s[...], wu_s[...], wd_s[...])
        if final:
            gf_ref, yp_ref, yr_ref = rest[:3]
            y = _rms(y, gf_ref[...])

            @pl.when(g < N_WCHUNK + TILES_PROMPT)
            def _():
                yp_ref[...] = y

            @pl.when(g >= N_WCHUNK + TILES_PROMPT)
            def _():
                yr_ref[...] = y
        else:
            rest[0][...] = y


def _post_call(x1, mixes, w_out, g, w_gu, w_down, layer, mix_idx, g_final):
    tm = DENSE_TM
    final = g_final is not None
    in_specs = [_row_spec(D_MODEL)]
    for triple in mixes:
        in_specs += _group_specs(triple[0].shape[1])
    in_specs += [
        _chunk_spec(w_out, mix_idx),
        _const_spec((None, 1, D_MODEL), (layer, 0, 0)),
        _chunk_spec(w_gu, layer),
        _chunk_spec(w_down, layer),
    ]
    args = [x1, *[m for triple in mixes for m in triple], w_out, g, w_gu, w_down]
    if final:
        in_specs.append(_const_spec((1, D_MODEL), (0, 0)))
        args.append(g_final)
        out_specs = [pl.BlockSpec((tm, D_MODEL), lambda s: (jnp.minimum(_tile(s), TILES_PROMPT - 1), 0)),
                     pl.BlockSpec((tm, D_MODEL), lambda s: (jnp.maximum(_tile(s) - TILES_PROMPT, 0), 0))]
        out_shape = [jax.ShapeDtypeStruct((ROWS_PROMPT, D_MODEL), F32),
                     jax.ShapeDtypeStruct((ROWS_REST, D_MODEL), F32)]
    else:
        out_specs = _row_spec(D_MODEL)
        out_shape = jax.ShapeDtypeStruct((ROWS_PAD, D_MODEL), F32)
    return pl.pallas_call(
        functools.partial(_post_kernel, len(mixes), final),
        grid=(N_WCHUNK + N_TILES,),
        in_specs=in_specs,
        out_specs=out_specs,
        out_shape=out_shape,
        scratch_shapes=[pltpu.VMEM((AB_OUT, D_MODEL), BF16), *_ffn_scratch()],
        compiler_params=_cparams(("arbitrary",)),
        name="post",
    )(*args)


S5_HALF_IN = S5_WIDTH // 2
S5_HALF_CH = S5_CH // 2
SUBLANES = 8


def _s5_kernel(chained, tc, u_ref, h0re_ref, h0im_ref, bcat_ref, cre_ref, cim_ref, ast_re_ref, ast_im_ref,
               apw_re_ref, apw_im_ref, d_ref, wglu_ref, out_ref, hre_out, him_out, xre, xim, *carry):
    nb = tc // SUBLANES
    u = u_ref[...]
    ub = u.astype(BF16)
    for hf in range(2):
        xh = _dot(ub[:, hf * S5_HALF_IN:(hf + 1) * S5_HALF_IN], bcat_ref[hf])
        xre[:, hf * S5_HALF_CH:(hf + 1) * S5_HALF_CH] = xh[:, :S5_HALF_CH]
        xim[:, hf * S5_HALF_CH:(hf + 1) * S5_HALF_CH] = xh[:, S5_HALF_CH:]

    sr = xre[...].reshape(nb, SUBLANES, S5_CH)
    si = xim[...].reshape(nb, SUBLANES, S5_CH)
    rowi = lax.broadcasted_iota(jnp.int32, (nb, SUBLANES, S5_CH), 1)
    for step, d in enumerate((1, 2, 4)):
        ar = ast_re_ref[step:step + 1, :][None]
        ai = ast_im_ref[step:step + 1, :][None]
        pr = pltpu.roll(sr, d, 1)
        pi = pltpu.roll(si, d, 1)
        keep = rowi >= d
        sr, si = (sr + jnp.where(keep, ar * pr - ai * pi, 0.0),
                  si + jnp.where(keep, ar * pi + ai * pr, 0.0))
    apr = apw_re_ref[...]
    api = apw_im_ref[...]
    if chained:
        cre, cim = carry
        xre[...] = sr.reshape(tc, S5_CH)
        xim[...] = si.reshape(tc, S5_CH)

        @pl.when(pl.program_id(1) == 0)
        def _():
            cre[...] = h0re_ref[...]
            cim[...] = h0im_ref[...]

        def group(r, c):
            hr, hi = c
            rows = pl.ds(pl.multiple_of(r * SUBLANES, SUBLANES), SUBLANES)
            nr = xre[rows, :] + apr * hr - api * hi
            ni = xim[rows, :] + apr * hi + api * hr
            xre[rows, :] = nr
            xim[rows, :] = ni
            return nr[SUBLANES - 1:, :], ni[SUBLANES - 1:, :]

        hr, hi = lax.fori_loop(0, nb, group, (cre[...], cim[...]), unroll=min(nb, 4))
        cre[...] = hr
        cim[...] = hi
        hre_out[...] = hr
        him_out[...] = hi
    else:
        h0r = h0re_ref[...].reshape(nb, SUBLANES, S5_CH)
        h0i = h0im_ref[...].reshape(nb, SUBLANES, S5_CH)
        fr = (sr + apr[None] * h0r - api[None] * h0i).reshape(tc, S5_CH)
        fi = (si + apr[None] * h0i + api[None] * h0r).reshape(tc, S5_CH)
        xre[...] = fr
        xim[...] = fi
        hre_out[...] = fr
        him_out[...] = fi

    ys = []
    for hf in range(2):
        cols = slice(hf * S5_HALF_CH, (hf + 1) * S5_HALF_CH)
        ys.append(_dot(xre[:, cols].astype(BF16), cre_ref[hf]) + _dot(xim[:, cols].astype(BF16), cim_ref[hf]))
    y = jnp.concatenate(ys, axis=1) + d_ref[...] * u
    z = jax.nn.gelu(y)
    _store_rows(out_ref, z * jax.nn.sigmoid(_dot(z.astype(BF16), wglu_ref[...])))


def _s5_call(p, row0, n_seq, seq_len, h0re, h0im, prm, chained, tc, out_rows=None):
    n_rows = n_seq * seq_len
    blk0 = row0 // tc
    assert row0 % tc == 0 and n_rows % tc == 0
    out_blk = tc if out_rows is None else out_rows
    assert out_rows is None or n_rows == tc
    consts = [prm["bcat"], prm["cre"], prm["cim"], prm["ast_re"], prm["ast_im"], prm["apw_re"], prm["apw_im"],
              prm["d"], prm["wglu"]]
    const_specs = [_const_spec(c.shape, (0,) * c.ndim) for c in consts]
    scratch = [pltpu.VMEM((tc, S5_CH), F32), pltpu.VMEM((tc, S5_CH), F32)]
    if chained:
        n_chunk = seq_len // tc
        grid = (n_seq, n_chunk)
        u_spec = pl.BlockSpec((tc, S5_WIDTH), lambda b, c: (blk0 + b * n_chunk + c, 0))
        h_specs = [_const_spec((1, S5_CH), (0, 0))] * 2
        out_specs = [pl.BlockSpec((out_blk, S5_WIDTH), lambda b, c: (b * n_chunk + c, 0)),
                     pl.BlockSpec((None, 1, S5_CH), lambda b, c: (b, 0, 0)),
                     pl.BlockSpec((None, 1, S5_CH), lambda b, c: (b, 0, 0))]
        out_shape = [jax.ShapeDtypeStruct((n_rows // tc * out_blk, S5_WIDTH), BF16),
                     jax.ShapeDtypeStruct((n_seq, 1, S5_CH), F32), jax.ShapeDtypeStruct((n_seq, 1, S5_CH), F32)]
        scratch += [pltpu.VMEM((1, S5_CH), F32), pltpu.VMEM((1, S5_CH), F32)]
        sem = ("arbitrary", "arbitrary")
    else:
        assert seq_len == SUBLANES
        grid = (n_rows // tc,)
        u_spec = pl.BlockSpec((tc, S5_WIDTH), lambda i: (blk0 + i, 0))
        h_specs = [pl.BlockSpec((tc, S5_CH), lambda i: (i, 0))] * 2
        out_specs = [pl.BlockSpec((tc, S5_WIDTH), lambda i: (i, 0)),
                     pl.BlockSpec((tc, S5_CH), lambda i: (i, 0)), pl.BlockSpec((tc, S5_CH), lambda i: (i, 0))]
        out_shape = [jax.ShapeDtypeStruct((n_rows, S5_WIDTH), BF16),
                     jax.ShapeDtypeStruct((n_rows, S5_CH), F32), jax.ShapeDtypeStruct((n_rows, S5_CH), F32)]
        sem = ("arbitrary",)
    return pl.pallas_call(
        functools.partial(_s5_kernel, chained, tc),
        grid=grid,
        in_specs=[u_spec, *h_specs, *const_specs],
        out_specs=out_specs,
        out_shape=out_shape,
        scratch_shapes=scratch,
        compiler_params=_cparams(sem),
        name="s5",
    )(p, h0re, h0im, *consts)


S5_SLABS = S5_WIDTH // LANES


def _s5_batch_kernel(n_seq, tt, *refs):
    u_refs = refs[:n_seq]
    (h0re_ref, h0im_ref, bcat_ref, cre_ref, cim_ref, are_ref, aim_ref, d_ref, wglu_ref,
     out_ref, hre_out, him_out, u_tb, o_tb, xre, xim, cre, cim) = refs[n_seq:]
    @pl.when(pl.program_id(0) == 0)
    def _():
        cre[...] = jnp.broadcast_to(h0re_ref[...], (n_seq, S5_CH))
        cim[...] = jnp.broadcast_to(h0im_ref[...], (n_seq, S5_CH))

    for b in range(n_seq):
        ub = u_refs[b][...]
        for sl in range(S5_SLABS):
            u_tb[sl, pl.ds(b, tt, stride=n_seq), :] = ub[:, sl * LANES:(sl + 1) * LANES]
    u = jnp.concatenate([u_tb[sl] for sl in range(S5_SLABS)], axis=1)
    ubf = u.astype(BF16)
    for hf in range(2):
        xh = _dot(ubf[:, hf * S5_HALF_IN:(hf + 1) * S5_HALF_IN], bcat_ref[hf])
        xre[:, hf * S5_HALF_CH:(hf + 1) * S5_HALF_CH] = xh[:, :S5_HALF_CH]
        xim[:, hf * S5_HALF_CH:(hf + 1) * S5_HALF_CH] = xh[:, S5_HALF_CH:]

    for hf in range(2):
        cols = slice(hf * S5_HALF_CH, (hf + 1) * S5_HALF_CH)
        ar = are_ref[:, cols]
        ai = aim_ref[:, cols]

        def step(t, c):
            hr, hi = c
            rw = pl.ds(pl.multiple_of(t * n_seq, n_seq), n_seq)
            nr = xre[rw, cols] + (ar * hr - ai * hi)
            ni = xim[rw, cols] + (ar * hi + ai * hr)
            xre[rw, cols] = nr
            xim[rw, cols] = ni
            return nr, ni

        hr, hi = lax.fori_loop(0, tt, step, (cre[:, cols], cim[:, cols]), unroll=tt)
        cre[:, cols] = hr
        cim[:, cols] = hi
        hre_out[:, cols] = hr
        him_out[:, cols] = hi

    ys = []
    for hf in range(2):
        cols = slice(hf * S5_HALF_CH, (hf + 1) * S5_HALF_CH)
        ys.append(_dot(xre[:, cols].astype(BF16), cre_ref[hf]) + _dot(xim[:, cols].astype(BF16), cim_ref[hf]))
    y = jnp.concatenate(ys, axis=1) + d_ref[...] * u
    z = jax.nn.gelu(y)
    o = z * jax.nn.sigmoid(_dot(z.astype(BF16), wglu_ref[...]))
    for sl in range(S5_SLABS):
        o_tb[sl] = o[:, sl * LANES:(sl + 1) * LANES]
    for b in range(n_seq):
        ob = jnp.concatenate([o_tb[sl, pl.ds(b, tt, stride=n_seq), :] for sl in range(S5_SLABS)], axis=1)
        out_ref[b] = ob.astype(out_ref.dtype)


def _s5_batch_call(p, row0, n_seq, seq_len, h0re, h0im, prm, tt):
    assert n_seq == SUBLANES and seq_len % tt == 0 and row0 % tt == 0
    n_chunk = seq_len // tt
    blk0 = row0 // tt
    a8 = lambda a: jnp.broadcast_to(a[:1], (SUBLANES, S5_CH))
    consts = [prm["bcat"], prm["cre"], prm["cim"], a8(prm["apw_re"]), a8(prm["apw_im"]), prm["d"], prm["wglu"]]
    const_specs = [_const_spec(c.shape, (0,) * c.ndim) for c in consts]
    u_specs = [pl.BlockSpec((tt, S5_WIDTH), functools.partial(lambda b, c: (blk0 + b * n_chunk + c, 0), b))
               for b in range(n_seq)]
    rows = n_seq * tt
    out, hre, him = pl.pallas_call(
        functools.partial(_s5_batch_kernel, n_seq, tt),
        grid=(n_chunk,),
        in_specs=[*u_specs, _const_spec((1, S5_CH), (0, 0)), _const_spec((1, S5_CH), (0, 0)), *const_specs],
        out_specs=[pl.BlockSpec((n_seq, tt, S5_WIDTH), lambda c: (0, c, 0)),
                   pl.BlockSpec((n_seq, S5_CH), lambda c: (0, 0)), pl.BlockSpec((n_seq, S5_CH), lambda c: (0, 0))],
        out_shape=[jax.ShapeDtypeStruct((n_seq, seq_len, S5_WIDTH), BF16),
                   jax.ShapeDtypeStruct((n_seq, S5_CH), F32), jax.ShapeDtypeStruct((n_seq, S5_CH), F32)],
        scratch_shapes=[pltpu.VMEM((S5_SLABS, rows, LANES), F32), pltpu.VMEM((S5_SLABS, rows, LANES), F32),
                        pltpu.VMEM((rows, S5_CH), F32), pltpu.VMEM((rows, S5_CH), F32),
                        pltpu.VMEM((n_seq, S5_CH), F32), pltpu.VMEM((n_seq, S5_CH), F32)],
        compiler_params=_cparams(("arbitrary",)),
        name="s5_batch",
    )(*([p] * n_seq), h0re, h0im, *consts)
    return out.reshape(n_seq * seq_len, S5_WIDTH), hre, him


def _s5_params(a_re, a_im, log_dt, b_re, b_im, c_re, c_im, d_skip, w_glu):
    dt = jnp.exp(log_dt)[:, None]
    mag = jnp.exp(dt * a_re)
    abar_re, abar_im = mag * jnp.cos(dt * a_im), mag * jnp.sin(dt * a_im)
    den = a_re * a_re + a_im * a_im
    num_re = abar_re - 1.0
    f_re = (num_re * a_re + abar_im * a_im) / den
    f_im = (abar_im * a_re - num_re * a_im) / den
    bbar_re = f_re[..., None] * b_re - f_im[..., None] * b_im
    bbar_im = f_re[..., None] * b_im + f_im[..., None] * b_re

    def block_diag_in(w):
        w = w.reshape(2, S5_GROUPS // 2, S5_STATE, S5_GROUP)
        eye = jnp.eye(S5_GROUPS // 2, dtype=F32)
        return jnp.einsum("hgpn,gk->hgnkp", w, eye).reshape(2, S5_HALF_IN, S5_HALF_CH)

    def block_diag_out(w):
        w = w.reshape(2, S5_GROUPS // 2, S5_GROUP, S5_STATE)
        eye = jnp.eye(S5_GROUPS // 2, dtype=F32)
        return jnp.einsum("hgnp,gk->hgpkn", w, eye).reshape(2, S5_HALF_CH, S5_HALF_IN)

    bcat = jnp.concatenate([block_diag_in(bbar_re), block_diag_in(bbar_im)], axis=-1).astype(BF16)
    ar, ai = abar_re.reshape(1, S5_CH), abar_im.reshape(1, S5_CH)
    pows_re, pows_im = [ar], [ai]
    for _ in range(SUBLANES - 1):
        pr, pi = pows_re[-1], pows_im[-1]
        pows_re.append(pr * ar - pi * ai)
        pows_im.append(pr * ai + pi * ar)
    zeros = jnp.zeros((SUBLANES - 3, S5_CH), F32)
    return dict(
        bcat=bcat,
        cre=block_diag_out(c_re).astype(BF16),
        cim=block_diag_out(-c_im).astype(BF16),
        ast_re=jnp.concatenate([pows_re[0], pows_re[1], pows_re[3], zeros], axis=0),
        ast_im=jnp.concatenate([pows_im[0], pows_im[1], pows_im[3], zeros], axis=0),
        apw_re=jnp.concatenate(pows_re, axis=0),
        apw_im=jnp.concatenate(pows_im, axis=0),
        d=d_skip.reshape(1, S5_WIDTH),
        wglu=w_glu.astype(BF16),
    )


def _ret_tables(t):
    gam = np.log(1.0 - 2.0 ** (-5.0 - np.arange(RET_HEADS, dtype=np.float64)))
    tt = np.arange(t, dtype=np.float64)
    dq = np.exp(gam[:, None] * (tt[None, :] + 1.0))
    dq = np.broadcast_to(dq.reshape(RET_HEADS * t, 1), (RET_HEADS * t, RET_DV))
    diff = tt[:, None] - tt[None, :]
    dm = np.where(diff >= 0, np.exp(gam[:, None, None] * np.maximum(diff, 0.0)[None]), 0.0)
    dk = np.exp(gam[:, None] * (t - 1.0 - tt[None, :]))
    dk = np.repeat(dk.T, RET_DK, axis=1)
    ds = np.repeat(np.exp(gam * t), RET_DK)[:, None] * np.ones((1, RET_DV))
    hm = np.repeat(np.eye(RET_HEADS), RET_DK, axis=1)
    f = lambda a: jnp.asarray(np.ascontiguousarray(a), F32)
    return dict(dq=f(dq), dm=f(dm.reshape(RET_HEADS * t, t)), dk=f(dk), ds=f(ds),
                hm=f(np.concatenate([hm, np.zeros((SUBLANES - RET_HEADS, RET_QK))], axis=0)))


def _rope_tables(pos):
    half = RET_DK // 2
    inv_freq = 1.0 / (ROPE_BASE ** (jnp.arange(half, dtype=F32) / half))
    ang = pos.astype(F32)[:, None] * inv_freq[None, :]
    cos, sin = jnp.cos(ang), jnp.sin(ang)
    zero = jnp.zeros_like(sin)
    tile = lambda a, b: jnp.tile(jnp.concatenate([a, b], axis=1), (1, RET_HEADS))
    return tile(cos, cos), tile(-sin, zero), tile(zero, sin)


def _ret_chunk(t, q, k, v, g, s, cos, s_up, s_dn, dq, dm, dk, ds, hm):
    half = RET_DK // 2

    def rope(x):
        return x * cos + pltpu.roll(x, RET_QK - half, 1) * s_up + pltpu.roll(x, half, 1) * s_dn

    qr = rope(q)
    kr = rope(k) * (RET_DK ** -0.5)
    qs = jnp.concatenate([qr * hm[h:h + 1, :] for h in range(RET_HEADS)], axis=0).astype(BF16)
    inter = _dot(qs, s.astype(BF16)) * dq
    prob = (_dot_nt(qs, kr.astype(BF16)) * dm).astype(BF16)
    vb = v.astype(BF16)
    outs = []
    for h in range(RET_HEADS):
        rows = slice(h * t, (h + 1) * t)
        cols = slice(h * RET_DV, (h + 1) * RET_DV)
        o = inter[rows] + _dot(prob[rows], vb[:, cols])
        mu = jnp.mean(o, axis=-1, keepdims=True)
        oc = o - mu
        var = jnp.mean(oc * oc, axis=-1, keepdims=True)
        gh = g[:, cols]
        outs.append(oc * lax.rsqrt(var + EPS) * (gh * jax.nn.sigmoid(gh)))
    kd = kr * dk
    ks = jnp.concatenate([kd * hm[h:h + 1, :] for h in range(RET_HEADS)], axis=0).astype(BF16)
    vs = jnp.concatenate([vb[:, h * RET_DV:(h + 1) * RET_DV] for h in range(RET_HEADS)], axis=0)
    s_new = s * ds + _dot_tn(ks, vs)
    return jnp.concatenate(outs, axis=1), s_new


def _ret_kernel(chained, t, n_sub, q_ref, k_ref, v_ref, g_ref, s0_ref, cos_ref, sup_ref, sdn_ref, dq_ref, dm_ref,
                dk_ref, ds_ref, hm_ref, *rest):
    out_ref, st_ref, *scratch = rest[-3:] if chained else rest[-2:]
    tabs = (dq_ref[...], dm_ref[...], dk_ref[...], ds_ref[...], hm_ref[...])
    if chained:
        (s_scr,) = scratch

        @pl.when(pl.program_id(1) == 0)
        def _():
            s_scr[...] = s0_ref[...]

        o, s_new = _ret_chunk(t, q_ref[...], k_ref[...], v_ref[...], g_ref[...], s_scr[...],
                              cos_ref[...], sup_ref[...], sdn_ref[...], *tabs)
        _store_rows(out_ref, o)
        s_scr[...] = s_new
        st_ref[...] = s_new
    else:
        def one(i, carry):
            rows = pl.ds(pl.multiple_of(i * t, t), t)
            o, s_new = _ret_chunk(t, q_ref[rows, :], k_ref[rows, :], v_ref[rows, :], g_ref[rows, :], s0_ref[i],
                                  cos_ref[...], sup_ref[...], sdn_ref[...], *tabs)
            out_ref[rows, :] = o.astype(out_ref.dtype)
            st_ref[i] = s_new
            return carry

        lax.fori_loop(0, n_sub, one, 0, unroll=8)


def _stacked_state_io(n_sub, sd, layer_i, st_prev, n_in):
    spec = pl.BlockSpec((None, n_sub, *sd), lambda i: (layer_i, i, 0, 0))
    if st_prev is None:
        return spec, [], [], {}
    return spec, [pl.BlockSpec(memory_space=pl.ANY)], [st_prev], {n_in: 1}


def _ret_call(p, row0, n_seq, seq_len, s0, pos0, chained, t, n_sub=1, out_rows=None, layer_i=0, st_prev=None):
    n_rows = n_seq * seq_len
    rows_blk = t if chained else t * n_sub
    blk0 = row0 // rows_blk
    assert row0 % rows_blk == 0
    out_blk = rows_blk if out_rows is None else out_rows
    assert out_rows is None or n_rows == rows_blk
    tabs = _ret_tables(t)
    consts = [tabs[n] for n in ("dq", "dm", "dk", "ds", "hm")]
    const_specs = [_const_spec(c.shape, (0, 0)) for c in consts]
    cos, s_up, s_dn = _rope_tables(pos0 + jnp.arange(seq_len, dtype=jnp.int32))
    sd = RET_QK, RET_DV
    if chained:
        n_chunk = seq_len // t
        grid = (n_seq, n_chunk)
        rmap = lambda cb: (lambda b, c: (blk0 + b * n_chunk + c, cb))
        data_specs = [pl.BlockSpec((t, RET_QK), rmap(2)), pl.BlockSpec((t, RET_QK), rmap(3)),
                      pl.BlockSpec((t, RET_WIDTH), rmap(2)), pl.BlockSpec((t, RET_WIDTH), rmap(3)),
                      _const_spec(sd, (0, 0))]
        rope_specs = [pl.BlockSpec((t, RET_QK), lambda b, c: (c, 0))] * 3
        out_specs = [pl.BlockSpec((out_blk, RET_WIDTH), lambda b, c: (b * n_chunk + c, 0)),
                     pl.BlockSpec((None, *sd), lambda b, c: (b, 0, 0))]
        st_shape = (n_seq, *sd)
        extra_specs, extra_args, aliases = [], [], {}
        scratch = [pltpu.VMEM(sd, F32)]
        sem = ("arbitrary", "arbitrary")
    else:
        assert seq_len == t and n_seq % n_sub == 0
        grid = (n_seq // n_sub,)
        rmap = lambda cb: (lambda i: (blk0 + i, cb))
        st_spec, extra_specs, extra_args, aliases = _stacked_state_io(n_sub, sd, layer_i, st_prev, 13)
        data_specs = [pl.BlockSpec((rows_blk, RET_QK), rmap(2)), pl.BlockSpec((rows_blk, RET_QK), rmap(3)),
                      pl.BlockSpec((rows_blk, RET_WIDTH), rmap(2)), pl.BlockSpec((rows_blk, RET_WIDTH), rmap(3)),
                      st_spec]
        rope_specs = [_const_spec((t, RET_QK), (0, 0))] * 3
        out_specs = [pl.BlockSpec((rows_blk, RET_WIDTH), lambda i: (i, 0)), st_spec]
        st_shape = s0.shape
        scratch = []
        sem = ("arbitrary",)
    in_specs = [*data_specs, *rope_specs, *const_specs, *extra_specs]
    assert not aliases or list(aliases) == [len(in_specs) - 1]
    return pl.pallas_call(
        functools.partial(_ret_kernel, chained, t, n_sub),
        grid=grid,
        in_specs=in_specs,
        out_specs=out_specs,
        out_shape=[jax.ShapeDtypeStruct((n_rows // rows_blk * out_blk, RET_WIDTH), BF16),
                   jax.ShapeDtypeStruct(st_shape, F32)],
        scratch_shapes=scratch,
        input_output_aliases=aliases,
        compiler_params=_cparams(sem),
        name="retention",
    )(p, p, p, p, s0, cos, s_up, s_dn, *consts, *extra_args)


def _gla_tables(t, n_blk):
    cg = t * n_blk
    blk = np.arange(cg) // t
    same = blk[:, None] == blk[None, :]
    cum = (same & (np.arange(cg)[:, None] >= np.arange(cg)[None, :])).astype(np.float32)
    ones = (np.arange(t * GLA_DK)[:, None] // GLA_DK == np.arange(LANES)[None, :]).astype(np.float32)
    hm = np.repeat(np.eye(GLA_HEADS), GLA_DK, axis=1)
    hm = np.concatenate([hm, np.zeros((SUBLANES - GLA_HEADS, GLA_QK))], axis=0).astype(np.float32)
    return jnp.asarray(cum, BF16), jnp.asarray(ones, BF16), jnp.asarray(hm, F32)


LOG2E = 1.4426950408889634


def _gla_kernel(chained, t, n_blk, n_par, *refs):
    data = [refs[5 * s:5 * s + 5] for s in range(n_par)]
    s0_ref, wa_ref, ba_ref, ng_ref, cum_ref, ones_ref, hm_ref = refs[5 * n_par:5 * n_par + 7]
    rest = refs[5 * n_par + 7:]
    out_ref, st_ref, rt_scr, o_scr, *scratch = rest[-5:] if chained else rest[-4:]
    cg = t * n_blk
    nh, dk, dv = GLA_HEADS, GLA_DK, GLA_DV
    par = range(n_par)
    if chained:
        (s_scr,) = scratch

        @pl.when(pl.program_id(1) == 0)
        def _():
            for s in par:
                s_scr[s] = s0_ref[...]

    hm = hm_ref[...]

    def heads(x):
        return jnp.concatenate([x * hm[h:h + 1, :] for h in range(nh)], axis=0).astype(BF16)

    def pad_rows(x):
        return jnp.concatenate([x, jnp.zeros((LANES - x.shape[0], x.shape[1]), x.dtype)], axis=0)

    q, ksc, vb, b, qe, ke, last = [], [], [], [], [], [], []
    for s in par:
        q_ref, k_ref, v_ref, _, lr_ref = data[s]
        q.append(q_ref[...])
        ksc.append(k_ref[...] * (dk ** -0.5))
        vb.append(v_ref[...].astype(BF16))
        la = jax.nn.log_sigmoid(_dot(lr_ref[...].astype(BF16), wa_ref[...]) + ba_ref[...]) / GLA_TAU
        b.append(_dot_exact01(cum_ref[...], la) * LOG2E)
        last3 = b[s].reshape(n_blk, t, GLA_QK)[:, t - 1:t, :]
        bl = jnp.broadcast_to(last3, (n_blk, t, GLA_QK)).reshape(cg, GLA_QK)
        last.append(last3.reshape(n_blk, GLA_QK))
        qe.append(q[s] * jnp.exp2(b[s]))
        ke.append(ksc[s] * jnp.exp2(bl - b[s]))

    row_t = {lo: lo + lax.broadcasted_iota(jnp.int32, (t - lo, GLA_QK), 0) for lo in range(0, t, SUBLANES)}
    for j in range(n_blk):
        rows = slice(j * t, (j + 1) * t)
        for s in par:
            qj, kj, bj = q[s][rows], ksc[s][rows], b[s][rows]
            for i in range(t):
                lo = i // SUBLANES * SUBLANES
                e = jnp.exp2(jnp.where(row_t[lo] >= i, bj[lo:] - bj[i:i + 1, :], NEG_BIG))
                prod = (qj[lo:] * kj[i:i + 1, :]) * e
                for h in range(nh):
                    ph = prod[:, h * dk:(h + 1) * dk]
                    if lo:
                        ph = jnp.concatenate([jnp.zeros((lo, dk), F32), ph], axis=0)
                    rt_scr[s, (j * nh + h) * t:(j * nh + h + 1) * t, i * dk:(i + 1) * dk] = ph.astype(rt_scr.dtype)
    scores = [_dot(rt_scr[s].astype(BF16), ones_ref[...]) for s in par]

    inter, off = [], []
    if chained:
        for s in par:
            pref = [jnp.zeros((1, GLA_QK), F32)]
            for i in range(n_blk):
                pref.append(pref[i] + last[s][i:i + 1, :])
            expand = lambda rs: jnp.concatenate([jnp.broadcast_to(r, (t, GLA_QK)) for r in rs], axis=0)
            qhat = qe[s] * jnp.exp2(expand(pref[:n_blk]))
            khat = ke[s] * jnp.exp2(pref[n_blk] - expand(pref[1:]))
            a_col = jnp.exp2(pad_rows(pref[n_blk])).T[:, 0:1]
            state = s_scr[s]
            inter.append(_dot(heads(qhat), state.astype(BF16)))
            vs = jnp.concatenate([vb[s][:, h * dv:(h + 1) * dv] for h in range(nh)], axis=0)
            state = state * a_col + _dot_tn(heads(khat), vs)
            s_scr[s] = state
            st_ref[s] = state
            per_blk = [None]
            for i in range(1, n_blk):
                parts = [ke[s][j * t:(j + 1) * t] * jnp.exp2(pref[i] - pref[j + 1]) for j in range(i)]
                rhs = pad_rows(jnp.concatenate(parts, axis=0)).astype(BF16)
                per_blk.append(_dot_nt(heads(qe[s][i * t:(i + 1) * t]), rhs))
            off.append(per_blk)
    else:
        a_cols = jnp.exp2(pad_rows(last[0])).T
        for j in range(n_blk):
            rows = slice(j * t, (j + 1) * t)
            state = s0_ref[j]
            res = _dot(heads(qe[0][rows]), state.astype(BF16))
            for h in range(nh):
                o_scr[0, rows, h * dv:(h + 1) * dv] = res[h * t:(h + 1) * t]
            vs = jnp.concatenate([vb[0][rows, h * dv:(h + 1) * dv] for h in range(nh)], axis=0)
            st_ref[j] = state * a_cols[:, j:j + 1] + _dot_tn(heads(ke[0][rows]), vs)
    for s in par:
        r = data[s][3][...]
        gated = []
        for h in range(nh):
            pieces = []
            for j in range(n_blk):
                piece = scores[s][(j * nh + h) * t:(j * nh + h + 1) * t, :]
                piece = pltpu.roll(piece, j * t, 1) if j else piece
                if chained and j:
                    piece = piece + off[s][j][h * t:(h + 1) * t]
                pieces.append(piece)
            pfull = jnp.concatenate(pieces, axis=0) if n_blk > 1 else pieces[0]
            cols = slice(h * dv, (h + 1) * dv)
            o_inter = inter[s][h * cg:(h + 1) * cg] if chained else o_scr[s, :, cols]
            oh = o_inter + _dot(pfull[:, :cg].astype(BF16), vb[s][:, cols])
            on = oh * lax.rsqrt(jnp.mean(oh * oh, axis=-1, keepdims=True) + EPS) * ng_ref[...]
            rh = r[:, cols]
            gated.append(on * (rh * jax.nn.sigmoid(rh)))
        _store_rows(out_ref.at[s] if chained else out_ref, jnp.concatenate(gated, axis=1))


def _gla_call(p, row0, n_seq, seq_len, s0, wa, ba, ng, chained, t, n_blk, n_par=1, out_rows=None, layer_i=0,
              st_prev=None):
    cg = t * n_blk
    blk0 = row0 // cg
    assert row0 % cg == 0 and cg <= LANES
    out_blk = cg if out_rows is None else out_rows
    assert out_rows is None or seq_len == cg
    cum, ones, hm = _gla_tables(t, n_blk)
    consts = [wa, ba, ng, cum, ones, hm]
    const_specs = [_const_spec(c.shape, (0,) * c.ndim) for c in consts]
    sd = GLA_QK, GLA_DV
    lr_col = (2 * GLA_QK + 2 * GLA_V) // LANES
    rt_dtype = BF16 if t % (2 * SUBLANES) == 0 else F32
    if chained:
        assert n_seq % n_par == 0
        n_chunk = seq_len // cg
        grid = (n_seq // n_par, n_chunk)
        rmap = lambda s, cb: (lambda b, c: (blk0 + (b * n_par + s) * n_chunk + c, cb))
        s_spec = _const_spec(sd, (0, 0))
        out_specs = [pl.BlockSpec((n_par, out_blk, GLA_V), lambda b, c: (b, c, 0)),
                     pl.BlockSpec((n_par, *sd), lambda b, c: (b, 0, 0))]
        out_shape = [jax.ShapeDtypeStruct((n_seq, n_chunk * out_blk, GLA_V), BF16),
                     jax.ShapeDtypeStruct((n_seq, *sd), F32)]
        extra_specs, extra_args, aliases = [], [], {}
        scratch = [pltpu.VMEM((n_par, *sd), F32)]
        sem = ("arbitrary", "arbitrary")
    else:
        assert seq_len == t and n_seq % n_blk == 0 and n_par == 1
        grid = (n_seq // n_blk,)
        rmap = lambda s, cb: (lambda i: (blk0 + i, cb))
        s_spec, extra_specs, extra_args, aliases = _stacked_state_io(n_blk, sd, layer_i, st_prev, 12)
        out_specs = [pl.BlockSpec((cg, GLA_V), lambda i: (i, 0)), s_spec]
        out_shape = [jax.ShapeDtypeStruct((n_seq * seq_len, GLA_V), BF16), jax.ShapeDtypeStruct(s0.shape, F32)]
        scratch = []
        sem = ("arbitrary",)
    data_specs = []
    for s in range(n_par):
        data_specs += [pl.BlockSpec((cg, GLA_QK), rmap(s, 0)), pl.BlockSpec((cg, GLA_QK), rmap(s, 1)),
                       pl.BlockSpec((cg, GLA_V), rmap(s, 1)), pl.BlockSpec((cg, GLA_V), rmap(s, 2)),
                       pl.BlockSpec((cg, LANES), rmap(s, lr_col))]
    in_specs = [*data_specs, s_spec, *const_specs, *extra_specs]
    assert not aliases or list(aliases) == [len(in_specs) - 1]
    out, st = pl.pallas_call(
        functools.partial(_gla_kernel, chained, t, n_blk, n_par),
        grid=grid,
        in_specs=in_specs,
        out_specs=out_specs,
        out_shape=out_shape,
        scratch_shapes=[pltpu.VMEM((n_par, n_blk * GLA_HEADS * t, t * GLA_DK), rt_dtype),
                        pltpu.VMEM((n_par, cg, GLA_V), F32), *scratch],
        input_output_aliases=aliases,
        compiler_params=_cparams(sem),
        name="gla",
    )(*([p] * (5 * n_par)), s0, *consts, *extra_args)
    return out.reshape(-1, GLA_V), st


S5_TT_PROMPT = 128
S5_TC_SAMPLE = 256
RET_T_PROMPT = 256
RET_SUB_SAMPLE = 16
GLA_T_PROMPT = 16
GLA_BLK_PROMPT = 8
GLA_BLK_SAMPLE = 8
GLA_PAR_PROMPT = 2


def _mixer_ab(p, i, prm, st_s5_re, st_s5_im, st_ret, s_ret_prev):
    zeros_h = jnp.zeros((1, S5_CH), F32)
    a_m, hre_m, him_m = _s5_call(p, ROW0_META, 1, N_META, zeros_h, zeros_h, prm, True, N_META, out_rows=DENSE_TM)
    a_p, hre_p, him_p = _s5_batch_call(p, 0, BATCH, SEQ, hre_m[0], him_m[0], prm, S5_TT_PROMPT)
    h0re = jnp.repeat(st_s5_re[i].reshape(DEC_BATCH, S5_CH), DEC_SEQ, axis=0)
    h0im = jnp.repeat(st_s5_im[i].reshape(DEC_BATCH, S5_CH), DEC_SEQ, axis=0)
    a_s, hre_s, him_s = _s5_call(p, ROW0_SAMPLE, DEC_BATCH, DEC_SEQ, h0re, h0im, prm, False, S5_TC_SAMPLE)
    last = lambda h: h.reshape(DEC_BATCH, DEC_SEQ, S5_CH)[:, DEC_SEQ - 1].reshape(DEC_BATCH, S5_GROUPS, S5_STATE)

    zeros_s = jnp.zeros((RET_QK, RET_DV), F32)
    b_m, s_m = _ret_call(p, ROW0_META, 1, N_META, zeros_s, 0, True, N_META, out_rows=DENSE_TM)
    b_p, s_p = _ret_call(p, 0, BATCH, SEQ, s_m[0], N_META, True, RET_T_PROMPT)
    b_s, s_ret = _ret_call(p, ROW0_SAMPLE, DEC_BATCH, DEC_SEQ, st_ret, PAST_LEN, False, DEC_SEQ, RET_SUB_SAMPLE,
                           layer_i=i, st_prev=s_ret_prev)
    states = dict(
        p_s5_re=hre_p.reshape(BATCH, S5_GROUPS, S5_STATE), p_s5_im=him_p.reshape(BATCH, S5_GROUPS, S5_STATE),
        p_ret=s_p.reshape(BATCH, RET_HEADS, RET_DK, RET_DV), s_s5_re=last(hre_s), s_s5_im=last(him_s))
    return [(a_p, a_s, a_m), (b_p, b_s, b_m)], states, s_ret


def _mixer_gla(p, i, wa, ba, ng, st_gla, s_gla_prev):
    zeros_s = jnp.zeros((GLA_QK, GLA_DV), F32)
    o_m, s_m = _gla_call(p, ROW0_META, 1, N_META, zeros_s, wa, ba, ng, True, N_META, 1, out_rows=DENSE_TM)
    o_p, s_p = _gla_call(p, 0, BATCH, SEQ, s_m[0], wa, ba, ng, True, GLA_T_PROMPT, GLA_BLK_PROMPT,
                         n_par=GLA_PAR_PROMPT)
    o_s, s_gla = _gla_call(p, ROW0_SAMPLE, DEC_BATCH, DEC_SEQ, st_gla, wa, ba, ng, False, DEC_SEQ, GLA_BLK_SAMPLE,
                           layer_i=i, st_prev=s_gla_prev)
    states = dict(p_gla=s_p.reshape(BATCH, GLA_HEADS, GLA_DK, GLA_DV))
    return [(o_p, o_s, o_m)], states, s_gla


def kernel(x_prompt, x_sample, state_s5_re, state_s5_im, state_ret, state_gla, meta_tokens, norm_ffn1, norm_mix,
           norm_ffn2, norm_final, ffn1_w_gu, ffn1_w_down, ffn2_w_gu, ffn2_w_down, ab_w_in, ab_w_out, s5_a_re,
           s5_a_im, s5_log_dt, s5_b_re, s5_b_im, s5_c_re, s5_c_im, s5_d, s5_w_glu, gla_w_in, gla_w_alpha2,
           gla_b_alpha, gla_norm, gla_w_out):
    ffn1_gu, ffn1_dn, ffn2_gu, ffn2_dn = ffn1_w_gu, ffn1_w_down, ffn2_w_gu, ffn2_w_down
    ab_in, ab_out, gla_in, gla_out = ab_w_in, ab_w_out, gla_w_in, gla_w_out
    gla_wa = jnp.pad(gla_w_alpha2, ((0, 0), (0, LANES - GLA_LOWRANK), (0, 0))).astype(BF16)
    n1 = norm_ffn1.reshape(DEPTH, 1, D_MODEL)
    nm = norm_mix.reshape(DEPTH, 1, D_MODEL)
    n2 = norm_ffn2.reshape(DEPTH, 1, D_MODEL)

    meta_pad = jnp.pad(meta_tokens.astype(x_prompt.dtype), ((0, DENSE_TM - N_META), (0, 0)))
    xs = [x_prompt.reshape(ROWS_PROMPT, D_MODEL), x_sample.reshape(ROWS_SAMPLE, D_MODEL), meta_pad]
    st_ret = state_ret.reshape(N_EVEN, DEC_BATCH, RET_QK, RET_DV)
    st_gla = state_gla.reshape(N_ODD, DEC_BATCH, GLA_QK, GLA_DV)
    s_ret = s_gla = None
    collected = {}
    for layer in range(DEPTH):
        i = layer // 2
        last = layer == DEPTH - 1
        g_final = norm_final.reshape(1, D_MODEL) if last else None
        if layer % 2 == 0:
            x1, p = _pre_call(xs, n1, ffn1_gu, ffn1_dn, nm, ab_in, layer, i, AB_IN)
            prm = _s5_params(s5_a_re[i], s5_a_im[i], s5_log_dt[i], s5_b_re[i], s5_b_im[i], s5_c_re[i],
                             s5_c_im[i], s5_d[i], s5_w_glu[i])
            mixes, states, s_ret = _mixer_ab(p, i, prm, state_s5_re, state_s5_im, st_ret, s_ret)
            x = _post_call(x1, mixes, ab_out, n2, ffn2_gu, ffn2_dn, layer, i, g_final)
        else:
            x1, p = _pre_call(xs, n1, ffn1_gu, ffn1_dn, nm, gla_in, layer, i, GLA_IN_PAD)
            mixes, states, s_gla = _mixer_gla(p, i, gla_wa[i], gla_b_alpha[i].reshape(1, GLA_QK),
                                              gla_norm[i].reshape(1, GLA_DV), st_gla, s_gla)
            x = _post_call(x1, mixes, gla_out, n2, ffn2_gu, ffn2_dn, layer, i, g_final)
        xs = [x]
        for name, val in states.items():
            collected.setdefault(name, []).append(val)
    out = {name: jnp.stack(vals) for name, vals in collected.items()}
    y_prompt, y_rest = x
    return (y_prompt.reshape(BATCH, SEQ, D_MODEL), y_rest[:ROWS_SAMPLE].reshape(DEC_BATCH, DEC_SEQ, D_MODEL),
            out["p_s5_re"], out["p_s5_im"], out["p_ret"], out["p_gla"], out["s_s5_re"], out["s_s5_im"],
            s_ret.reshape(N_EVEN, DEC_BATCH, RET_HEADS, RET_DK, RET_DV),
            s_gla.reshape(N_ODD, DEC_BATCH, GLA_HEADS, GLA_DK, GLA_DV))
```

```python
import functools
import math

import numpy as np
import jax
import jax.numpy as jnp
from jax import lax
from jax.experimental import pallas as pl
from jax.experimental.pallas import tpu as pltpu

F32 = jnp.float32
BF16 = jnp.bfloat16

D_MODEL = 1024
BATCH = 8
SEQ = 2048
DEPTH = 4
DEC_BATCH = 128
DEC_SEQ = 8
PAST_LEN = 16384
N_META = 16
N_EVEN = (DEPTH + 1) // 2
N_ODD = DEPTH // 2
S5_WIDTH = D_MODEL // 2
S5_GROUP = 16
S5_GROUPS = S5_WIDTH // S5_GROUP
S5_STATE = 64
S5_CH = S5_GROUPS * S5_STATE
RET_HEADS = 4
RET_DK = D_MODEL // 16
RET_DV = 2 * RET_DK
RET_QK = RET_HEADS * RET_DK
RET_WIDTH = RET_HEADS * RET_DV
AB_IN = S5_WIDTH + 2 * RET_QK + 2 * RET_WIDTH
AB_OUT = S5_WIDTH + RET_WIDTH
GLA_HEADS = 4
GLA_DK = D_MODEL // (2 * GLA_HEADS)
GLA_DV = D_MODEL // GLA_HEADS
GLA_QK = GLA_HEADS * GLA_DK
GLA_V = GLA_HEADS * GLA_DV
GLA_LOWRANK = 16
GLA_TAU = 16.0
GLA_IN = 2 * GLA_QK + 2 * GLA_V + GLA_LOWRANK
LANES = 128
GLA_IN_PAD = 2 * GLA_QK + 2 * GLA_V + LANES
D_FF = 128 * ((8 * D_MODEL // 3 + 127) // 128)
EPS = 1e-6
ROPE_BASE = 10000.0
NEG_BIG = -1e30

ROWS_PROMPT = BATCH * SEQ
ROWS_SAMPLE = DEC_BATCH * DEC_SEQ
ROW0_SAMPLE = ROWS_PROMPT
ROW0_META = ROWS_PROMPT + ROWS_SAMPLE
ROWS = ROW0_META + N_META

VMEM_LIMIT = 56 * 1024 * 1024
DENSE_TM = 256
TILES_PROMPT = ROWS_PROMPT // DENSE_TM
TILES_SAMPLE = ROWS_SAMPLE // DENSE_TM
N_TILES = TILES_PROMPT + TILES_SAMPLE + 1
ROWS_PAD = N_TILES * DENSE_TM
ROWS_REST = ROWS_PAD - ROWS_PROMPT


def _cparams(sem):
    return pltpu.CompilerParams(dimension_semantics=sem, vmem_limit_bytes=VMEM_LIMIT)


def _dot(a, b):
    return jnp.dot(a, b, preferred_element_type=F32)


def _dot_tn(a, b):
    return lax.dot_general(a, b, (((0,), (0,)), ((), ())), preferred_element_type=F32)


def _dot_nt(a, b):
    return lax.dot_general(a, b, (((1,), (1,)), ((), ())), preferred_element_type=F32)


def _dot_exact01(m_bf, x):
    h1 = x.astype(BF16)
    r1 = x - h1.astype(F32)
    h2 = r1.astype(BF16)
    r2 = r1 - h2.astype(F32)
    h3 = r2.astype(BF16)
    return _dot(m_bf, h1) + _dot(m_bf, h2) + _dot(m_bf, h3)


def _rms(x, g):
    return x * lax.rsqrt(jnp.mean(x * x, axis=-1, keepdims=True) + EPS) * g


def _swiglu_half(x, g, wg, wu, wd):
    h = _rms(x, g).astype(BF16)
    gate = _dot(h, wg)
    up = _dot(h, wu)
    act = (gate * jax.nn.sigmoid(gate) * up).astype(BF16)
    return x + 0.5 * _dot(act, wd)


def _store_rows(out_ref, val):
    rows = val.shape[0]
    out_ref[0:rows, :] = val.astype(out_ref.dtype)
    if out_ref.shape[0] > rows:
        out_ref[rows:, :] = jnp.zeros((out_ref.shape[0] - rows, out_ref.shape[1]), out_ref.dtype)


def _const_spec(shape, index):
    return pl.BlockSpec(shape, lambda *_: index, pipeline_mode=pl.Buffered(1))


N_WCHUNK = 16


def _tile(g):
    return jnp.maximum(g - N_WCHUNK, 0)


def _row_spec(width):
    return pl.BlockSpec((DENSE_TM, width), lambda g: (_tile(g), 0))


def _group_specs(width):
    tm = DENSE_TM
    return [pl.BlockSpec((tm, width), lambda g: (jnp.minimum(_tile(g), TILES_PROMPT - 1), 0)),
            pl.BlockSpec((tm, width), lambda g: (jnp.clip(_tile(g) - TILES_PROMPT, 0, TILES_SAMPLE - 1), 0)),
            pl.BlockSpec((tm, width), lambda g: (0, 0))]


def _pick_group(refs):
    if len(refs) == 1:
        return refs[0][...]
    i = pl.program_id(0) - N_WCHUNK
    return jnp.where(i < TILES_PROMPT, refs[0][...],
                     jnp.where(i < TILES_PROMPT + TILES_SAMPLE, refs[1][...], refs[2][...]))


def _chunk_spec(w, lead):
    _, rows, cols = w.shape
    assert rows % (N_WCHUNK * 2 * SUBLANES) == 0
    return pl.BlockSpec((None, rows // N_WCHUNK, cols), lambda g: (lead, jnp.minimum(g, N_WCHUNK - 1), 0))


def _stage(dst, chunk, g, cols=None):
    rows = chunk.shape[0]
    r = pl.ds(pl.multiple_of(g * rows, rows), rows)
    val = chunk[...] if cols is None else chunk[:, cols]
    width = val.shape[1]
    dst[r, 0:width] = val.astype(BF16)
    if dst.shape[1] > width:
        dst[r, width:] = jnp.zeros((rows, dst.shape[1] - width), BF16)


def _pre_kernel(n_x, *refs):
    x_refs = refs[:n_x]
    g1_ref, wgu_ref, wd_ref, g2_ref, win_ref, x1_ref, p_ref, wg_s, wu_s, wd_s, win_s = refs[n_x:]
    g = pl.program_id(0)

    @pl.when(g < N_WCHUNK)
    def _():
        _stage(wg_s, wgu_ref, g, slice(0, D_FF))
        _stage(wu_s, wgu_ref, g, slice(D_FF, 2 * D_FF))
        _stage(wd_s, wd_ref, g)
        _stage(win_s, win_ref, g)

    @pl.when(g >= N_WCHUNK)
    def _():
        x1 = _swiglu_half(_pick_group(x_refs), g1_ref[...], wg_s[...], wu_s[...], wd_s[...])
        x1_ref[...] = x1
        p_ref[...] = _dot(_rms(x1, g2_ref[...]).astype(BF16), win_s[...])


def _ffn_scratch():
    return [pltpu.VMEM((D_MODEL, D_FF), BF16), pltpu.VMEM((D_MODEL, D_FF), BF16), pltpu.VMEM((D_FF, D_MODEL), BF16)]


def _pre_call(xs, g1, w_gu, w_down, g2, w_in, layer, mix_idx, n_in):
    x_specs = [_row_spec(D_MODEL)] if len(xs) == 1 else _group_specs(D_MODEL)
    return pl.pallas_call(
        functools.partial(_pre_kernel, len(xs)),
        grid=(N_WCHUNK + N_TILES,),
        in_specs=[
            *x_specs,
            _const_spec((None, 1, D_MODEL), (layer, 0, 0)),
            _chunk_spec(w_gu, layer),
            _chunk_spec(w_down, layer),
            _const_spec((None, 1, D_MODEL), (layer, 0, 0)),
            _chunk_spec(w_in, mix_idx),
        ],
        out_specs=[_row_spec(D_MODEL), _row_spec(n_in)],
        out_shape=[jax.ShapeDtypeStruct((ROWS_PAD, D_MODEL), F32), jax.ShapeDtypeStruct((ROWS_PAD, n_in), F32)],
        scratch_shapes=[*_ffn_scratch(), pltpu.VMEM((D_MODEL, n_in), BF16)],
        compiler_params=_cparams(("arbitrary",)),
        name="pre",
    )(*xs, g1, w_gu, w_down, g2, w_in)


def _post_kernel(n_mix, final, *refs):
    x1_ref = refs[0]
    mix_refs = refs[1:1 + 3 * n_mix]
    wout_ref, g_ref, wgu_ref, wd_ref = refs[1 + 3 * n_mix:5 + 3 * n_mix]
    rest = refs[5 + 3 * n_mix:]
    wout_s, wg_s, wu_s, wd_s = rest[-4:]
    g = pl.program_id(0)

    @pl.when(g < N_WCHUNK)
    def _():
        _stage(wout_s, wout_ref, g)
        _stage(wg_s, wgu_ref, g, slice(0, D_FF))
        _stage(wu_s, wgu_ref, g, slice(D_FF, 2 * D_FF))
        _stage(wd_s, wd_ref, g)

    @pl.when(g >= N_WCHUNK)
    def _():
        x2 = x1_ref[...]
        width = AB_OUT // n_mix
        for i in range(n_mix):
            x2 = x2 + _dot(_pick_group(mix_refs[3 * i:3 * i + 3]), wout_s[i * width:(i + 1) * width, :])
        y = _swiglu_half(x2, g_ref[...], wg_s[...], wu_s[...], wd_s[...])
        if final:
            gf_ref, yp_ref, yr_ref = rest[:3]
            y = _rms(y, gf_ref[...])

            @pl.when(g < N_WCHUNK + TILES_PROMPT)
            def _():
                yp_ref[...] = y

            @pl.when(g >= N_WCHUNK + TILES_PROMPT)
            def _():
                yr_ref[...] = y
        else:
            rest[0][...] = y


def _post_call(x1, mixes, w_out, g, w_gu, w_down, layer, mix_idx, g_final):
    tm = DENSE_TM
    final = g_final is not None
    in_specs = [_row_spec(D_MODEL)]
    for triple in mixes:
        in_specs += _group_specs(triple[0].shape[1])
    in_specs += [
        _chunk_spec(w_out, mix_idx),
        _const_spec((None, 1, D_MODEL), (layer, 0, 0)),
        _chunk_spec(w_gu, layer),
        _chunk_spec(w_down, layer),
    ]
    args = [x1, *[m for triple in mixes for m in triple], w_out, g, w_gu, w_down]
    if final:
        in_specs.append(_const_spec((1, D_MODEL), (0, 0)))
        args.append(g_final)
        out_specs = [pl.BlockSpec((tm, D_MODEL), lambda s: (jnp.minimum(_tile(s), TILES_PROMPT - 1), 0)),
                     pl.BlockSpec((tm, D_MODEL), lambda s: (jnp.maximum(_tile(s) - TILES_PROMPT, 0), 0))]
        out_shape = [jax.ShapeDtypeStruct((ROWS_PROMPT, D_MODEL), F32),
                     jax.ShapeDtypeStruct((ROWS_REST, D_MODEL), F32)]
    else:
        out_specs = _row_spec(D_MODEL)
        out_shape = jax.ShapeDtypeStruct((ROWS_PAD, D_MODEL), F32)
    return pl.pallas_call(
        functools.partial(_post_kernel, len(mixes), final),
        grid=(N_WCHUNK + N_TILES,),
        in_specs=in_specs,
        out_specs=out_specs,
        out_shape=out_shape,
        scratch_shapes=[pltpu.VMEM((AB_OUT, D_MODEL), BF16), *_ffn_scratch()],
        compiler_params=_cparams(("arbitrary",)),
        name="post",
    )(*args)


S5_HALF_IN = S5_WIDTH // 2
S5_HALF_CH = S5_CH // 2
SUBLANES = 8


def _s5_kernel(chained, tc, u_ref, h0re_ref, h0im_ref, bcat_ref, cre_ref, cim_ref, ast_re_ref, ast_im_ref,
               apw_re_ref, apw_im_ref, d_ref, wglu_ref, out_ref, hre_out, him_out, xre, xim, *carry):
    nb = tc // SUBLANES
    u = u_ref[...]
    ub = u.astype(BF16)
    for hf in range(2):
        xh = _dot(ub[:, hf * S5_HALF_IN:(hf + 1) * S5_HALF_IN], bcat_ref[hf])
        xre[:, hf * S5_HALF_CH:(hf + 1) * S5_HALF_CH] = xh[:, :S5_HALF_CH]
        xim[:, hf * S5_HALF_CH:(hf + 1) * S5_HALF_CH] = xh[:, S5_HALF_CH:]

    sr = xre[...].reshape(nb, SUBLANES, S5_CH)
    si = xim[...].reshape(nb, SUBLANES, S5_CH)
    rowi = lax.broadcasted_iota(jnp.int32, (nb, SUBLANES, S5_CH), 1)
    for step, d in enumerate((1, 2, 4)):
        ar = ast_re_ref[step:step + 1, :][None]
        ai = ast_im_ref[step:step + 1, :][None]
        pr = pltpu.roll(sr, d, 1)
        pi = pltpu.roll(si, d, 1)
        keep = rowi >= d
        sr, si = (sr + jnp.where(keep, ar * pr - ai * pi, 0.0),
                  si + jnp.where(keep, ar * pi + ai * pr, 0.0))
    apr = apw_re_ref[...]
    api = apw_im_ref[...]
    if chained:
        cre, cim = carry
        xre[...] = sr.reshape(tc, S5_CH)
        xim[...] = si.reshape(tc, S5_CH)

        @pl.when(pl.program_id(1) == 0)
        def _():
            cre[...] = h0re_ref[...]
            cim[...] = h0im_ref[...]

        def group(r, c):
            hr, hi = c
            rows = pl.ds(pl.multiple_of(r * SUBLANES, SUBLANES), SUBLANES)
            nr = xre[rows, :] + apr * hr - api * hi
            ni = xim[rows, :] + apr * hi + api * hr
            xre[rows, :] = nr
            xim[rows, :] = ni
            return nr[SUBLANES - 1:, :], ni[SUBLANES - 1:, :]

        hr, hi = lax.fori_loop(0, nb, group, (cre[...], cim[...]), unroll=min(nb, 4))
        cre[...] = hr
        cim[...] = hi
        hre_out[...] = hr
        him_out[...] = hi
    else:
        h0r = h0re_ref[...].reshape(nb, SUBLANES, S5_CH)
        h0i = h0im_ref[...].reshape(nb, SUBLANES, S5_CH)
        fr = (sr + apr[None] * h0r - api[None] * h0i).reshape(tc, S5_CH)
        fi = (si + apr[None] * h0i + api[None] * h0r).reshape(tc, S5_CH)
        xre[...] = fr
        xim[...] = fi
        hre_out[...] = fr
        him_out[...] = fi

    ys = []
    for hf in range(2):
        cols = slice(hf * S5_HALF_CH, (hf + 1) * S5_HALF_CH)
        ys.append(_dot(xre[:, cols].astype(BF16), cre_ref[hf]) + _dot(xim[:, cols].astype(BF16), cim_ref[hf]))
    y = jnp.concatenate(ys, axis=1) + d_ref[...] * u
    z = jax.nn.gelu(y)
    _store_rows(out_ref, z * jax.nn.sigmoid(_dot(z.astype(BF16), wglu_ref[...])))


def _s5_call(p, row0, n_seq, seq_len, h0re, h0im, prm, chained, tc, out_rows=None):
    n_rows = n_seq * seq_len
    blk0 = row0 // tc
    assert row0 % tc == 0 and n_rows % tc == 0
    out_blk = tc if out_rows is None else out_rows
    assert out_rows is None or n_rows == tc
    consts = [prm["bcat"], prm["cre"], prm["cim"], prm["ast_re"], prm["ast_im"], prm["apw_re"], prm["apw_im"],
              prm["d"], prm["wglu"]]
    const_specs = [_const_spec(c.shape, (0,) * c.ndim) for c in consts]
    scratch = [pltpu.VMEM((tc, S5_CH), F32), pltpu.VMEM((tc, S5_CH), F32)]
    if chained:
        n_chunk = seq_len // tc
        grid = (n_seq, n_chunk)
        u_spec = pl.BlockSpec((tc, S5_WIDTH), lambda b, c: (blk0 + b * n_chunk + c, 0))
        h_specs = [_const_spec((1, S5_CH), (0, 0))] * 2
        out_specs = [pl.BlockSpec((out_blk, S5_WIDTH), lambda b, c: (b * n_chunk + c, 0)),
                     pl.BlockSpec((None, 1, S5_CH), lambda b, c: (b, 0, 0)),
                     pl.BlockSpec((None, 1, S5_CH), lambda b, c: (b, 0, 0))]
        out_shape = [jax.ShapeDtypeStruct((n_rows // tc * out_blk, S5_WIDTH), BF16),
                     jax.ShapeDtypeStruct((n_seq, 1, S5_CH), F32), jax.ShapeDtypeStruct((n_seq, 1, S5_CH), F32)]
        scratch += [pltpu.VMEM((1, S5_CH), F32), pltpu.VMEM((1, S5_CH), F32)]
        sem = ("arbitrary", "arbitrary")
    else:
        assert seq_len == SUBLANES
        grid = (n_rows // tc,)
        u_spec = pl.BlockSpec((tc, S5_WIDTH), lambda i: (blk0 + i, 0))
        h_specs = [pl.BlockSpec((tc, S5_CH), lambda i: (i, 0))] * 2
        out_specs = [pl.BlockSpec((tc, S5_WIDTH), lambda i: (i, 0)),
                     pl.BlockSpec((tc, S5_CH), lambda i: (i, 0)), pl.BlockSpec((tc, S5_CH), lambda i: (i, 0))]
        out_shape = [jax.ShapeDtypeStruct((n_rows, S5_WIDTH), BF16),
                     jax.ShapeDtypeStruct((n_rows, S5_CH), F32), jax.ShapeDtypeStruct((n_rows, S5_CH), F32)]
        sem = ("arbitrary",)
    return pl.pallas_call(
        functools.partial(_s5_kernel, chained, tc),
        grid=grid,
        in_specs=[u_spec, *h_specs, *const_specs],
        out_specs=out_specs,
        out_shape=out_shape,
        scratch_shapes=scratch,
        compiler_params=_cparams(sem),
        name="s5",
    )(p, h0re, h0im, *consts)


S5_SLABS = S5_WIDTH // LANES


def _s5_batch_kernel(n_seq, tt, *refs):
    u_refs = refs[:n_seq]
    (h0re_ref, h0im_ref, bcat_ref, cre_ref, cim_ref, are_ref, aim_ref, d_ref, wglu_ref,
     out_ref, hre_out, him_out, u_tb, o_tb, xre, xim, cre, cim) = refs[n_seq:]
    @pl.when(pl.program_id(0) == 0)
    def _():
        cre[...] = jnp.broadcast_to(h0re_ref[...], (n_seq, S5_CH))
        cim[...] = jnp.broadcast_to(h0im_ref[...], (n_seq, S5_CH))

    for b in range(n_seq):
        ub = u_refs[b][...]
        for sl in range(S5_SLABS):
            u_tb[sl, pl.ds(b, tt, stride=n_seq), :] = ub[:, sl * LANES:(sl + 1) * LANES]
    u = jnp.concatenate([u_tb[sl] for sl in range(S5_SLABS)], axis=1)
    ubf = u.astype(BF16)
    for hf in range(2):
        xh = _dot(ubf[:, hf * S5_HALF_IN:(hf + 1) * S5_HALF_IN], bcat_ref[hf])
        xre[:, hf * S5_HALF_CH:(hf + 1) * S5_HALF_CH] = xh[:, :S5_HALF_CH]
        xim[:, hf * S5_HALF_CH:(hf + 1) * S5_HALF_CH] = xh[:, S5_HALF_CH:]

    for hf in range(2):
        cols = slice(hf * S5_HALF_CH, (hf + 1) * S5_HALF_CH)
        ar = are_ref[:, cols]
        ai = aim_ref[:, cols]

        def step(t, c):
            hr, hi = c
            rw = pl.ds(pl.multiple_of(t * n_seq, n_seq), n_seq)
            nr = xre[rw, cols] + (ar * hr - ai * hi)
            ni = xim[rw, cols] + (ar * hi + ai * hr)
            xre[rw, cols] = nr
            xim[rw, cols] = ni
            return nr, ni

        hr, hi = lax.fori_loop(0, tt, step, (cre[:, cols], cim[:, cols]), unroll=tt)
        cre[:, cols] = hr
        cim[:, cols] = hi
        hre_out[:, cols] = hr
        him_out[:, cols] = hi

    ys = []
    for hf in range(2):
        cols = slice(hf * S5_HALF_CH, (hf + 1) * S5_HALF_CH)
        ys.append(_dot(xre[:, cols].astype(BF16), cre_ref[hf]) + _dot(xim[:, cols].astype(BF16), cim_ref[hf]))
    y = jnp.concatenate(ys, axis=1) + d_ref[...] * u
    z = jax.nn.gelu(y)
    o = z * jax.nn.sigmoid(_dot(z.astype(BF16), wglu_ref[...]))
    for sl in range(S5_SLABS):
        o_tb[sl] = o[:, sl * LANES:(sl + 1) * LANES]
    for b in range(n_seq):
        ob = jnp.concatenate([o_tb[sl, pl.ds(b, tt, stride=n_seq), :] for sl in range(S5_SLABS)], axis=1)
        out_ref[b] = ob.astype(out_ref.dtype)


def _s5_batch_call(p, row0, n_seq, seq_len, h0re, h0im, prm, tt):
    assert n_seq == SUBLANES and seq_len % tt == 0 and row0 % tt == 0
    n_chunk = seq_len // tt
    blk0 = row0 // tt
    a8 = lambda a: jnp.broadcast_to(a[:1], (SUBLANES, S5_CH))
    consts = [prm["bcat"], prm["cre"], prm["cim"], a8(prm["apw_re"]), a8(prm["apw_im"]), prm["d"], prm["wglu"]]
    const_specs = [_const_spec(c.shape, (0,) * c.ndim) for c in consts]
    u_specs = [pl.BlockSpec((tt, S5_WIDTH), functools.partial(lambda b, c: (blk0 + b * n_chunk + c, 0), b))
               for b in range(n_seq)]
    rows = n_seq * tt
    out, hre, him = pl.pallas_call(
        functools.partial(_s5_batch_kernel, n_seq, tt),
        grid=(n_chunk,),
        in_specs=[*u_specs, _const_spec((1, S5_CH), (0, 0)), _const_spec((1, S5_CH), (0, 0)), *const_specs],
        out_specs=[pl.BlockSpec((n_seq, tt, S5_WIDTH), lambda c: (0, c, 0)),
                   pl.BlockSpec((n_seq, S5_CH), lambda c: (0, 0)), pl.BlockSpec((n_seq, S5_CH), lambda c: (0, 0))],
        out_shape=[jax.ShapeDtypeStruct((n_seq, seq_len, S5_WIDTH), BF16),
                   jax.ShapeDtypeStruct((n_seq, S5_CH), F32), jax.ShapeDtypeStruct((n_seq, S5_CH), F32)],
        scratch_shapes=[pltpu.VMEM((S5_SLABS, rows, LANES), F32), pltpu.VMEM((S5_SLABS, rows, LANES), F32),
                        pltpu.VMEM((rows, S5_CH), F32), pltpu.VMEM((rows, S5_CH), F32),
                        pltpu.VMEM((n_seq, S5_CH), F32), pltpu.VMEM((n_seq, S5_CH), F32)],
        compiler_params=_cparams(("arbitrary",)),
        name="s5_batch",
    )(*([p] * n_seq), h0re, h0im, *consts)
    return out.reshape(n_seq * seq_len, S5_WIDTH), hre, him


def _s5_params(a_re, a_im, log_dt, b_re, b_im, c_re, c_im, d_skip, w_glu):
    dt = jnp.exp(log_dt)[:, None]
    mag = jnp.exp(dt * a_re)
    abar_re, abar_im = mag * jnp.cos(dt * a_im), mag * jnp.sin(dt * a_im)
    den = a_re * a_re + a_im * a_im
    num_re = abar_re - 1.0
    f_re = (num_re * a_re + abar_im * a_im) / den
    f_im = (abar_im * a_re - num_re * a_im) / den
    bbar_re = f_re[..., None] * b_re - f_im[..., None] * b_im
    bbar_im = f_re[..., None] * b_im + f_im[..., None] * b_re

    def block_diag_in(w):
        w = w.reshape(2, S5_GROUPS // 2, S5_STATE, S5_GROUP)
        eye = jnp.eye(S5_GROUPS // 2, dtype=F32)
        return jnp.einsum("hgpn,gk->hgnkp", w, eye).reshape(2, S5_HALF_IN, S5_HALF_CH)

    def block_diag_out(w):
        w = w.reshape(2, S5_GROUPS // 2, S5_GROUP, S5_STATE)
        eye = jnp.eye(S5_GROUPS // 2, dtype=F32)
        return jnp.einsum("hgnp,gk->hgpkn", w, eye).reshape(2, S5_HALF_CH, S5_HALF_IN)

    bcat = jnp.concatenate([block_diag_in(bbar_re), block_diag_in(bbar_im)], axis=-1).astype(BF16)
    ar, ai = abar_re.reshape(1, S5_CH), abar_im.reshape(1, S5_CH)
    pows_re, pows_im = [ar], [ai]
    for _ in range(SUBLANES - 1):
        pr, pi = pows_re[-1], pows_im[-1]
        pows_re.append(pr * ar - pi * ai)
        pows_im.append(pr * ai + pi * ar)
    zeros = jnp.zeros((SUBLANES - 3, S5_CH), F32)
    return dict(
        bcat=bcat,
        cre=block_diag_out(c_re).astype(BF16),
        cim=block_diag_out(-c_im).astype(BF16),
        ast_re=jnp.concatenate([pows_re[0], pows_re[1], pows_re[3], zeros], axis=0),
        ast_im=jnp.concatenate([pows_im[0], pows_im[1], pows_im[3], zeros], axis=0),
        apw_re=jnp.concatenate(pows_re, axis=0),
        apw_im=jnp.concatenate(pows_im, axis=0),
        d=d_skip.reshape(1, S5_WIDTH),
        wglu=w_glu.astype(BF16),
    )


def _ret_tables(t):
    gam = np.log(1.0 - 2.0 ** (-5.0 - np.arange(RET_HEADS, dtype=np.float64)))
    tt = np.arange(t, dtype=np.float64)
    dq = np.exp(gam[:, None] * (tt[None, :] + 1.0))
    dq = np.broadcast_to(dq.reshape(RET_HEADS * t, 1), (RET_HEADS * t, RET_DV))
    diff = tt[:, None] - tt[None, :]
    dm = np.where(diff >= 0, np.exp(gam[:, None, None] * np.maximum(diff, 0.0)[None]), 0.0)
    dk = np.exp(gam[:, None] * (t - 1.0 - tt[None, :]))
    dk = np.repeat(dk.T, RET_DK, axis=1)
    ds = np.repeat(np.exp(gam * t), RET_DK)[:, None] * np.ones((1, RET_DV))
    hm = np.repeat(np.eye(RET_HEADS), RET_DK, axis=1)
    f = lambda a: jnp.asarray(np.ascontiguousarray(a), F32)
    return dict(dq=f(dq), dm=f(dm.reshape(RET_HEADS * t, t)), dk=f(dk), ds=f(ds),
                hm=f(np.concatenate([hm, np.zeros((SUBLANES - RET_HEADS, RET_QK))], axis=0)))


def _rope_tables(pos):
    half = RET_DK // 2
    inv_freq = 1.0 / (ROPE_BASE ** (jnp.arange(half, dtype=F32) / half))
    ang = pos.astype(F32)[:, None] * inv_freq[None, :]
    cos, sin = jnp.cos(ang), jnp.sin(ang)
    zero = jnp.zeros_like(sin)
    tile = lambda a, b: jnp.tile(jnp.concatenate([a, b], axis=1), (1, RET_HEADS))
    return tile(cos, cos), tile(-sin, zero), tile(zero, sin)


def _ret_chunk(t, q, k, v, g, s, cos, s_up, s_dn, dq, dm, dk, ds, hm):
    half = RET_DK // 2

    def rope(x):
        return x * cos + pltpu.roll(x, RET_QK - half, 1) * s_up + pltpu.roll(x, half, 1) * s_dn

    qr = rope(q)
    kr = rope(k) * (RET_DK ** -0.5)
    qs = jnp.concatenate([qr * hm[h:h + 1, :] for h in range(RET_HEADS)], axis=0).astype(BF16)
    inter = _dot(qs, s.astype(BF16)) * dq
    prob = (_dot_nt(qs, kr.astype(BF16)) * dm).astype(BF16)
    vb = v.astype(BF16)
    outs = []
    for h in range(RET_HEADS):
        rows = slice(h * t, (h + 1) * t)
        cols = slice(h * RET_DV, (h + 1) * RET_DV)
        o = inter[rows] + _dot(prob[rows], vb[:, cols])
        mu = jnp.mean(o, axis=-1, keepdims=True)
        oc = o - mu
        var = jnp.mean(oc * oc, axis=-1, keepdims=True)
        gh = g[:, cols]
        outs.append(oc * lax.rsqrt(var + EPS) * (gh * jax.nn.sigmoid(gh)))
    kd = kr * dk
    ks = jnp.concatenate([kd * hm[h:h + 1, :] for h in range(RET_HEADS)], axis=0).astype(BF16)
    vs = jnp.concatenate([vb[:, h * RET_DV:(h + 1) * RET_DV] for h in range(RET_HEADS)], axis=0)
    s_new = s * ds + _dot_tn(ks, vs)
    return jnp.concatenate(outs, axis=1), s_new


def _ret_kernel(chained, t, n_sub, q_ref, k_ref, v_ref, g_ref, s0_ref, cos_ref, sup_ref, sdn_ref, dq_ref, dm_ref,
                dk_ref, ds_ref, hm_ref, *rest):
    out_ref, st_ref, *scratch = rest[-3:] if chained else rest[-2:]
    tabs = (dq_ref[...], dm_ref[...], dk_ref[...], ds_ref[...], hm_ref[...])
    if chained:
        (s_scr,) = scratch

        @pl.when(pl.program_id(1) == 0)
        def _():
            s_scr[...] = s0_ref[...]

        o, s_new = _ret_chunk(t, q_ref[...], k_ref[...], v_ref[...], g_ref[...], s_scr[...],
                              cos_ref[...], sup_ref[...], sdn_ref[...], *tabs)
        _store_rows(out_ref, o)
        s_scr[...] = s_new
        st_ref[...] = s_new
    else:
        def one(i, carry):
            rows = pl.ds(pl.multiple_of(i * t, t), t)
            o, s_new = _ret_chunk(t, q_ref[rows, :], k_ref[rows, :], v_ref[rows, :], g_ref[rows, :], s0_ref[i],
                                  cos_ref[...], sup_ref[...], sdn_ref[...], *tabs)
            out_ref[rows, :] = o.astype(out_ref.dtype)
            st_ref[i] = s_new
            return carry

        lax.fori_loop(0, n_sub, one, 0, unroll=8)


def _stacked_state_io(n_sub, sd, layer_i, st_prev, n_in):
    spec = pl.BlockSpec((None, n_sub, *sd), lambda i: (layer_i, i, 0, 0))
    if st_prev is None:
        return spec, [], [], {}
    return spec, [pl.BlockSpec(memory_space=pl.ANY)], [st_prev], {n_in: 1}


def _ret_call(p, row0, n_seq, seq_len, s0, pos0, chained, t, n_sub=1, out_rows=None, layer_i=0, st_prev=None):
    n_rows = n_seq * seq_len
    rows_blk = t if chained else t * n_sub
    blk0 = row0 // rows_blk
    assert row0 % rows_blk == 0
    out_blk = rows_blk if out_rows is None else out_rows
    assert out_rows is None or n_rows == rows_blk
    tabs = _ret_tables(t)
    consts = [tabs[n] for n in ("dq", "dm", "dk", "ds", "hm")]
    const_specs = [_const_spec(c.shape, (0, 0)) for c in consts]
    cos, s_up, s_dn = _rope_tables(pos0 + jnp.arange(seq_len, dtype=jnp.int32))
    sd = RET_QK, RET_DV
    if chained:
        n_chunk = seq_len // t
        grid = (n_seq, n_chunk)
        rmap = lambda cb: (lambda b, c: (blk0 + b * n_chunk + c, cb))
        data_specs = [pl.BlockSpec((t, RET_QK), rmap(2)), pl.BlockSpec((t, RET_QK), rmap(3)),
                      pl.BlockSpec((t, RET_WIDTH), rmap(2)), pl.BlockSpec((t, RET_WIDTH), rmap(3)),
                      _const_spec(sd, (0, 0))]
        rope_specs = [pl.BlockSpec((t, RET_QK), lambda b, c: (c, 0))] * 3
        out_specs = [pl.BlockSpec((out_blk, RET_WIDTH), lambda b, c: (b * n_chunk + c, 0)),
                     pl.BlockSpec((None, *sd), lambda b, c: (b, 0, 0))]
        st_shape = (n_seq, *sd)
        extra_specs, extra_args, aliases = [], [], {}
        scratch = [pltpu.VMEM(sd, F32)]
        sem = ("arbitrary", "arbitrary")
    else:
        assert seq_len == t and n_seq % n_sub == 0
        grid = (n_seq // n_sub,)
        rmap = lambda cb: (lambda i: (blk0 + i, cb))
        st_spec, extra_specs, extra_args, aliases = _stacked_state_io(n_sub, sd, layer_i, st_prev, 13)
        data_specs = [pl.BlockSpec((rows_blk, RET_QK), rmap(2)), pl.BlockSpec((rows_blk, RET_QK), rmap(3)),
                      pl.BlockSpec((rows_blk, RET_WIDTH), rmap(2)), pl.BlockSpec((rows_blk, RET_WIDTH), rmap(3)),
                      st_spec]
        rope_specs = [_const_spec((t, RET_QK), (0, 0))] * 3
        out_specs = [pl.BlockSpec((rows_blk, RET_WIDTH), lambda i: (i, 0)), st_spec]
        st_shape = s0.shape
        scratch = []
        sem = ("arbitrary",)
    in_specs = [*data_specs, *rope_specs, *const_specs, *extra_specs]
    assert not aliases or list(aliases) == [len(in_specs) - 1]
    return pl.pallas_call(
        functools.partial(_ret_kernel, chained, t, n_sub),
        grid=grid,
        in_specs=in_specs,
        out_specs=out_specs,
        out_shape=[jax.ShapeDtypeStruct((n_rows // rows_blk * out_blk, RET_WIDTH), BF16),
                   jax.ShapeDtypeStruct(st_shape, F32)],
        scratch_shapes=scratch,
        input_output_aliases=aliases,
        compiler_params=_cparams(sem),
        name="retention",
    )(p, p, p, p, s0, cos, s_up, s_dn, *consts, *extra_args)


def _gla_tables(t, n_blk):
    cg = t * n_blk
    blk = np.arange(cg) // t
    same = blk[:, None] == blk[None, :]
    cum = (same & (np.arange(cg)[:, None] >= np.arange(cg)[None, :])).astype(np.float32)
    ones = (np.arange(t * GLA_DK)[:, None] // GLA_DK == np.arange(LANES)[None, :]).astype(np.float32)
    return jnp.asarray(cum, BF16), jnp.asarray(ones, BF16)


LOG2E = 1.4426950408889634


def _gla_kernel(chained, t, n_blk, n_par, *refs):
    data = [refs[5 * s:5 * s + 5] for s in range(n_par)]
    s0_ref, wa_ref, ba_ref, ng_ref, cum_ref, ones_ref = refs[5 * n_par:5 * n_par + 6]
    rest = refs[5 * n_par + 6:]
    out_ref, st_ref, rt_scr, o_scr, *scratch = rest[-5:] if chained else rest[-4:]
    cg = t * n_blk
    nh, dk, dv = GLA_HEADS, GLA_DK, GLA_DV
    par = range(n_par)
    if chained:
        (s_scr,) = scratch

        @pl.when(pl.program_id(1) == 0)
        def _():
            for s in par:
                s_scr[s] = s0_ref[...]

    def pad_rows(x):
        return jnp.concatenate([x, jnp.zeros((LANES - x.shape[0], x.shape[1]), x.dtype)], axis=0)

    q, ksc, vb, b, qe, ke, last = [], [], [], [], [], [], []
    for s in par:
        q_ref, k_ref, v_ref, _, lr_ref = data[s]
        q.append(q_ref[...])
        ksc.append(k_ref[...] * (dk ** -0.5))
        vb.append(v_ref[...].astype(BF16))
        la = jax.nn.log_sigmoid(_dot(lr_ref[...].astype(BF16), wa_ref[...]) + ba_ref[...]) / GLA_TAU
        b.append(_dot_exact01(cum_ref[...], la) * LOG2E)
        last3 = b[s].reshape(n_blk, t, GLA_QK)[:, t - 1:t, :]
        bl = jnp.broadcast_to(last3, (n_blk, t, GLA_QK)).reshape(cg, GLA_QK)
        last.append(last3.reshape(n_blk, GLA_QK))
        qe.append(q[s] * jnp.exp2(b[s]))
        ke.append(ksc[s] * jnp.exp2(bl - b[s]))

    row_t = {lo: lo + lax.broadcasted_iota(jnp.int32, (t - lo, GLA_QK), 0) for lo in range(0, t, SUBLANES)}
    for j in range(n_blk):
        rows = slice(j * t, (j + 1) * t)
        for s in par:
            qj, kj, bj = q[s][rows], ksc[s][rows], b[s][rows]
            for i in range(t):
                lo = i // SUBLANES * SUBLANES
                e = jnp.exp2(jnp.where(row_t[lo] >= i, bj[lo:] - bj[i:i + 1, :], NEG_BIG))
                prod = (qj[lo:] * kj[i:i + 1, :]) * e
                for h in range(nh):
                    ph = prod[:, h * dk:(h + 1) * dk]
                    if lo:
                        ph = jnp.concatenate([jnp.zeros((lo, dk), F32), ph], axis=0)
                    rt_scr[s, (j * nh + h) * t:(j * nh + h + 1) * t, i * dk:(i + 1) * dk] = ph.astype(rt_scr.dtype)
    scores = [_dot(rt_scr[s].astype(BF16), ones_ref[...]) for s in par]

    inter, off = [], []
    if chained:
        for s in par:
            pref = [jnp.zeros((1, GLA_QK), F32)]
            for i in range(n_blk):
                pref.append(pref[i] + last[s][i:i + 1, :])
            expand = lambda rs: jnp.concatenate([jnp.broadcast_to(r, (t, GLA_QK)) for r in rs], axis=0)
            qhat = qe[s] * jnp.exp2(expand(pref[:n_blk]))
            khat = ke[s] * jnp.exp2(pref[n_blk] - expand(pref[1:]))
            a_col = jnp.exp2(pad_rows(pref[n_blk])).T[:, 0:1]
            qhat, khat = qhat.astype(BF16), khat.astype(BF16)
            per_head = []
            for h in range(nh):
                hk = slice(h * dk, (h + 1) * dk)
                state = s_scr[s, hk, :]
                per_head.append(_dot(qhat[:, hk], state.astype(BF16)))
                state = state * a_col[hk] + _dot_tn(khat[:, hk], vb[s][:, h * dv:(h + 1) * dv])
                s_scr[s, hk, :] = state
                st_ref[s, hk, :] = state
            inter.append(per_head)
            per_blk = [None]
            for i in range(1, n_blk):
                parts = [ke[s][j * t:(j + 1) * t] * jnp.exp2(pref[i] - pref[j + 1]) for j in range(i)]
                rhs = pad_rows(jnp.concatenate(parts, axis=0)).astype(BF16)
                qi = qe[s][i * t:(i + 1) * t].astype(BF16)
                per_blk.append([_dot_nt(qi[:, h * dk:(h + 1) * dk], rhs[:, h * dk:(h + 1) * dk])
                                for h in range(nh)])
            off.append(per_blk)
    else:
        a_cols = jnp.exp2(pad_rows(last[0])).T
        qb, kb = qe[0].astype(BF16), ke[0].astype(BF16)
        for j in range(n_blk):
            rows = slice(j * t, (j + 1) * t)
            for h in range(nh):
                hk = slice(h * dk, (h + 1) * dk)
                cols = slice(h * dv, (h + 1) * dv)
                state = s0_ref[j, hk, :]
                o_scr[0, rows, cols] = _dot(qb[rows, hk], state.astype(BF16))
                st_ref[j, hk, :] = state * a_cols[hk, j:j + 1] + _dot_tn(kb[rows, hk], vb[0][rows, cols])
    for s in par:
        r = data[s][3][...]
        gated = []
        for h in range(nh):
            pieces = []
            for j in range(n_blk):
                piece = scores[s][(j * nh + h) * t:(j * nh + h + 1) * t, :]
                piece = pltpu.roll(piece, j * t, 1) if j else piece
                if chained and j:
                    piece = piece + off[s][j][h]
                pieces.append(piece)
            pfull = jnp.concatenate(pieces, axis=0) if n_blk > 1 else pieces[0]
            cols = slice(h * dv, (h + 1) * dv)
            o_inter = inter[s][h] if chained else o_scr[s, :, cols]
            oh = o_inter + _dot(pfull[:, :cg].astype(BF16), vb[s][:, cols])
            on = oh * lax.rsqrt(jnp.mean(oh * oh, axis=-1, keepdims=True) + EPS) * ng_ref[...]
            rh = r[:, cols]
            gated.append(on * (rh * jax.nn.sigmoid(rh)))
        _store_rows(out_ref.at[s] if chained else out_ref, jnp.concatenate(gated, axis=1))


def _gla_call(p, row0, n_seq, seq_len, s0, wa, ba, ng, chained, t, n_blk, n_par=1, out_rows=None, layer_i=0,
              st_prev=None):
    cg = t * n_blk
    blk0 = row0 // cg
    assert row0 % cg == 0 and cg <= LANES
    out_blk = cg if out_rows is None else out_rows
    assert out_rows is None or seq_len == cg
    cum, ones = _gla_tables(t, n_blk)
    consts = [wa, ba, ng, cum, ones]
    const_specs = [_const_spec(c.shape, (0,) * c.ndim) for c in consts]
    sd = GLA_QK, GLA_DV
    lr_col = (2 * GLA_QK + 2 * GLA_V) // LANES
    rt_dtype = BF16 if t % (2 * SUBLANES) == 0 else F32
    if chained:
        assert n_seq % n_par == 0
        n_chunk = seq_len // cg
        grid = (n_seq // n_par, n_chunk)
        rmap = lambda s, cb: (lambda b, c: (blk0 + (b * n_par + s) * n_chunk + c, cb))
        s_spec = _const_spec(sd, (0, 0))
        out_specs = [pl.BlockSpec((n_par, out_blk, GLA_V), lambda b, c: (b, c, 0)),
                     pl.BlockSpec((n_par, *sd), lambda b, c: (b, 0, 0))]
        out_shape = [jax.ShapeDtypeStruct((n_seq, n_chunk * out_blk, GLA_V), BF16),
                     jax.ShapeDtypeStruct((n_seq, *sd), F32)]
        extra_specs, extra_args, aliases = [], [], {}
        scratch = [pltpu.VMEM((n_par, *sd), F32)]
        sem = ("arbitrary", "arbitrary")
    else:
        assert seq_len == t and n_seq % n_blk == 0 and n_par == 1
        grid = (n_seq // n_blk,)
        rmap = lambda s, cb: (lambda i: (blk0 + i, cb))
        s_spec, extra_specs, extra_args, aliases = _stacked_state_io(n_blk, sd, layer_i, st_prev, 11)
        out_specs = [pl.BlockSpec((cg, GLA_V), lambda i: (i, 0)), s_spec]
        out_shape = [jax.ShapeDtypeStruct((n_seq * seq_len, GLA_V), BF16), jax.ShapeDtypeStruct(s0.shape, F32)]
        scratch = []
        sem = ("arbitrary",)
    data_specs = []
    for s in range(n_par):
        data_specs += [pl.BlockSpec((cg, GLA_QK), rmap(s, 0)), pl.BlockSpec((cg, GLA_QK), rmap(s, 1)),
                       pl.BlockSpec((cg, GLA_V), rmap(s, 1)), pl.BlockSpec((cg, GLA_V), rmap(s, 2)),
                       pl.BlockSpec((cg, LANES), rmap(s, lr_col))]
    in_specs = [*data_specs, s_spec, *const_specs, *extra_specs]
    assert not aliases or list(aliases) == [len(in_specs) - 1]
    out, st = pl.pallas_call(
        functools.partial(_gla_kernel, chained, t, n_blk, n_par),
        grid=grid,
        in_specs=in_specs,
        out_specs=out_specs,
        out_shape=out_shape,
        scratch_shapes=[pltpu.VMEM((n_par, n_blk * GLA_HEADS * t, t * GLA_DK), rt_dtype),
                        pltpu.VMEM((n_par, cg, GLA_V), F32), *scratch],
        input_output_aliases=aliases,
        compiler_params=_cparams(sem),
        name="gla",
    )(*([p] * (5 * n_par)), s0, *consts, *extra_args)
    return out.reshape(-1, GLA_V), st


S5_TT_PROMPT = 128
S5_TC_SAMPLE = 256
RET_T_PROMPT = 512
RET_SUB_SAMPLE = 16
GLA_T_PROMPT = 16
GLA_BLK_PROMPT = 8
GLA_BLK_SAMPLE = 8
GLA_PAR_PROMPT = 2


def _mixer_ab(p, i, prm, st_s5_re, st_s5_im, st_ret, s_ret_prev):
    zeros_h = jnp.zeros((1, S5_CH), F32)
    a_m, hre_m, him_m = _s5_call(p, ROW0_META, 1, N_META, zeros_h, zeros_h, prm, True, N_META, out_rows=DENSE_TM)
    a_p, hre_p, him_p = _s5_batch_call(p, 0, BATCH, SEQ, hre_m[0], him_m[0], prm, S5_TT_PROMPT)
    h0re = jnp.repeat(st_s5_re[i].reshape(DEC_BATCH, S5_CH), DEC_SEQ, axis=0)
    h0im = jnp.repeat(st_s5_im[i].reshape(DEC_BATCH, S5_CH), DEC_SEQ, axis=0)
    a_s, hre_s, him_s = _s5_call(p, ROW0_SAMPLE, DEC_BATCH, DEC_SEQ, h0re, h0im, prm, False, S5_TC_SAMPLE)
    last = lambda h: h.reshape(DEC_BATCH, DEC_SEQ, S5_CH)[:, DEC_SEQ - 1].reshape(DEC_BATCH, S5_GROUPS, S5_STATE)

    zeros_s = jnp.zeros((RET_QK, RET_DV), F32)
    b_m, s_m = _ret_call(p, ROW0_META, 1, N_META, zeros_s, 0, True, N_META, out_rows=DENSE_TM)
    b_p, s_p = _ret_call(p, 0, BATCH, SEQ, s_m[0], N_META, True, RET_T_PROMPT)
    b_s, s_ret = _ret_call(p, ROW0_SAMPLE, DEC_BATCH, DEC_SEQ, st_ret, PAST_LEN, False, DEC_SEQ, RET_SUB_SAMPLE,
                           layer_i=i, st_prev=s_ret_prev)
    states = dict(
        p_s5_re=hre_p.reshape(BATCH, S5_GROUPS, S5_STATE), p_s5_im=him_p.reshape(BATCH, S5_GROUPS, S5_STATE),
        p_ret=s_p.reshape(BATCH, RET_HEADS, RET_DK, RET_DV), s_s5_re=last(hre_s), s_s5_im=last(him_s))
    return [(a_p, a_s, a_m), (b_p, b_s, b_m)], states, s_ret


def _mixer_gla(p, i, wa, ba, ng, st_gla, s_gla_prev):
    zeros_s = jnp.zeros((GLA_QK, GLA_DV), F32)
    o_m, s_m = _gla_call(p, ROW0_META, 1, N_META, zeros_s, wa, ba, ng, True, N_META, 1, out_rows=DENSE_TM)
    o_p, s_p = _gla_call(p, 0, BATCH, SEQ, s_m[0], wa, ba, ng, True, GLA_T_PROMPT, GLA_BLK_PROMPT,
                         n_par=GLA_PAR_PROMPT)
    o_s, s_gla = _gla_call(p, ROW0_SAMPLE, DEC_BATCH, DEC_SEQ, st_gla, wa, ba, ng, False, DEC_SEQ, GLA_BLK_SAMPLE,
                           layer_i=i, st_prev=s_gla_prev)
    states = dict(p_gla=s_p.reshape(BATCH, GLA_HEADS, GLA_DK, GLA_DV))
    return [(o_p, o_s, o_m)], states, s_gla


def kernel(x_prompt, x_sample, state_s5_re, state_s5_im, state_ret, state_gla, meta_tokens, norm_ffn1, norm_mix,
           norm_ffn2, norm_final, ffn1_w_gu, ffn1_w_down, ffn2_w_gu, ffn2_w_down, ab_w_in, ab_w_out, s5_a_re,
           s5_a_im, s5_log_dt, s5_b_re, s5_b_im, s5_c_re, s5_c_im, s5_d, s5_w_glu, gla_w_in, gla_w_alpha2,
           gla_b_alpha, gla_norm, gla_w_out):
    ffn1_gu, ffn1_dn, ffn2_gu, ffn2_dn = ffn1_w_gu, ffn1_w_down, ffn2_w_gu, ffn2_w_down
    ab_in, ab_out, gla_in, gla_out = ab_w_in, ab_w_out, gla_w_in, gla_w_out
    gla_wa = jnp.pad(gla_w_alpha2, ((0, 0), (0, LANES - GLA_LOWRANK), (0, 0))).astype(BF16)
    n1 = norm_ffn1.reshape(DEPTH, 1, D_MODEL)
    nm = norm_mix.reshape(DEPTH, 1, D_MODEL)
    n2 = norm_ffn2.reshape(DEPTH, 1, D_MODEL)

    meta_pad = jnp.pad(meta_tokens.astype(x_prompt.dtype), ((0, DENSE_TM - N_META), (0, 0)))
    xs = [x_prompt.reshape(ROWS_PROMPT, D_MODEL), x_sample.reshape(ROWS_SAMPLE, D_MODEL), meta_pad]
    st_ret = state_ret.reshape(N_EVEN, DEC_BATCH, RET_QK, RET_DV)
    st_gla = state_gla.reshape(N_ODD, DEC_BATCH, GLA_QK, GLA_DV)
    s_ret = s_gla = None
    collected = {}
    for layer in range(DEPTH):
        i = layer // 2
        last = layer == DEPTH - 1
        g_final = norm_final.reshape(1, D_MODEL) if last else None
        if layer % 2 == 0:
            x1, p = _pre_call(xs, n1, ffn1_gu, ffn1_dn, nm, ab_in, layer, i, AB_IN)
            prm = _s5_params(s5_a_re[i], s5_a_im[i], s5_log_dt[i], s5_b_re[i], s5_b_im[i], s5_c_re[i],
                             s5_c_im[i], s5_d[i], s5_w_glu[i])
            mixes, states, s_ret = _mixer_ab(p, i, prm, state_s5_re, state_s5_im, st_ret, s_ret)
            x = _post_call(x1, mixes, ab_out, n2, ffn2_gu, ffn2_dn, layer, i, g_final)
        else:
            x1, p = _pre_call(xs, n1, ffn1_gu, ffn1_dn, nm, gla_in, layer, i, GLA_IN_PAD)
            mixes, states, s_gla = _mixer_gla(p, i, gla_wa[i], gla_b_alpha[i].reshape(1, GLA_QK),
                                              gla_norm[i].reshape(1, GLA_DV), st_gla, s_gla)
            x = _post_call(x1, mixes, gla_out, n2, ffn2_gu, ffn2_dn, layer, i, g_final)
        xs = [x]
        for name, val in states.items():
            collected.setdefault(name, []).append(val)
    out = {name: jnp.stack(vals) for name, vals in collected.items()}
    y_prompt, y_rest = x
    return (y_prompt.reshape(BATCH, SEQ, D_MODEL), y_rest[:ROWS_SAMPLE].reshape(DEC_BATCH, DEC_SEQ, D_MODEL),
            out["p_s5_re"], out["p_s5_im"], out["p_ret"], out["p_gla"], out["s_s5_re"], out["s_s5_im"],
            s_ret.reshape(N_EVEN, DEC_BATCH, RET_HEADS, RET_DK, RET_DV),
            s_gla.reshape(N_ODD, DEC_BATCH, GLA_HEADS, GLA_DK, GLA_DV))
```

```python
import functools
import math

import numpy as np
import jax
import jax.numpy as jnp
from jax import lax
from jax.experimental import pallas as pl
from jax.experimental.pallas import tpu as pltpu

F32 = jnp.float32
BF16 = jnp.bfloat16

D_MODEL = 1024
BATCH = 8
SEQ = 2048
DEPTH = 4
DEC_BATCH = 128
DEC_SEQ = 8
PAST_LEN = 16384
N_META = 16
N_EVEN = (DEPTH + 1) // 2
N_ODD = DEPTH // 2
S5_WIDTH = D_MODEL // 2
S5_GROUP = 16
S5_GROUPS = S5_WIDTH // S5_GROUP
S5_STATE = 64
S5_CH = S5_GROUPS * S5_STATE
RET_HEADS = 4
RET_DK = D_MODEL // 16
RET_DV = 2 * RET_DK
RET_QK = RET_HEADS * RET_DK
RET_WIDTH = RET_HEADS * RET_DV
AB_IN = S5_WIDTH + 2 * RET_QK + 2 * RET_WIDTH
AB_OUT = S5_WIDTH + RET_WIDTH
GLA_HEADS = 4
GLA_DK = D_MODEL // (2 * GLA_HEADS)
GLA_DV = D_MODEL // GLA_HEADS
GLA_QK = GLA_HEADS * GLA_DK
GLA_V = GLA_HEADS * GLA_DV
GLA_LOWRANK = 16
GLA_TAU = 16.0
GLA_IN = 2 * GLA_QK + 2 * GLA_V + GLA_LOWRANK
LANES = 128
GLA_IN_PAD = 2 * GLA_QK + 2 * GLA_V + LANES
D_FF = 128 * ((8 * D_MODEL // 3 + 127) // 128)
EPS = 1e-6
ROPE_BASE = 10000.0
NEG_BIG = -1e30

ROWS_PROMPT = BATCH * SEQ
ROWS_SAMPLE = DEC_BATCH * DEC_SEQ
ROW0_SAMPLE = ROWS_PROMPT
ROW0_META = ROWS_PROMPT + ROWS_SAMPLE
ROWS = ROW0_META + N_META

VMEM_LIMIT = 56 * 1024 * 1024
DENSE_TM = 256
TILES_PROMPT = ROWS_PROMPT // DENSE_TM
TILES_SAMPLE = ROWS_SAMPLE // DENSE_TM
N_TILES = TILES_PROMPT + TILES_SAMPLE + 1
ROWS_PAD = N_TILES * DENSE_TM
ROWS_REST = ROWS_PAD - ROWS_PROMPT


def _cparams(sem):
    return pltpu.CompilerParams(dimension_semantics=sem, vmem_limit_bytes=VMEM_LIMIT)


def _dot(a, b):
    return jnp.dot(a, b, preferred_element_type=F32)


def _dot_tn(a, b):
    return lax.dot_general(a, b, (((0,), (0,)), ((), ())), preferred_element_type=F32)


def _dot_nt(a, b):
    return lax.dot_general(a, b, (((1,), (1,)), ((), ())), preferred_element_type=F32)


def _dot_exact01(m_bf, x):
    h1 = x.astype(BF16)
    r1 = x - h1.astype(F32)
    h2 = r1.astype(BF16)
    r2 = r1 - h2.astype(F32)
    h3 = r2.astype(BF16)
    return _dot(m_bf, h1) + _dot(m_bf, h2) + _dot(m_bf, h3)


def _rms(x, g):
    return x * lax.rsqrt(jnp.mean(x * x, axis=-1, keepdims=True) + EPS) * g


def _swiglu_half(x, g, wg, wu, wd):
    h = _rms(x, g).astype(BF16)
    gate = _dot(h, wg)
    up = _dot(h, wu)
    act = (gate * jax.nn.sigmoid(gate) * up).astype(BF16)
    return x + 0.5 * _dot(act, wd)


def _store_rows(out_ref, val):
    rows = val.shape[0]
    out_ref[0:rows, :] = val.astype(out_ref.dtype)
    if out_ref.shape[0] > rows:
        out_ref[rows:, :] = jnp.zeros((out_ref.shape[0] - rows, out_ref.shape[1]), out_ref.dtype)


def _const_spec(shape, index):
    return pl.BlockSpec(shape, lambda *_: index, pipeline_mode=pl.Buffered(1))


N_WCHUNK = 16


def _tile(g):
    return jnp.maximum(g - N_WCHUNK, 0)


def _row_spec(width):
    return pl.BlockSpec((DENSE_TM, width), lambda g: (_tile(g), 0))


def _group_specs(width):
    tm = DENSE_TM
    return [pl.BlockSpec((tm, width), lambda g: (jnp.minimum(_tile(g), TILES_PROMPT - 1), 0)),
            pl.BlockSpec((tm, width), lambda g: (jnp.clip(_tile(g) - TILES_PROMPT, 0, TILES_SAMPLE - 1), 0)),
            pl.BlockSpec((tm, width), lambda g: (0, 0))]


def _pick_group(refs):
    if len(refs) == 1:
        return refs[0][...]
    i = pl.program_id(0) - N_WCHUNK
    return jnp.where(i < TILES_PROMPT, refs[0][...],
                     jnp.where(i < TILES_PROMPT + TILES_SAMPLE, refs[1][...], refs[2][...]))


def _chunk_spec(w, lead):
    _, rows, cols = w.shape
    assert rows % (N_WCHUNK * 2 * SUBLANES) == 0
    return pl.BlockSpec((None, rows // N_WCHUNK, cols), lambda g: (lead, jnp.minimum(g, N_WCHUNK - 1), 0))


def _stage(dst, chunk, g, cols=None):
    rows = chunk.shape[0]
    r = pl.ds(pl.multiple_of(g * rows, rows), rows)
    val = chunk[...] if cols is None else chunk[:, cols]
    width = val.shape[1]
    dst[r, 0:width] = val.astype(BF16)
    if dst.shape[1] > width:
        dst[r, width:] = jnp.zeros((rows, dst.shape[1] - width), BF16)


def _pre_kernel(n_x, *refs):
    x_refs = refs[:n_x]
    g1_ref, wgu_ref, wd_ref, g2_ref, win_ref, x1_ref, p_ref, wg_s, wu_s, wd_s, win_s = refs[n_x:]
    g = pl.program_id(0)

    @pl.when(g < N_WCHUNK)
    def _():
        _stage(wg_s, wgu_ref, g, slice(0, D_FF))
        _stage(wu_s, wgu_ref, g, slice(D_FF, 2 * D_FF))
        _stage(wd_s, wd_ref, g)
        _stage(win_s, win_ref, g)

    @pl.when(g >= N_WCHUNK)
    def _():
        x1 = _swiglu_half(_pick_group(x_refs), g1_ref[...], wg_s[...], wu_s[...], wd_s[...])
        x1_ref[...] = x1
        p_ref[...] = _dot(_rms(x1, g2_ref[...]).astype(BF16), win_s[...])


def _ffn_scratch():
    return [pltpu.VMEM((D_MODEL, D_FF), BF16), pltpu.VMEM((D_MODEL, D_FF), BF16), pltpu.VMEM((D_FF, D_MODEL), BF16)]


def _pre_call(xs, g1, w_gu, w_down, g2, w_in, layer, mix_idx, n_in):
    x_specs = [_row_spec(D_MODEL)] if len(xs) == 1 else _group_specs(D_MODEL)
    return pl.pallas_call(
        functools.partial(_pre_kernel, len(xs)),
        grid=(N_WCHUNK + N_TILES,),
        in_specs=[
            *x_specs,
            _const_spec((None, 1, D_MODEL), (layer, 0, 0)),
            _chunk_spec(w_gu, layer),
            _chunk_spec(w_down, layer),
            _const_spec((None, 1, D_MODEL), (layer, 0, 0)),
            _chunk_spec(w_in, mix_idx),
        ],
        out_specs=[_row_spec(D_MODEL), _row_spec(n_in)],
        out_shape=[jax.ShapeDtypeStruct((ROWS_PAD, D_MODEL), F32), jax.ShapeDtypeStruct((ROWS_PAD, n_in), F32)],
        scratch_shapes=[*_ffn_scratch(), pltpu.VMEM((D_MODEL, n_in), BF16)],
        compiler_params=_cparams(("arbitrary",)),
        name="pre",
    )(*xs, g1, w_gu, w_down, g2, w_in)


def _post_kernel(n_mix, final, *refs):
    x1_ref = refs[0]
    mix_refs = refs[1:1 + 3 * n_mix]
    wout_ref, g_ref, wgu_ref, wd_ref = refs[1 + 3 * n_mix:5 + 3 * n_mix]
    rest = refs[5 + 3 * n_mix:]
    wout_s, wg_s, wu_s, wd_s = rest[-4:]
    g = pl.program_id(0)

    @pl.when(g < N_WCHUNK)
    def _():
        _stage(wout_s, wout_ref, g)
        _stage(wg_s, wgu_ref, g, slice(0, D_FF))
        _stage(wu_s, wgu_ref, g, slice(D_FF, 2 * D_FF))
        _stage(wd_s, wd_ref, g)

    @pl.when(g >= N_WCHUNK)
    def _():
        x2 = x1_ref[...]
        width = AB_OUT // n_mix
        for i in range(n_mix):
            x2 = x2 + _dot(_pick_group(mix_refs[3 * i:3 * i + 3]), wout_s[i * width:(i + 1) * width, :])
        y = _swiglu_half(x2, g_ref[...], wg_s[...], wu_s[...], wd_s[...])
        if final:
            gf_ref, yp_ref, yr_ref = rest[:3]
            y = _rms(y, gf_ref[...])

            @pl.when(g < N_WCHUNK + TILES_PROMPT)
            def _():
                yp_ref[...] = y

            @pl.when(g >= N_WCHUNK + TILES_PROMPT)
            def _():
                yr_ref[...] = y
        else:
            rest[0][...] = y


def _post_call(x1, mixes, w_out, g, w_gu, w_down, layer, mix_idx, g_final):
    tm = DENSE_TM
    final = g_final is not None
    in_specs = [_row_spec(D_MODEL)]
    for triple in mixes:
        in_specs += _group_specs(triple[0].shape[1])
    in_specs += [
        _chunk_spec(w_out, mix_idx),
        _const_spec((None, 1, D_MODEL), (layer, 0, 0)),
        _chunk_spec(w_gu, layer),
        _chunk_spec(w_down, layer),
    ]
    args = [x1, *[m for triple in mixes for m in triple], w_out, g, w_gu, w_down]
    if final:
        in_specs.append(_const_spec((1, D_MODEL), (0, 0)))
        args.append(g_final)
        out_specs = [pl.BlockSpec((tm, D_MODEL), lambda s: (jnp.minimum(_tile(s), TILES_PROMPT - 1), 0)),
                     pl.BlockSpec((tm, D_MODEL), lambda s: (jnp.maximum(_tile(s) - TILES_PROMPT, 0), 0))]
        out_shape = [jax.ShapeDtypeStruct((ROWS_PROMPT, D_MODEL), F32),
                     jax.ShapeDtypeStruct((ROWS_REST, D_MODEL), F32)]
    else:
        out_specs = _row_spec(D_MODEL)
        out_shape = jax.ShapeDtypeStruct((ROWS_PAD, D_MODEL), F32)
    return pl.pallas_call(
        functools.partial(_post_kernel, len(mixes), final),
        grid=(N_WCHUNK + N_TILES,),
        in_specs=in_specs,
        out_specs=out_specs,
        out_shape=out_shape,
        scratch_shapes=[pltpu.VMEM((AB_OUT, D_MODEL), BF16), *_ffn_scratch()],
        compiler_params=_cparams(("arbitrary",)),
        name="post",
    )(*args)


S5_HALF_IN = S5_WIDTH // 2
S5_HALF_CH = S5_CH // 2
SUBLANES = 8


def _s5_kernel(chained, tc, u_ref, h0re_ref, h0im_ref, bcat_ref, cre_ref, cim_ref, ast_re_ref, ast_im_ref,
               apw_re_ref, apw_im_ref, d_ref, wglu_ref, out_ref, hre_out, him_out, xre, xim, *carry):
    nb = tc // SUBLANES
    u = u_ref[...]
    ub = u.astype(BF16)
    for hf in range(2):
        xh = _dot(ub[:, hf * S5_HALF_IN:(hf + 1) * S5_HALF_IN], bcat_ref[hf])
        xre[:, hf * S5_HALF_CH:(hf + 1) * S5_HALF_CH] = xh[:, :S5_HALF_CH]
        xim[:, hf * S5_HALF_CH:(hf + 1) * S5_HALF_CH] = xh[:, S5_HALF_CH:]

    sr = xre[...].reshape(nb, SUBLANES, S5_CH)
    si = xim[...].reshape(nb, SUBLANES, S5_CH)
    rowi = lax.broadcasted_iota(jnp.int32, (nb, SUBLANES, S5_CH), 1)
    for step, d in enumerate((1, 2, 4)):
        ar = ast_re_ref[step:step + 1, :][None]
        ai = ast_im_ref[step:step + 1, :][None]
        pr = pltpu.roll(sr, d, 1)
        pi = pltpu.roll(si, d, 1)
        keep = rowi >= d
        sr, si = (sr + jnp.where(keep, ar * pr - ai * pi, 0.0),
                  si + jnp.where(keep, ar * pi + ai * pr, 0.0))
    apr = apw_re_ref[...]
    api = apw_im_ref[...]
    if chained:
        cre, cim = carry
        xre[...] = sr.reshape(tc, S5_CH)
        xim[...] = si.reshape(tc, S5_CH)

        @pl.when(pl.program_id(1) == 0)
        def _():
            cre[...] = h0re_ref[...]
            cim[...] = h0im_ref[...]

        def group(r, c):
            hr, hi = c
            rows = pl.ds(pl.multiple_of(r * SUBLANES, SUBLANES), SUBLANES)
            nr = xre[rows, :] + apr * hr - api * hi
            ni = xim[rows, :] + apr * hi + api * hr
            xre[rows, :] = nr
            xim[rows, :] = ni
            return nr[SUBLANES - 1:, :], ni[SUBLANES - 1:, :]

        hr, hi = lax.fori_loop(0, nb, group, (cre[...], cim[...]), unroll=min(nb, 4))
        cre[...] = hr
        cim[...] = hi
        hre_out[...] = hr
        him_out[...] = hi
    else:
        seq_of_row = lax.broadcasted_iota(jnp.int32, (tc, nb), 0) // SUBLANES
        rep = jnp.where(seq_of_row == lax.broadcasted_iota(jnp.int32, (tc, nb), 1), 1.0, 0.0).astype(BF16)
        last_row = lax.broadcasted_iota(jnp.int32, (nb, tc), 0) * SUBLANES + (SUBLANES - 1)
        pick = jnp.where(last_row == lax.broadcasted_iota(jnp.int32, (nb, tc), 1), 1.0, 0.0).astype(BF16)
        h0r = _dot_exact01(rep, h0re_ref[...]).reshape(nb, SUBLANES, S5_CH)
        h0i = _dot_exact01(rep, h0im_ref[...]).reshape(nb, SUBLANES, S5_CH)
        fr = (sr + apr[None] * h0r - api[None] * h0i).reshape(tc, S5_CH)
        fi = (si + apr[None] * h0i + api[None] * h0r).reshape(tc, S5_CH)
        xre[...] = fr
        xim[...] = fi
        hre_out[...] = _dot_exact01(pick, fr)
        him_out[...] = _dot_exact01(pick, fi)

    ys = []
    for hf in range(2):
        cols = slice(hf * S5_HALF_CH, (hf + 1) * S5_HALF_CH)
        ys.append(_dot(xre[:, cols].astype(BF16), cre_ref[hf]) + _dot(xim[:, cols].astype(BF16), cim_ref[hf]))
    y = jnp.concatenate(ys, axis=1) + d_ref[...] * u
    z = jax.nn.gelu(y)
    _store_rows(out_ref, z * jax.nn.sigmoid(_dot(z.astype(BF16), wglu_ref[...])))


def _s5_call(p, row0, n_seq, seq_len, h0re, h0im, prm, chained, tc, out_rows=None):
    n_rows = n_seq * seq_len
    blk0 = row0 // tc
    assert row0 % tc == 0 and n_rows % tc == 0
    out_blk = tc if out_rows is None else out_rows
    assert out_rows is None or n_rows == tc
    consts = [prm["bcat"], prm["cre"], prm["cim"], prm["ast_re"], prm["ast_im"], prm["apw_re"], prm["apw_im"],
              prm["d"], prm["wglu"]]
    const_specs = [_const_spec(c.shape, (0,) * c.ndim) for c in consts]
    scratch = [pltpu.VMEM((tc, S5_CH), F32), pltpu.VMEM((tc, S5_CH), F32)]
    if chained:
        n_chunk = seq_len // tc
        grid = (n_seq, n_chunk)
        u_spec = pl.BlockSpec((tc, S5_WIDTH), lambda b, c: (blk0 + b * n_chunk + c, 0))
        h_specs = [_const_spec((1, S5_CH), (0, 0))] * 2
        out_specs = [pl.BlockSpec((out_blk, S5_WIDTH), lambda b, c: (b * n_chunk + c, 0)),
                     pl.BlockSpec((None, 1, S5_CH), lambda b, c: (b, 0, 0)),
                     pl.BlockSpec((None, 1, S5_CH), lambda b, c: (b, 0, 0))]
        out_shape = [jax.ShapeDtypeStruct((n_rows // tc * out_blk, S5_WIDTH), BF16),
                     jax.ShapeDtypeStruct((n_seq, 1, S5_CH), F32), jax.ShapeDtypeStruct((n_seq, 1, S5_CH), F32)]
        scratch += [pltpu.VMEM((1, S5_CH), F32), pltpu.VMEM((1, S5_CH), F32)]
        sem = ("arbitrary", "arbitrary")
    else:
        assert seq_len == SUBLANES
        grid = (n_rows // tc,)
        u_spec = pl.BlockSpec((tc, S5_WIDTH), lambda i: (blk0 + i, 0))
        nb = tc // SUBLANES
        h_specs = [pl.BlockSpec((nb, S5_CH), lambda i: (i, 0))] * 2
        out_specs = [pl.BlockSpec((tc, S5_WIDTH), lambda i: (i, 0)),
                     pl.BlockSpec((nb, S5_CH), lambda i: (i, 0)), pl.BlockSpec((nb, S5_CH), lambda i: (i, 0))]
        out_shape = [jax.ShapeDtypeStruct((n_rows, S5_WIDTH), BF16),
                     jax.ShapeDtypeStruct((n_seq, S5_CH), F32), jax.ShapeDtypeStruct((n_seq, S5_CH), F32)]
        sem = ("arbitrary",)
    return pl.pallas_call(
        functools.partial(_s5_kernel, chained, tc),
        grid=grid,
        in_specs=[u_spec, *h_specs, *const_specs],
        out_specs=out_specs,
        out_shape=out_shape,
        scratch_shapes=scratch,
        compiler_params=_cparams(sem),
        name="s5",
    )(p, h0re, h0im, *consts)


S5_SLABS = S5_WIDTH // LANES


def _s5_batch_kernel(n_seq, tt, *refs):
    u_refs = refs[:n_seq]
    (h0re_ref, h0im_ref, bcat_ref, cre_ref, cim_ref, are_ref, aim_ref, d_ref, wglu_ref,
     out_ref, hre_out, him_out, u_tb, o_tb, xre, xim, cre, cim) = refs[n_seq:]
    @pl.when(pl.program_id(0) == 0)
    def _():
        cre[...] = jnp.broadcast_to(h0re_ref[...], (n_seq, S5_CH))
        cim[...] = jnp.broadcast_to(h0im_ref[...], (n_seq, S5_CH))

    for b in range(n_seq):
        ub = u_refs[b][...]
        for sl in range(S5_SLABS):
            u_tb[sl, pl.ds(b, tt, stride=n_seq), :] = ub[:, sl * LANES:(sl + 1) * LANES]
    u = jnp.concatenate([u_tb[sl] for sl in range(S5_SLABS)], axis=1)
    ubf = u.astype(BF16)
    for hf in range(2):
        xh = _dot(ubf[:, hf * S5_HALF_IN:(hf + 1) * S5_HALF_IN], bcat_ref[hf])
        xre[:, hf * S5_HALF_CH:(hf + 1) * S5_HALF_CH] = xh[:, :S5_HALF_CH]
        xim[:, hf * S5_HALF_CH:(hf + 1) * S5_HALF_CH] = xh[:, S5_HALF_CH:]

    for hf in range(2):
        cols = slice(hf * S5_HALF_CH, (hf + 1) * S5_HALF_CH)
        ar = are_ref[:, cols]
        ai = aim_ref[:, cols]

        def step(t, c):
            hr, hi = c
            rw = pl.ds(pl.multiple_of(t * n_seq, n_seq), n_seq)
            nr = xre[rw, cols] + (ar * hr - ai * hi)
            ni = xim[rw, cols] + (ar * hi + ai * hr)
            xre[rw, cols] = nr
            xim[rw, cols] = ni
            return nr, ni

        hr, hi = lax.fori_loop(0, tt, step, (cre[:, cols], cim[:, cols]), unroll=tt)
        cre[:, cols] = hr
        cim[:, cols] = hi
        hre_out[:, cols] = hr
        him_out[:, cols] = hi

    ys = []
    for hf in range(2):
        cols = slice(hf * S5_HALF_CH, (hf + 1) * S5_HALF_CH)
        ys.append(_dot(xre[:, cols].astype(BF16), cre_ref[hf]) + _dot(xim[:, cols].astype(BF16), cim_ref[hf]))
    y = jnp.concatenate(ys, axis=1) + d_ref[...] * u
    z = jax.nn.gelu(y)
    o = z * jax.nn.sigmoid(_dot(z.astype(BF16), wglu_ref[...]))
    for sl in range(S5_SLABS):
        o_tb[sl] = o[:, sl * LANES:(sl + 1) * LANES]
    for b in range(n_seq):
        ob = jnp.concatenate([o_tb[sl, pl.ds(b, tt, stride=n_seq), :] for sl in range(S5_SLABS)], axis=1)
        out_ref[b] = ob.astype(out_ref.dtype)


def _s5_batch_call(p, row0, n_seq, seq_len, h0re, h0im, prm, tt):
    assert n_seq == SUBLANES and seq_len % tt == 0 and row0 % tt == 0
    n_chunk = seq_len // tt
    blk0 = row0 // tt
    a8 = lambda a: jnp.broadcast_to(a[:1], (SUBLANES, S5_CH))
    consts = [prm["bcat"], prm["cre"], prm["cim"], a8(prm["apw_re"]), a8(prm["apw_im"]), prm["d"], prm["wglu"]]
    const_specs = [_const_spec(c.shape, (0,) * c.ndim) for c in consts]
    u_specs = [pl.BlockSpec((tt, S5_WIDTH), functools.partial(lambda b, c: (blk0 + b * n_chunk + c, 0), b))
               for b in range(n_seq)]
    rows = n_seq * tt
    out, hre, him = pl.pallas_call(
        functools.partial(_s5_batch_kernel, n_seq, tt),
        grid=(n_chunk,),
        in_specs=[*u_specs, _const_spec((1, S5_CH), (0, 0)), _const_spec((1, S5_CH), (0, 0)), *const_specs],
        out_specs=[pl.BlockSpec((n_seq, tt, S5_WIDTH), lambda c: (0, c, 0)),
                   pl.BlockSpec((n_seq, S5_CH), lambda c: (0, 0)), pl.BlockSpec((n_seq, S5_CH), lambda c: (0, 0))],
        out_shape=[jax.ShapeDtypeStruct((n_seq, seq_len, S5_WIDTH), BF16),
                   jax.ShapeDtypeStruct((n_seq, S5_CH), F32), jax.ShapeDtypeStruct((n_seq, S5_CH), F32)],
        scratch_shapes=[pltpu.VMEM((S5_SLABS, rows, LANES), F32), pltpu.VMEM((S5_SLABS, rows, LANES), F32),
                        pltpu.VMEM((rows, S5_CH), F32), pltpu.VMEM((rows, S5_CH), F32),
                        pltpu.VMEM((n_seq, S5_CH), F32), pltpu.VMEM((n_seq, S5_CH), F32)],
        compiler_params=_cparams(("arbitrary",)),
        name="s5_batch",
    )(*([p] * n_seq), h0re, h0im, *consts)
    return out.reshape(n_seq * seq_len, S5_WIDTH), hre, him


def _s5_params(a_re, a_im, log_dt, b_re, b_im, c_re, c_im, d_skip, w_glu):
    dt = jnp.exp(log_dt)[:, None]
    mag = jnp.exp(dt * a_re)
    abar_re, abar_im = mag * jnp.cos(dt * a_im), mag * jnp.sin(dt * a_im)
    den = a_re * a_re + a_im * a_im
    num_re = abar_re - 1.0
    f_re = (num_re * a_re + abar_im * a_im) / den
    f_im = (abar_im * a_re - num_re * a_im) / den
    bbar_re = f_re[..., None] * b_re - f_im[..., None] * b_im
    bbar_im = f_re[..., None] * b_im + f_im[..., None] * b_re

    def block_diag_in(w):
        w = w.reshape(2, S5_GROUPS // 2, S5_STATE, S5_GROUP)
        eye = jnp.eye(S5_GROUPS // 2, dtype=F32)
        return jnp.einsum("hgpn,gk->hgnkp", w, eye).reshape(2, S5_HALF_IN, S5_HALF_CH)

    def block_diag_out(w):
        w = w.reshape(2, S5_GROUPS // 2, S5_GROUP, S5_STATE)
        eye = jnp.eye(S5_GROUPS // 2, dtype=F32)
        return jnp.einsum("hgnp,gk->hgpkn", w, eye).reshape(2, S5_HALF_CH, S5_HALF_IN)

    bcat = jnp.concatenate([block_diag_in(bbar_re), block_diag_in(bbar_im)], axis=-1).astype(BF16)
    ar, ai = abar_re.reshape(1, S5_CH), abar_im.reshape(1, S5_CH)
    pows_re, pows_im = [ar], [ai]
    for _ in range(SUBLANES - 1):
        pr, pi = pows_re[-1], pows_im[-1]
        pows_re.append(pr * ar - pi * ai)
        pows_im.append(pr * ai + pi * ar)
    zeros = jnp.zeros((SUBLANES - 3, S5_CH), F32)
    return dict(
        bcat=bcat,
        cre=block_diag_out(c_re).astype(BF16),
        cim=block_diag_out(-c_im).astype(BF16),
        ast_re=jnp.concatenate([pows_re[0], pows_re[1], pows_re[3], zeros], axis=0),
        ast_im=jnp.concatenate([pows_im[0], pows_im[1], pows_im[3], zeros], axis=0),
        apw_re=jnp.concatenate(pows_re, axis=0),
        apw_im=jnp.concatenate(pows_im, axis=0),
        d=d_skip.reshape(1, S5_WIDTH),
        wglu=w_glu.astype(BF16),
    )


def _ret_tables(t):
    gam = np.log(1.0 - 2.0 ** (-5.0 - np.arange(RET_HEADS, dtype=np.float64)))
    tt = np.arange(t, dtype=np.float64)
    dq = np.exp(gam[:, None] * (tt[None, :] + 1.0))
    dq = np.broadcast_to(dq.reshape(RET_HEADS * t, 1), (RET_HEADS * t, RET_DV))
    diff = tt[:, None] - tt[None, :]
    dm = np.where(diff >= 0, np.exp(gam[:, None, None] * np.maximum(diff, 0.0)[None]), 0.0)
    dk = np.exp(gam[:, None] * (t - 1.0 - tt[None, :]))
    dk = np.repeat(dk.T, RET_DK, axis=1)
    ds = np.repeat(np.exp(gam * t), RET_DK)[:, None] * np.ones((1, RET_DV))
    hm = np.repeat(np.eye(RET_HEADS), RET_DK, axis=1)
    f = lambda a: jnp.asarray(np.ascontiguousarray(a), F32)
    return dict(dq=f(dq), dm=f(dm.reshape(RET_HEADS * t, t)), dk=f(dk), ds=f(ds),
                hm=f(np.concatenate([hm, np.zeros((SUBLANES - RET_HEADS, RET_QK))], axis=0)))


def _rope_tables(pos):
    half = RET_DK // 2
    inv_freq = 1.0 / (ROPE_BASE ** (jnp.arange(half, dtype=F32) / half))
    ang = pos.astype(F32)[:, None] * inv_freq[None, :]
    cos, sin = jnp.cos(ang), jnp.sin(ang)
    zero = jnp.zeros_like(sin)
    tile = lambda a, b: jnp.tile(jnp.concatenate([a, b], axis=1), (1, RET_HEADS))
    return tile(cos, cos), tile(-sin, zero), tile(zero, sin)


def _ret_chunk(t, q, k, v, g, s, cos, s_up, s_dn, dq, dm, dk, ds, hm):
    half = RET_DK // 2

    def rope(x):
        return x * cos + pltpu.roll(x, RET_QK - half, 1) * s_up + pltpu.roll(x, half, 1) * s_dn

    qr = rope(q)
    kr = rope(k) * (RET_DK ** -0.5)
    qs = jnp.concatenate([qr * hm[h:h + 1, :] for h in range(RET_HEADS)], axis=0).astype(BF16)
    inter = _dot(qs, s.astype(BF16)) * dq
    prob = (_dot_nt(qs, kr.astype(BF16)) * dm).astype(BF16)
    vb = v.astype(BF16)
    outs = []
    for h in range(RET_HEADS):
        rows = slice(h * t, (h + 1) * t)
        cols = slice(h * RET_DV, (h + 1) * RET_DV)
        o = inter[rows] + _dot(prob[rows], vb[:, cols])
        mu = jnp.mean(o, axis=-1, keepdims=True)
        oc = o - mu
        var = jnp.mean(oc * oc, axis=-1, keepdims=True)
        gh = g[:, cols]
        outs.append(oc * lax.rsqrt(var + EPS) * (gh * jax.nn.sigmoid(gh)))
    kd = kr * dk
    ks = jnp.concatenate([kd * hm[h:h + 1, :] for h in range(RET_HEADS)], axis=0).astype(BF16)
    vs = jnp.concatenate([vb[:, h * RET_DV:(h + 1) * RET_DV] for h in range(RET_HEADS)], axis=0)
    s_new = s * ds + _dot_tn(ks, vs)
    return jnp.concatenate(outs, axis=1), s_new


def _ret_kernel(chained, t, n_sub, q_ref, k_ref, v_ref, g_ref, s0_ref, cos_ref, sup_ref, sdn_ref, dq_ref, dm_ref,
                dk_ref, ds_ref, hm_ref, *rest):
    out_ref, st_ref, *scratch = rest[-3:] if chained else rest[-2:]
    tabs = (dq_ref[...], dm_ref[...], dk_ref[...], ds_ref[...], hm_ref[...])
    if chained:
        (s_scr,) = scratch

        @pl.when(pl.program_id(1) == 0)
        def _():
            s_scr[...] = s0_ref[...]

        o, s_new = _ret_chunk(t, q_ref[...], k_ref[...], v_ref[...], g_ref[...], s_scr[...],
                              cos_ref[...], sup_ref[...], sdn_ref[...], *tabs)
        _store_rows(out_ref, o)
        s_scr[...] = s_new
        st_ref[...] = s_new
    else:
        def one(i, carry):
            rows = pl.ds(pl.multiple_of(i * t, t), t)
            o, s_new = _ret_chunk(t, q_ref[rows, :], k_ref[rows, :], v_ref[rows, :], g_ref[rows, :], s0_ref[i],
                                  cos_ref[...], sup_ref[...], sdn_ref[...], *tabs)
            out_ref[rows, :] = o.astype(out_ref.dtype)
            st_ref[i] = s_new
            return carry

        lax.fori_loop(0, n_sub, one, 0, unroll=8)


def _stacked_state_io(n_sub, sd, layer_i, st_prev, n_in):
    spec = pl.BlockSpec((None, n_sub, *sd), lambda i: (layer_i, i, 0, 0))
    if st_prev is None:
        return spec, [], [], {}
    return spec, [pl.BlockSpec(memory_space=pl.ANY)], [st_prev], {n_in: 1}


def _ret_call(p, row0, n_seq, seq_len, s0, pos0, chained, t, n_sub=1, out_rows=None, layer_i=0, st_prev=None):
    n_rows = n_seq * seq_len
    rows_blk = t if chained else t * n_sub
    blk0 = row0 // rows_blk
    assert row0 % rows_blk == 0
    out_blk = rows_blk if out_rows is None else out_rows
    assert out_rows is None or n_rows == rows_blk
    tabs = _ret_tables(t)
    consts = [tabs[n] for n in ("dq", "dm", "dk", "ds", "hm")]
    const_specs = [_const_spec(c.shape, (0, 0)) for c in consts]
    cos, s_up, s_dn = _rope_tables(pos0 + jnp.arange(seq_len, dtype=jnp.int32))
    sd = RET_QK, RET_DV
    if chained:
        n_chunk = seq_len // t
        grid = (n_seq, n_chunk)
        rmap = lambda cb: (lambda b, c: (blk0 + b * n_chunk + c, cb))
        data_specs = [pl.BlockSpec((t, RET_QK), rmap(2)), pl.BlockSpec((t, RET_QK), rmap(3)),
                      pl.BlockSpec((t, RET_WIDTH), rmap(2)), pl.BlockSpec((t, RET_WIDTH), rmap(3)),
                      _const_spec(sd, (0, 0))]
        rope_specs = [pl.BlockSpec((t, RET_QK), lambda b, c: (c, 0))] * 3
        out_specs = [pl.BlockSpec((out_blk, RET_WIDTH), lambda b, c: (b * n_chunk + c, 0)),
                     pl.BlockSpec((None, *sd), lambda b, c: (b, 0, 0))]
        st_shape = (n_seq, *sd)
        extra_specs, extra_args, aliases = [], [], {}
        scratch = [pltpu.VMEM(sd, F32)]
        sem = ("arbitrary", "arbitrary")
    else:
        assert seq_len == t and n_seq % n_sub == 0
        grid = (n_seq // n_sub,)
        rmap = lambda cb: (lambda i: (blk0 + i, cb))
        st_spec, extra_specs, extra_args, aliases = _stacked_state_io(n_sub, sd, layer_i, st_prev, 13)
        data_specs = [pl.BlockSpec((rows_blk, RET_QK), rmap(2)), pl.BlockSpec((rows_blk, RET_QK), rmap(3)),
                      pl.BlockSpec((rows_blk, RET_WIDTH), rmap(2)), pl.BlockSpec((rows_blk, RET_WIDTH), rmap(3)),
                      st_spec]
        rope_specs = [_const_spec((t, RET_QK), (0, 0))] * 3
        out_specs = [pl.BlockSpec((rows_blk, RET_WIDTH), lambda i: (i, 0)), st_spec]
        st_shape = s0.shape
        scratch = []
        sem = ("arbitrary",)
    in_specs = [*data_specs, *rope_specs, *const_specs, *extra_specs]
    assert not aliases or list(aliases) == [len(in_specs) - 1]
    return pl.pallas_call(
        functools.partial(_ret_kernel, chained, t, n_sub),
        grid=grid,
        in_specs=in_specs,
        out_specs=out_specs,
        out_shape=[jax.ShapeDtypeStruct((n_rows // rows_blk * out_blk, RET_WIDTH), BF16),
                   jax.ShapeDtypeStruct(st_shape, F32)],
        scratch_shapes=scratch,
        input_output_aliases=aliases,
        compiler_params=_cparams(sem),
        name="retention",
    )(p, p, p, p, s0, cos, s_up, s_dn, *consts, *extra_args)


def _gla_tables(t, n_blk):
    cg = t * n_blk
    blk = np.arange(cg) // t
    same = blk[:, None] == blk[None, :]
    cum = (same & (np.arange(cg)[:, None] >= np.arange(cg)[None, :])).astype(np.float32)
    ones = (np.arange(t * GLA_DK)[:, None] // GLA_DK == np.arange(LANES)[None, :]).astype(np.float32)
    return jnp.asarray(cum, BF16), jnp.asarray(ones, BF16)


LOG2E = 1.4426950408889634


def _gla_kernel(chained, t, n_blk, n_par, *refs):
    data = [refs[5 * s:5 * s + 5] for s in range(n_par)]
    s0_ref, wa_ref, ba_ref, ng_ref, cum_ref, ones_ref = refs[5 * n_par:5 * n_par + 6]
    rest = refs[5 * n_par + 6:]
    out_ref, st_ref, rt_scr, o_scr, *scratch = rest[-5:] if chained else rest[-4:]
    cg = t * n_blk
    nh, dk, dv = GLA_HEADS, GLA_DK, GLA_DV
    par = range(n_par)
    if chained:
        (s_scr,) = scratch

        @pl.when(pl.program_id(1) == 0)
        def _():
            for s in par:
                s_scr[s] = s0_ref[...]

    def pad_rows(x):
        return jnp.concatenate([x, jnp.zeros((LANES - x.shape[0], x.shape[1]), x.dtype)], axis=0)

    q, ksc, vb, b, qe, ke, last = [], [], [], [], [], [], []
    for s in par:
        q_ref, k_ref, v_ref, _, lr_ref = data[s]
        q.append(q_ref[...])
        ksc.append(k_ref[...] * (dk ** -0.5))
        vb.append(v_ref[...].astype(BF16))
        la = jax.nn.log_sigmoid(_dot(lr_ref[...].astype(BF16), wa_ref[...]) + ba_ref[...]) / GLA_TAU
        b.append(_dot_exact01(cum_ref[...], la) * LOG2E)
        last3 = b[s].reshape(n_blk, t, GLA_QK)[:, t - 1:t, :]
        bl = jnp.broadcast_to(last3, (n_blk, t, GLA_QK)).reshape(cg, GLA_QK)
        last.append(last3.reshape(n_blk, GLA_QK))
        qe.append(q[s] * jnp.exp2(b[s]))
        ke.append(ksc[s] * jnp.exp2(bl - b[s]))

    row_t = {lo: lo + lax.broadcasted_iota(jnp.int32, (t - lo, GLA_QK), 0) for lo in range(0, t, SUBLANES)}
    for j in range(n_blk):
        rows = slice(j * t, (j + 1) * t)
        for s in par:
            qj, kj, bj = q[s][rows], ksc[s][rows], b[s][rows]
            for i in range(t):
                lo = i // SUBLANES * SUBLANES
                e = jnp.exp2(jnp.where(row_t[lo] >= i, bj[lo:] - bj[i:i + 1, :], NEG_BIG))
                prod = (qj[lo:] * kj[i:i + 1, :]) * e
                for h in range(nh):
                    ph = prod[:, h * dk:(h + 1) * dk]
                    if lo:
                        ph = jnp.concatenate([jnp.zeros((lo, dk), F32), ph], axis=0)
                    rt_scr[s, (j * nh + h) * t:(j * nh + h + 1) * t, i * dk:(i + 1) * dk] = ph.astype(rt_scr.dtype)
    scores = [_dot(rt_scr[s].astype(BF16), ones_ref[...]) for s in par]

    inter, off = [], []
    if chained:
        for s in par:
            pref = [jnp.zeros((1, GLA_QK), F32)]
            for i in range(n_blk):
                pref.append(pref[i] + last[s][i:i + 1, :])
            expand = lambda rs: jnp.concatenate([jnp.broadcast_to(r, (t, GLA_QK)) for r in rs], axis=0)
            qhat = qe[s] * jnp.exp2(expand(pref[:n_blk]))
            khat = ke[s] * jnp.exp2(pref[n_blk] - expand(pref[1:]))
            a_col = jnp.exp2(pad_rows(pref[n_blk])).T[:, 0:1]
            qhat, khat = qhat.astype(BF16), khat.astype(BF16)
            per_head = []
            for h in range(nh):
                hk = slice(h * dk, (h + 1) * dk)
                state = s_scr[s, hk, :]
                per_head.append(_dot(qhat[:, hk], state.astype(BF16)))
                state = state * a_col[hk] + _dot_tn(khat[:, hk], vb[s][:, h * dv:(h + 1) * dv])
                s_scr[s, hk, :] = state
                st_ref[s, hk, :] = state
            inter.append(per_head)
            per_blk = [None]
            for i in range(1, n_blk):
                parts = [ke[s][j * t:(j + 1) * t] * jnp.exp2(pref[i] - pref[j + 1]) for j in range(i)]
                rhs = pad_rows(jnp.concatenate(parts, axis=0)).T.astype(BF16)
                qi = qe[s][i * t:(i + 1) * t].astype(BF16)
                per_blk.append([_dot(qi[:, h * dk:(h + 1) * dk], rhs[h * dk:(h + 1) * dk, :])
                                for h in range(nh)])
            off.append(per_blk)
    else:
        a_cols = jnp.exp2(pad_rows(last[0])).T
        qb, kb = qe[0].astype(BF16), ke[0].astype(BF16)
        for j in range(n_blk):
            rows = slice(j * t, (j + 1) * t)
            for h in range(nh):
                hk = slice(h * dk, (h + 1) * dk)
                cols = slice(h * dv, (h + 1) * dv)
                state = s0_ref[j, hk, :]
                o_scr[0, rows, cols] = _dot(qb[rows, hk], state.astype(BF16))
                st_ref[j, hk, :] = state * a_cols[hk, j:j + 1] + _dot_tn(kb[rows, hk], vb[0][rows, cols])
    for s in par:
        r = data[s][3][...]
        gated = []
        for h in range(nh):
            pieces = []
            for j in range(n_blk):
                piece = scores[s][(j * nh + h) * t:(j * nh + h + 1) * t, :]
                piece = pltpu.roll(piece, j * t, 1) if j else piece
                if chained and j:
                    piece = piece + off[s][j][h]
                pieces.append(piece)
            pfull = jnp.concatenate(pieces, axis=0) if n_blk > 1 else pieces[0]
            cols = slice(h * dv, (h + 1) * dv)
            o_inter = inter[s][h] if chained else o_scr[s, :, cols]
            oh = o_inter + _dot(pfull[:, :cg].astype(BF16), vb[s][:, cols])
            on = oh * lax.rsqrt(jnp.mean(oh * oh, axis=-1, keepdims=True) + EPS) * ng_ref[...]
            rh = r[:, cols]
            gated.append(on * (rh * jax.nn.sigmoid(rh)))
        _store_rows(out_ref.at[s] if chained else out_ref, jnp.concatenate(gated, axis=1))


def _gla_call(p, row0, n_seq, seq_len, s0, wa, ba, ng, chained, t, n_blk, n_par=1, out_rows=None, layer_i=0,
              st_prev=None):
    cg = t * n_blk
    blk0 = row0 // cg
    assert row0 % cg == 0 and cg <= LANES
    out_blk = cg if out_rows is None else out_rows
    assert out_rows is None or seq_len == cg
    cum, ones = _gla_tables(t, n_blk)
    consts = [wa, ba, ng, cum, ones]
    const_specs = [_const_spec(c.shape, (0,) * c.ndim) for c in consts]
    sd = GLA_QK, GLA_DV
    lr_col = (2 * GLA_QK + 2 * GLA_V) // LANES
    rt_dtype = BF16 if t % (2 * SUBLANES) == 0 else F32
    if chained:
        assert n_seq % n_par == 0
        n_chunk = seq_len // cg
        grid = (n_seq // n_par, n_chunk)
        rmap = lambda s, cb: (lambda b, c: (blk0 + (b * n_par + s) * n_chunk + c, cb))
        s_spec = _const_spec(sd, (0, 0))
        out_specs = [pl.BlockSpec((n_par, out_blk, GLA_V), lambda b, c: (b, c, 0)),
                     pl.BlockSpec((n_par, *sd), lambda b, c: (b, 0, 0))]
        out_shape = [jax.ShapeDtypeStruct((n_seq, n_chunk * out_blk, GLA_V), BF16),
                     jax.ShapeDtypeStruct((n_seq, *sd), F32)]
        extra_specs, extra_args, aliases = [], [], {}
        scratch = [pltpu.VMEM((n_par, *sd), F32)]
        sem = ("arbitrary", "arbitrary")
    else:
        assert seq_len == t and n_seq % n_blk == 0 and n_par == 1
        grid = (n_seq // n_blk,)
        rmap = lambda s, cb: (lambda i: (blk0 + i, cb))
        s_spec, extra_specs, extra_args, aliases = _stacked_state_io(n_blk, sd, layer_i, st_prev, 11)
        out_specs = [pl.BlockSpec((cg, GLA_V), lambda i: (i, 0)), s_spec]
        out_shape = [jax.ShapeDtypeStruct((n_seq * seq_len, GLA_V), BF16), jax.ShapeDtypeStruct(s0.shape, F32)]
        scratch = []
        sem = ("arbitrary",)
    data_specs = []
    for s in range(n_par):
        data_specs += [pl.BlockSpec((cg, GLA_QK), rmap(s, 0)), pl.BlockSpec((cg, GLA_QK), rmap(s, 1)),
                       pl.BlockSpec((cg, GLA_V), rmap(s, 1)), pl.BlockSpec((cg, GLA_V), rmap(s, 2)),
                       pl.BlockSpec((cg, LANES), rmap(s, lr_col))]
    in_specs = [*data_specs, s_spec, *const_specs, *extra_specs]
    assert not aliases or list(aliases) == [len(in_specs) - 1]
    out, st = pl.pallas_call(
        functools.partial(_gla_kernel, chained, t, n_blk, n_par),
        grid=grid,
        in_specs=in_specs,
        out_specs=out_specs,
        out_shape=out_shape,
        scratch_shapes=[pltpu.VMEM((n_par, n_blk * GLA_HEADS * t, t * GLA_DK), rt_dtype),
                        pltpu.VMEM((n_par, cg, GLA_V), F32), *scratch],
        input_output_aliases=aliases,
        compiler_params=_cparams(sem),
        name="gla",
    )(*([p] * (5 * n_par)), s0, *consts, *extra_args)
    return out.reshape(-1, GLA_V), st


S5_TT_PROMPT = 128
S5_TC_SAMPLE = 256
RET_T_PROMPT = 512
RET_SUB_SAMPLE = 16
GLA_T_PROMPT = 16
GLA_BLK_PROMPT = 8
GLA_BLK_SAMPLE = 8
GLA_PAR_PROMPT = 2


def _mixer_ab(p, i, prm, st_s5_re, st_s5_im, st_ret, s_ret_prev):
    zeros_h = jnp.zeros((1, S5_CH), F32)
    a_m, hre_m, him_m = _s5_call(p, ROW0_META, 1, N_META, zeros_h, zeros_h, prm, True, N_META, out_rows=DENSE_TM)
    a_p, hre_p, him_p = _s5_batch_call(p, 0, BATCH, SEQ, hre_m[0], him_m[0], prm, S5_TT_PROMPT)
    a_s, hre_s, him_s = _s5_call(p, ROW0_SAMPLE, DEC_BATCH, DEC_SEQ, st_s5_re[i].reshape(DEC_BATCH, S5_CH),
                                 st_s5_im[i].reshape(DEC_BATCH, S5_CH), prm, False, S5_TC_SAMPLE)
    last = lambda h: h.reshape(DEC_BATCH, S5_GROUPS, S5_STATE)

    zeros_s = jnp.zeros((RET_QK, RET_DV), F32)
    b_m, s_m = _ret_call(p, ROW0_META, 1, N_META, zeros_s, 0, True, N_META, out_rows=DENSE_TM)
    b_p, s_p = _ret_call(p, 0, BATCH, SEQ, s_m[0], N_META, True, RET_T_PROMPT)
    b_s, s_ret = _ret_call(p, ROW0_SAMPLE, DEC_BATCH, DEC_SEQ, st_ret, PAST_LEN, False, DEC_SEQ, RET_SUB_SAMPLE,
                           layer_i=i, st_prev=s_ret_prev)
    states = dict(
        p_s5_re=hre_p.reshape(BATCH, S5_GROUPS, S5_STATE), p_s5_im=him_p.reshape(BATCH, S5_GROUPS, S5_STATE),
        p_ret=s_p.reshape(BATCH, RET_HEADS, RET_DK, RET_DV), s_s5_re=last(hre_s), s_s5_im=last(him_s))
    return [(a_p, a_s, a_m), (b_p, b_s, b_m)], states, s_ret


def _mixer_gla(p, i, wa, ba, ng, st_gla, s_gla_prev):
    zeros_s = jnp.zeros((GLA_QK, GLA_DV), F32)
    o_m, s_m = _gla_call(p, ROW0_META, 1, N_META, zeros_s, wa, ba, ng, True, N_META, 1, out_rows=DENSE_TM)
    o_p, s_p = _gla_call(p, 0, BATCH, SEQ, s_m[0], wa, ba, ng, True, GLA_T_PROMPT, GLA_BLK_PROMPT,
                         n_par=GLA_PAR_PROMPT)
    o_s, s_gla = _gla_call(p, ROW0_SAMPLE, DEC_BATCH, DEC_SEQ, st_gla, wa, ba, ng, False, DEC_SEQ, GLA_BLK_SAMPLE,
                           layer_i=i, st_prev=s_gla_prev)
    states = dict(p_gla=s_p.reshape(BATCH, GLA_HEADS, GLA_DK, GLA_DV))
    return [(o_p, o_s, o_m)], states, s_gla


def kernel(x_prompt, x_sample, state_s5_re, state_s5_im, state_ret, state_gla, meta_tokens, norm_ffn1, norm_mix,
           norm_ffn2, norm_final, ffn1_w_gu, ffn1_w_down, ffn2_w_gu, ffn2_w_down, ab_w_in, ab_w_out, s5_a_re,
           s5_a_im, s5_log_dt, s5_b_re, s5_b_im, s5_c_re, s5_c_im, s5_d, s5_w_glu, gla_w_in, gla_w_alpha2,
           gla_b_alpha, gla_norm, gla_w_out):
    ffn1_gu, ffn1_dn, ffn2_gu, ffn2_dn = ffn1_w_gu, ffn1_w_down, ffn2_w_gu, ffn2_w_down
    ab_in, ab_out, gla_in, gla_out = ab_w_in, ab_w_out, gla_w_in, gla_w_out
    gla_wa = jnp.pad(gla_w_alpha2, ((0, 0), (0, LANES - GLA_LOWRANK), (0, 0))).astype(BF16)
    n1 = norm_ffn1.reshape(DEPTH, 1, D_MODEL)
    nm = norm_mix.reshape(DEPTH, 1, D_MODEL)
    n2 = norm_ffn2.reshape(DEPTH, 1, D_MODEL)

    meta_pad = jnp.pad(meta_tokens.astype(x_prompt.dtype), ((0, DENSE_TM - N_META), (0, 0)))
    xs = [x_prompt.reshape(ROWS_PROMPT, D_MODEL), x_sample.reshape(ROWS_SAMPLE, D_MODEL), meta_pad]
    st_ret = state_ret.reshape(N_EVEN, DEC_BATCH, RET_QK, RET_DV)
    st_gla = state_gla.reshape(N_ODD, DEC_BATCH, GLA_QK, GLA_DV)
    s_ret = s_gla = None
    collected = {}
    for layer in range(DEPTH):
        i = layer // 2
        last = layer == DEPTH - 1
        g_final = norm_final.reshape(1, D_MODEL) if last else None
        if layer % 2 == 0:
            x1, p = _pre_call(xs, n1, ffn1_gu, ffn1_dn, nm, ab_in, layer, i, AB_IN)
            prm = _s5_params(s5_a_re[i], s5_a_im[i], s5_log_dt[i], s5_b_re[i], s5_b_im[i], s5_c_re[i],
                             s5_c_im[i], s5_d[i], s5_w_glu[i])
            mixes, states, s_ret = _mixer_ab(p, i, prm, state_s5_re, state_s5_im, st_ret, s_ret)
            x = _post_call(x1, mixes, ab_out, n2, ffn2_gu, ffn2_dn, layer, i, g_final)
        else:
            x1, p = _pre_call(xs, n1, ffn1_gu, ffn1_dn, nm, gla_in, layer, i, GLA_IN_PAD)
            mixes, states, s_gla = _mixer_gla(p, i, gla_wa[i], gla_b_alpha[i].reshape(1, GLA_QK),
                                              gla_norm[i].reshape(1, GLA_DV), st_gla, s_gla)
            x = _post_call(x1, mixes, gla_out, n2, ffn2_gu, ffn2_dn, layer, i, g_final)
        xs = [x]
        for name, val in states.items():
            collected.setdefault(name, []).append(val)
    out = {name: jnp.stack(vals) for name, vals in collected.items()}
    y_prompt, y_rest = x
    return (y_prompt.reshape(BATCH, SEQ, D_MODEL), y_rest[:ROWS_SAMPLE].reshape(DEC_BATCH, DEC_SEQ, D_MODEL),
            out["p_s5_re"], out["p_s5_im"], out["p_ret"], out["p_gla"], out["s_s5_re"], out["s_s5_im"],
            s_ret.reshape(N_EVEN, DEC_BATCH, RET_HEADS, RET_DK, RET_DV),
            s_gla.reshape(N_ODD, DEC_BATCH, GLA_HEADS, GLA_DK, GLA_DV))
```

```python
import functools
import math

import numpy as np
import jax
import jax.numpy as jnp
from jax import lax
from jax.experimental import pallas as pl
from jax.experimental.pallas import tpu as pltpu

F32 = jnp.float32
BF16 = jnp.bfloat16

D_MODEL = 1024
BATCH = 8
SEQ = 2048
DEPTH = 4
DEC_BATCH = 128
DEC_SEQ = 8
PAST_LEN = 16384
N_META = 16
N_EVEN = (DEPTH + 1) // 2
N_ODD = DEPTH // 2
S5_WIDTH = D_MODEL // 2
S5_GROUP = 16
S5_GROUPS = S5_WIDTH // S5_GROUP
S5_STATE = 64
S5_CH = S5_GROUPS * S5_STATE
RET_HEADS = 4
RET_DK = D_MODEL // 16
RET_DV = 2 * RET_DK
RET_QK = RET_HEADS * RET_DK
RET_WIDTH = RET_HEADS * RET_DV
AB_IN = S5_WIDTH + 2 * RET_QK + 2 * RET_WIDTH
AB_OUT = S5_WIDTH + RET_WIDTH
GLA_HEADS = 4
GLA_DK = D_MODEL // (2 * GLA_HEADS)
GLA_DV = D_MODEL // GLA_HEADS
GLA_QK = GLA_HEADS * GLA_DK
GLA_V = GLA_HEADS * GLA_DV
GLA_LOWRANK = 16
GLA_TAU = 16.0
GLA_IN = 2 * GLA_QK + 2 * GLA_V + GLA_LOWRANK
LANES = 128
GLA_IN_PAD = 2 * GLA_QK + 2 * GLA_V + LANES
D_FF = 128 * ((8 * D_MODEL // 3 + 127) // 128)
EPS = 1e-6
ROPE_BASE = 10000.0
NEG_BIG = -1e30

ROWS_PROMPT = BATCH * SEQ
ROWS_SAMPLE = DEC_BATCH * DEC_SEQ
ROW0_SAMPLE = ROWS_PROMPT
ROW0_META = ROWS_PROMPT + ROWS_SAMPLE
ROWS = ROW0_META + N_META

VMEM_LIMIT = 56 * 1024 * 1024
DENSE_TM = 256
TILES_PROMPT = ROWS_PROMPT // DENSE_TM
TILES_SAMPLE = ROWS_SAMPLE // DENSE_TM
N_TILES = TILES_PROMPT + TILES_SAMPLE + 1
ROWS_PAD = N_TILES * DENSE_TM
ROWS_REST = ROWS_PAD - ROWS_PROMPT


def _cparams(sem):
    return pltpu.CompilerParams(dimension_semantics=sem, vmem_limit_bytes=VMEM_LIMIT)


def _dot(a, b):
    return jnp.dot(a, b, preferred_element_type=F32)


def _dot_tn(a, b):
    return lax.dot_general(a, b, (((0,), (0,)), ((), ())), preferred_element_type=F32)


def _dot_nt(a, b):
    return lax.dot_general(a, b, (((1,), (1,)), ((), ())), preferred_element_type=F32)


def _dot_exact01(m_bf, x):
    h1 = x.astype(BF16)
    r1 = x - h1.astype(F32)
    h2 = r1.astype(BF16)
    r2 = r1 - h2.astype(F32)
    h3 = r2.astype(BF16)
    return _dot(m_bf, h1) + _dot(m_bf, h2) + _dot(m_bf, h3)


def _rms(x, g):
    return x * lax.rsqrt(jnp.mean(x * x, axis=-1, keepdims=True) + EPS) * g


def _swiglu_half(x, g, wg, wu, wd):
    h = _rms(x, g).astype(BF16)
    gate = _dot(h, wg)
    up = _dot(h, wu)
    act = (gate * jax.nn.sigmoid(gate) * up).astype(BF16)
    return x + 0.5 * _dot(act, wd)


def _store_rows(out_ref, val):
    rows = val.shape[0]
    out_ref[0:rows, :] = val.astype(out_ref.dtype)
    if out_ref.shape[0] > rows:
        out_ref[rows:, :] = jnp.zeros((out_ref.shape[0] - rows, out_ref.shape[1]), out_ref.dtype)


def _const_spec(shape, index):
    return pl.BlockSpec(shape, lambda *_: index, pipeline_mode=pl.Buffered(1))


N_WCHUNK = 8


def _tile(g):
    return jnp.maximum(g - N_WCHUNK, 0)


def _row_spec(width):
    return pl.BlockSpec((DENSE_TM, width), lambda g: (_tile(g), 0))


def _group_specs(width):
    tm = DENSE_TM
    return [pl.BlockSpec((tm, width), lambda g: (jnp.minimum(_tile(g), TILES_PROMPT - 1), 0)),
            pl.BlockSpec((tm, width), lambda g: (jnp.clip(_tile(g) - TILES_PROMPT, 0, TILES_SAMPLE - 1), 0)),
            pl.BlockSpec((tm, width), lambda g: (0, 0))]


def _pick_group(refs):
    if len(refs) == 1:
        return refs[0][...]
    i = pl.program_id(0) - N_WCHUNK
    return jnp.where(i < TILES_PROMPT, refs[0][...],
                     jnp.where(i < TILES_PROMPT + TILES_SAMPLE, refs[1][...], refs[2][...]))


def _chunk_spec(w, lead):
    _, rows, cols = w.shape
    assert rows % (N_WCHUNK * 2 * SUBLANES) == 0
    return pl.BlockSpec((None, rows // N_WCHUNK, cols), lambda g: (lead, jnp.minimum(g, N_WCHUNK - 1), 0))


def _stage(dst, chunk, g, cols=None):
    rows = chunk.shape[0]
    r = pl.ds(pl.multiple_of(g * rows, rows), rows)
    val = chunk[...] if cols is None else chunk[:, cols]
    width = val.shape[1]
    dst[r, 0:width] = val.astype(BF16)
    if dst.shape[1] > width:
        dst[r, width:] = jnp.zeros((rows, dst.shape[1] - width), BF16)


def _pre_kernel(n_x, *refs):
    x_refs = refs[:n_x]
    g1_ref, wgu_ref, wd_ref, g2_ref, win_ref, x1_ref, p_ref, wg_s, wu_s, wd_s, win_s = refs[n_x:]
    g = pl.program_id(0)

    @pl.when(g < N_WCHUNK)
    def _():
        _stage(wg_s, wgu_ref, g, slice(0, D_FF))
        _stage(wu_s, wgu_ref, g, slice(D_FF, 2 * D_FF))
        _stage(wd_s, wd_ref, g)
        _stage(win_s, win_ref, g)

    @pl.when(g >= N_WCHUNK)
    def _():
        x1 = _swiglu_half(_pick_group(x_refs), g1_ref[...], wg_s[...], wu_s[...], wd_s[...])
        x1_ref[...] = x1
        p_ref[...] = _dot(_rms(x1, g2_ref[...]).astype(BF16), win_s[...])


def _ffn_scratch():
    return [pltpu.VMEM((D_MODEL, D_FF), BF16), pltpu.VMEM((D_MODEL, D_FF), BF16), pltpu.VMEM((D_FF, D_MODEL), BF16)]


def _pre_call(xs, g1, w_gu, w_down, g2, w_in, layer, mix_idx, n_in):
    x_specs = [_row_spec(D_MODEL)] if len(xs) == 1 else _group_specs(D_MODEL)
    return pl.pallas_call(
        functools.partial(_pre_kernel, len(xs)),
        grid=(N_WCHUNK + N_TILES,),
        in_specs=[
            *x_specs,
            _const_spec((None, 1, D_MODEL), (layer, 0, 0)),
            _chunk_spec(w_gu, layer),
            _chunk_spec(w_down, layer),
            _const_spec((None, 1, D_MODEL), (layer, 0, 0)),
            _chunk_spec(w_in, mix_idx),
        ],
        out_specs=[_row_spec(D_MODEL), _row_spec(n_in)],
        out_shape=[jax.ShapeDtypeStruct((ROWS_PAD, D_MODEL), F32), jax.ShapeDtypeStruct((ROWS_PAD, n_in), F32)],
        scratch_shapes=[*_ffn_scratch(), pltpu.VMEM((D_MODEL, n_in), BF16)],
        compiler_params=_cparams(("arbitrary",)),
        name="pre",
    )(*xs, g1, w_gu, w_down, g2, w_in)


def _post_kernel(n_mix, final, *refs):
    x1_ref = refs[0]
    mix_refs = refs[1:1 + 3 * n_mix]
    wout_ref, g_ref, wgu_ref, wd_ref = refs[1 + 3 * n_mix:5 + 3 * n_mix]
    rest = refs[5 + 3 * n_mix:]
    wout_s, wg_s, wu_s, wd_s = rest[-4:]
    g = pl.program_id(0)

    @pl.when(g < N_WCHUNK)
    def _():
        _stage(wout_s, wout_ref, g)
        _stage(wg_s, wgu_ref, g, slice(0, D_FF))
        _stage(wu_s, wgu_ref, g, slice(D_FF, 2 * D_FF))
        _stage(wd_s, wd_ref, g)

    @pl.when(g >= N_WCHUNK)
    def _():
        x2 = x1_ref[...]
        width = AB_OUT // n_mix
        for i in range(n_mix):
            x2 = x2 + _dot(_pick_group(mix_refs[3 * i:3 * i + 3]), wout_s[i * width:(i + 1) * width, :])
        y = _swiglu_half(x2, g_ref[...], wg_s[...], wu_s[...], wd_s[...])
        if final:
            gf_ref, yp_ref, yr_ref = rest[:3]
            y = _rms(y, gf_ref[...])

            @pl.when(g < N_WCHUNK + TILES_PROMPT)
            def _():
                yp_ref[...] = y

            @pl.when(g >= N_WCHUNK + TILES_PROMPT)
            def _():
                yr_ref[...] = y
        else:
            rest[0][...] = y


def _post_call(x1, mixes, w_out, g, w_gu, w_down, layer, mix_idx, g_final):
    tm = DENSE_TM
    final = g_final is not None
    in_specs = [_row_spec(D_MODEL)]
    for triple in mixes:
        in_specs += _group_specs(triple[0].shape[1])
    in_specs += [
        _chunk_spec(w_out, mix_idx),
        _const_spec((None, 1, D_MODEL), (layer, 0, 0)),
        _chunk_spec(w_gu, layer),
        _chunk_spec(w_down, layer),
    ]
    args = [x1, *[m for triple in mixes for m in triple], w_out, g, w_gu, w_down]
    if final:
        in_specs.append(_const_spec((1, D_MODEL), (0, 0)))
        args.append(g_final)
        out_specs = [pl.BlockSpec((tm, D_MODEL), lambda s: (jnp.minimum(_tile(s), TILES_PROMPT - 1), 0)),
                     pl.BlockSpec((tm, D_MODEL), lambda s: (jnp.maximum(_tile(s) - TILES_PROMPT, 0), 0))]
        out_shape = [jax.ShapeDtypeStruct((ROWS_PROMPT, D_MODEL), F32),
                     jax.ShapeDtypeStruct((ROWS_REST, D_MODEL), F32)]
    else:
        out_specs = _row_spec(D_MODEL)
        out_shape = jax.ShapeDtypeStruct((ROWS_PAD, D_MODEL), F32)
    return pl.pallas_call(
        functools.partial(_post_kernel, len(mixes), final),
        grid=(N_WCHUNK + N_TILES,),
        in_specs=in_specs,
        out_specs=out_specs,
        out_shape=out_shape,
        scratch_shapes=[pltpu.VMEM((AB_OUT, D_MODEL), BF16), *_ffn_scratch()],
        compiler_params=_cparams(("arbitrary",)),
        name="post",
    )(*args)


S5_HALF_IN = S5_WIDTH // 2
S5_HALF_CH = S5_CH // 2
SUBLANES = 8


def _s5_kernel(chained, tc, u_ref, h0re_ref, h0im_ref, bcat_ref, cre_ref, cim_ref, ast_re_ref, ast_im_ref,
               apw_re_ref, apw_im_ref, d_ref, wglu_ref, out_ref, hre_out, him_out, xre, xim, *carry):
    nb = tc // SUBLANES
    u = u_ref[...]
    ub = u.astype(BF16)
    for hf in range(2):
        xh = _dot(ub[:, hf * S5_HALF_IN:(hf + 1) * S5_HALF_IN], bcat_ref[hf])
        xre[:, hf * S5_HALF_CH:(hf + 1) * S5_HALF_CH] = xh[:, :S5_HALF_CH]
        xim[:, hf * S5_HALF_CH:(hf + 1) * S5_HALF_CH] = xh[:, S5_HALF_CH:]

    sr = xre[...].reshape(nb, SUBLANES, S5_CH)
    si = xim[...].reshape(nb, SUBLANES, S5_CH)
    rowi = lax.broadcasted_iota(jnp.int32, (nb, SUBLANES, S5_CH), 1)
    for step, d in enumerate((1, 2, 4)):
        ar = ast_re_ref[step:step + 1, :][None]
        ai = ast_im_ref[step:step + 1, :][None]
        pr = pltpu.roll(sr, d, 1)
        pi = pltpu.roll(si, d, 1)
        keep = rowi >= d
        sr, si = (sr + jnp.where(keep, ar * pr - ai * pi, 0.0),
                  si + jnp.where(keep, ar * pi + ai * pr, 0.0))
    apr = apw_re_ref[...]
    api = apw_im_ref[...]
    if chained:
        cre, cim = carry
        xre[...] = sr.reshape(tc, S5_CH)
        xim[...] = si.reshape(tc, S5_CH)

        @pl.when(pl.program_id(1) == 0)
        def _():
            cre[...] = h0re_ref[...]
            cim[...] = h0im_ref[...]

        def group(r, c):
            hr, hi = c
            rows = pl.ds(pl.multiple_of(r * SUBLANES, SUBLANES), SUBLANES)
            nr = xre[rows, :] + apr * hr - api * hi
            ni = xim[rows, :] + apr * hi + api * hr
            xre[rows, :] = nr
            xim[rows, :] = ni
            return nr[SUBLANES - 1:, :], ni[SUBLANES - 1:, :]

        hr, hi = lax.fori_loop(0, nb, group, (cre[...], cim[...]), unroll=min(nb, 4))
        cre[...] = hr
        cim[...] = hi
        hre_out[...] = hr
        him_out[...] = hi
    else:
        seq_of_row = lax.broadcasted_iota(jnp.int32, (tc, nb), 0) // SUBLANES
        rep = jnp.where(seq_of_row == lax.broadcasted_iota(jnp.int32, (tc, nb), 1), 1.0, 0.0).astype(BF16)
        last_row = lax.broadcasted_iota(jnp.int32, (nb, tc), 0) * SUBLANES + (SUBLANES - 1)
        pick = jnp.where(last_row == lax.broadcasted_iota(jnp.int32, (nb, tc), 1), 1.0, 0.0).astype(BF16)
        h0r = _dot_exact01(rep, h0re_ref[...]).reshape(nb, SUBLANES, S5_CH)
        h0i = _dot_exact01(rep, h0im_ref[...]).reshape(nb, SUBLANES, S5_CH)
        fr = (sr + apr[None] * h0r - api[None] * h0i).reshape(tc, S5_CH)
        fi = (si + apr[None] * h0i + api[None] * h0r).reshape(tc, S5_CH)
        xre[...] = fr
        xim[...] = fi
        hre_out[...] = _dot_exact01(pick, fr)
        him_out[...] = _dot_exact01(pick, fi)

    ys = []
    for hf in range(2):
        cols = slice(hf * S5_HALF_CH, (hf + 1) * S5_HALF_CH)
        ys.append(_dot(xre[:, cols].astype(BF16), cre_ref[hf]) + _dot(xim[:, cols].astype(BF16), cim_ref[hf]))
    y = jnp.concatenate(ys, axis=1) + d_ref[...] * u
    z = jax.nn.gelu(y)
    _store_rows(out_ref, z * jax.nn.sigmoid(_dot(z.astype(BF16), wglu_ref[...])))


def _s5_call(p, row0, n_seq, seq_len, h0re, h0im, prm, chained, tc, out_rows=None):
    n_rows = n_seq * seq_len
    blk0 = row0 // tc
    assert row0 % tc == 0 and n_rows % tc == 0
    out_blk = tc if out_rows is None else out_rows
    assert out_rows is None or n_rows == tc
    consts = [prm["bcat"], prm["cre"], prm["cim"], prm["ast_re"], prm["ast_im"], prm["apw_re"], prm["apw_im"],
              prm["d"], prm["wglu"]]
    const_specs = [_const_spec(c.shape, (0,) * c.ndim) for c in consts]
    scratch = [pltpu.VMEM((tc, S5_CH), F32), pltpu.VMEM((tc, S5_CH), F32)]
    if chained:
        n_chunk = seq_len // tc
        grid = (n_seq, n_chunk)
        u_spec = pl.BlockSpec((tc, S5_WIDTH), lambda b, c: (blk0 + b * n_chunk + c, 0))
        h_specs = [_const_spec((1, S5_CH), (0, 0))] * 2
        out_specs = [pl.BlockSpec((out_blk, S5_WIDTH), lambda b, c: (b * n_chunk + c, 0)),
                     pl.BlockSpec((None, 1, S5_CH), lambda b, c: (b, 0, 0)),
                     pl.BlockSpec((None, 1, S5_CH), lambda b, c: (b, 0, 0))]
        out_shape = [jax.ShapeDtypeStruct((n_rows // tc * out_blk, S5_WIDTH), BF16),
                     jax.ShapeDtypeStruct((n_seq, 1, S5_CH), F32), jax.ShapeDtypeStruct((n_seq, 1, S5_CH), F32)]
        scratch += [pltpu.VMEM((1, S5_CH), F32), pltpu.VMEM((1, S5_CH), F32)]
        sem = ("arbitrary", "arbitrary")
    else:
        assert seq_len == SUBLANES
        grid = (n_rows // tc,)
        u_spec = pl.BlockSpec((tc, S5_WIDTH), lambda i: (blk0 + i, 0))
        nb = tc // SUBLANES
        h_specs = [pl.BlockSpec((nb, S5_CH), lambda i: (i, 0))] * 2
        out_specs = [pl.BlockSpec((tc, S5_WIDTH), lambda i: (i, 0)),
                     pl.BlockSpec((nb, S5_CH), lambda i: (i, 0)), pl.BlockSpec((nb, S5_CH), lambda i: (i, 0))]
        out_shape = [jax.ShapeDtypeStruct((n_rows, S5_WIDTH), BF16),
                     jax.ShapeDtypeStruct((n_seq, S5_CH), F32), jax.ShapeDtypeStruct((n_seq, S5_CH), F32)]
        sem = ("arbitrary",)
    return pl.pallas_call(
        functools.partial(_s5_kernel, chained, tc),
        grid=grid,
        in_specs=[u_spec, *h_specs, *const_specs],
        out_specs=out_specs,
        out_shape=out_shape,
        scratch_shapes=scratch,
        compiler_params=_cparams(sem),
        name="s5",
    )(p, h0re, h0im, *consts)


S5_SLABS = S5_WIDTH // LANES


def _s5_batch_kernel(n_seq, tt, *refs):
    u_refs = refs[:n_seq]
    (h0re_ref, h0im_ref, bcat_ref, cre_ref, cim_ref, are_ref, aim_ref, d_ref, wglu_ref,
     out_ref, hre_out, him_out, u_tb, o_tb, xre, xim, cre, cim) = refs[n_seq:]
    @pl.when(pl.program_id(0) == 0)
    def _():
        cre[...] = jnp.broadcast_to(h0re_ref[...], (n_seq, S5_CH))
        cim[...] = jnp.broadcast_to(h0im_ref[...], (n_seq, S5_CH))

    for b in range(n_seq):
        ub = u_refs[b][...]
        for sl in range(S5_SLABS):
            u_tb[sl, pl.ds(b, tt, stride=n_seq), :] = ub[:, sl * LANES:(sl + 1) * LANES]
    u = jnp.concatenate([u_tb[sl] for sl in range(S5_SLABS)], axis=1)
    ubf = u.astype(BF16)
    for hf in range(2):
        xh = _dot(ubf[:, hf * S5_HALF_IN:(hf + 1) * S5_HALF_IN], bcat_ref[hf])
        xre[:, hf * S5_HALF_CH:(hf + 1) * S5_HALF_CH] = xh[:, :S5_HALF_CH]
        xim[:, hf * S5_HALF_CH:(hf + 1) * S5_HALF_CH] = xh[:, S5_HALF_CH:]

    for hf in range(2):
        cols = slice(hf * S5_HALF_CH, (hf + 1) * S5_HALF_CH)
        ar = are_ref[:, cols]
        ai = aim_ref[:, cols]

        def step(t, c):
            hr, hi = c
            rw = pl.ds(pl.multiple_of(t * n_seq, n_seq), n_seq)
            nr = xre[rw, cols] + (ar * hr - ai * hi)
            ni = xim[rw, cols] + (ar * hi + ai * hr)
            xre[rw, cols] = nr
            xim[rw, cols] = ni
            return nr, ni

        hr, hi = lax.fori_loop(0, tt, step, (cre[:, cols], cim[:, cols]), unroll=tt)
        cre[:, cols] = hr
        cim[:, cols] = hi
        hre_out[:, cols] = hr
        him_out[:, cols] = hi

    ys = []
    for hf in range(2):
        cols = slice(hf * S5_HALF_CH, (hf + 1) * S5_HALF_CH)
        ys.append(_dot(xre[:, cols].astype(BF16), cre_ref[hf]) + _dot(xim[:, cols].astype(BF16), cim_ref[hf]))
    y = jnp.concatenate(ys, axis=1) + d_ref[...] * u
    z = jax.nn.gelu(y)
    o = z * jax.nn.sigmoid(_dot(z.astype(BF16), wglu_ref[...]))
    for sl in range(S5_SLABS):
        o_tb[sl] = o[:, sl * LANES:(sl + 1) * LANES]
    for b in range(n_seq):
        ob = jnp.concatenate([o_tb[sl, pl.ds(b, tt, stride=n_seq), :] for sl in range(S5_SLABS)], axis=1)
        out_ref[b] = ob.astype(out_ref.dtype)


def _s5_batch_call(p, row0, n_seq, seq_len, h0re, h0im, prm, tt):
    assert n_seq == SUBLANES and seq_len % tt == 0 and row0 % tt == 0
    n_chunk = seq_len // tt
    blk0 = row0 // tt
    a8 = lambda a: jnp.broadcast_to(a[:1], (SUBLANES, S5_CH))
    consts = [prm["bcat"], prm["cre"], prm["cim"], a8(prm["apw_re"]), a8(prm["apw_im"]), prm["d"], prm["wglu"]]
    const_specs = [_const_spec(c.shape, (0,) * c.ndim) for c in consts]
    u_specs = [pl.BlockSpec((tt, S5_WIDTH), functools.partial(lambda b, c: (blk0 + b * n_chunk + c, 0), b))
               for b in range(n_seq)]
    rows = n_seq * tt
    out, hre, him = pl.pallas_call(
        functools.partial(_s5_batch_kernel, n_seq, tt),
        grid=(n_chunk,),
        in_specs=[*u_specs, _const_spec((1, S5_CH), (0, 0)), _const_spec((1, S5_CH), (0, 0)), *const_specs],
        out_specs=[pl.BlockSpec((n_seq, tt, S5_WIDTH), lambda c: (0, c, 0)),
                   pl.BlockSpec((n_seq, S5_CH), lambda c: (0, 0)), pl.BlockSpec((n_seq, S5_CH), lambda c: (0, 0))],
        out_shape=[jax.ShapeDtypeStruct((n_seq, seq_len, S5_WIDTH), BF16),
                   jax.ShapeDtypeStruct((n_seq, S5_CH), F32), jax.ShapeDtypeStruct((n_seq, S5_CH), F32)],
        scratch_shapes=[pltpu.VMEM((S5_SLABS, rows, LANES), F32), pltpu.VMEM((S5_SLABS, rows, LANES), F32),
                        pltpu.VMEM((rows, S5_CH), F32), pltpu.VMEM((rows, S5_CH), F32),
                        pltpu.VMEM((n_seq, S5_CH), F32), pltpu.VMEM((n_seq, S5_CH), F32)],
        compiler_params=_cparams(("arbitrary",)),
        name="s5_batch",
    )(*([p] * n_seq), h0re, h0im, *consts)
    return out.reshape(n_seq * seq_len, S5_WIDTH), hre, him


def _s5_params(a_re, a_im, log_dt, b_re, b_im, c_re, c_im, d_skip, w_glu):
    dt = jnp.exp(log_dt)[:, None]
    mag = jnp.exp(dt * a_re)
    abar_re, abar_im = mag * jnp.cos(dt * a_im), mag * jnp.sin(dt * a_im)
    den = a_re * a_re + a_im * a_im
    num_re = abar_re - 1.0
    f_re = (num_re * a_re + abar_im * a_im) / den
    f_im = (abar_im * a_re - num_re * a_im) / den
    bbar_re = f_re[..., None] * b_re - f_im[..., None] * b_im
    bbar_im = f_re[..., None] * b_im + f_im[..., None] * b_re

    def block_diag_in(w):
        w = w.reshape(2, S5_GROUPS // 2, S5_STATE, S5_GROUP)
        eye = jnp.eye(S5_GROUPS // 2, dtype=F32)
        return jnp.einsum("hgpn,gk->hgnkp", w, eye).reshape(2, S5_HALF_IN, S5_HALF_CH)

    def block_diag_out(w):
        w = w.reshape(2, S5_GROUPS // 2, S5_GROUP, S5_STATE)
        eye = jnp.eye(S5_GROUPS // 2, dtype=F32)
        return jnp.einsum("hgnp,gk->hgpkn", w, eye).reshape(2, S5_HALF_CH, S5_HALF_IN)

    bcat = jnp.concatenate([block_diag_in(bbar_re), block_diag_in(bbar_im)], axis=-1).astype(BF16)
    ar, ai = abar_re.reshape(1, S5_CH), abar_im.reshape(1, S5_CH)
    pows_re, pows_im = [ar], [ai]
    for _ in range(SUBLANES - 1):
        pr, pi = pows_re[-1], pows_im[-1]
        pows_re.append(pr * ar - pi * ai)
        pows_im.append(pr * ai + pi * ar)
    zeros = jnp.zeros((SUBLANES - 3, S5_CH), F32)
    return dict(
        bcat=bcat,
        cre=block_diag_out(c_re).astype(BF16),
        cim=block_diag_out(-c_im).astype(BF16),
        ast_re=jnp.concatenate([pows_re[0], pows_re[1], pows_re[3], zeros], axis=0),
        ast_im=jnp.concatenate([pows_im[0], pows_im[1], pows_im[3], zeros], axis=0),
        apw_re=jnp.concatenate(pows_re, axis=0),
        apw_im=jnp.concatenate(pows_im, axis=0),
        d=d_skip.reshape(1, S5_WIDTH),
        wglu=w_glu.astype(BF16),
    )


def _ret_tables(t):
    gam = np.log(1.0 - 2.0 ** (-5.0 - np.arange(RET_HEADS, dtype=np.float64)))
    tt = np.arange(t, dtype=np.float64)
    dq = np.exp(gam[:, None] * (tt[None, :] + 1.0))
    dq = np.broadcast_to(dq.reshape(RET_HEADS * t, 1), (RET_HEADS * t, RET_DV))
    diff = tt[:, None] - tt[None, :]
    dm = np.where(diff >= 0, np.exp(gam[:, None, None] * np.maximum(diff, 0.0)[None]), 0.0)
    dk = np.exp(gam[:, None] * (t - 1.0 - tt[None, :]))
    dk = np.repeat(dk.T, RET_DK, axis=1)
    ds = np.repeat(np.exp(gam * t), RET_DK)[:, None] * np.ones((1, RET_DV))
    hm = np.repeat(np.eye(RET_HEADS), RET_DK, axis=1)
    f = lambda a: jnp.asarray(np.ascontiguousarray(a), F32)
    return dict(dq=f(dq), dm=f(dm.reshape(RET_HEADS * t, t)), dk=f(dk), ds=f(ds),
                hm=f(np.concatenate([hm, np.zeros((SUBLANES - RET_HEADS, RET_QK))], axis=0)))


def _rope_tables(pos):
    half = RET_DK // 2
    inv_freq = 1.0 / (ROPE_BASE ** (jnp.arange(half, dtype=F32) / half))
    ang = pos.astype(F32)[:, None] * inv_freq[None, :]
    cos, sin = jnp.cos(ang), jnp.sin(ang)
    zero = jnp.zeros_like(sin)
    tile = lambda a, b: jnp.tile(jnp.concatenate([a, b], axis=1), (1, RET_HEADS))
    return tile(cos, cos), tile(-sin, zero), tile(zero, sin)


def _ret_chunk(t, q, k, v, g, s, cos, s_up, s_dn, dq, dm, dk, ds, hm):
    half = RET_DK // 2

    def rope(x):
        return x * cos + pltpu.roll(x, RET_QK - half, 1) * s_up + pltpu.roll(x, half, 1) * s_dn

    qr = rope(q)
    kr = rope(k) * (RET_DK ** -0.5)
    qs = jnp.concatenate([qr * hm[h:h + 1, :] for h in range(RET_HEADS)], axis=0).astype(BF16)
    inter = _dot(qs, s.astype(BF16)) * dq
    prob = (_dot_nt(qs, kr.astype(BF16)) * dm).astype(BF16)
    vb = v.astype(BF16)
    outs = []
    for h in range(RET_HEADS):
        rows = slice(h * t, (h + 1) * t)
        cols = slice(h * RET_DV, (h + 1) * RET_DV)
        o = inter[rows] + _dot(prob[rows], vb[:, cols])
        mu = jnp.mean(o, axis=-1, keepdims=True)
        oc = o - mu
        var = jnp.mean(oc * oc, axis=-1, keepdims=True)
        gh = g[:, cols]
        outs.append(oc * lax.rsqrt(var + EPS) * (gh * jax.nn.sigmoid(gh)))
    kd = kr * dk
    ks = jnp.concatenate([kd * hm[h:h + 1, :] for h in range(RET_HEADS)], axis=0).astype(BF16)
    vs = jnp.concatenate([vb[:, h * RET_DV:(h + 1) * RET_DV] for h in range(RET_HEADS)], axis=0)
    s_new = s * ds + _dot_tn(ks, vs)
    return jnp.concatenate(outs, axis=1), s_new


def _ret_kernel(chained, t, n_sub, *refs):
    n_data = 4 * n_sub if chained else 4
    data = refs[:n_data]
    s0_ref, cos_ref, sup_ref, sdn_ref, dq_ref, dm_ref, dk_ref, ds_ref, hm_ref = refs[n_data:n_data + 9]
    rest = refs[n_data + 9:]
    out_ref, st_ref, *scratch = rest[-3:] if chained else rest[-2:]
    tabs = (dq_ref[...], dm_ref[...], dk_ref[...], ds_ref[...], hm_ref[...])
    if chained:
        (s_scr,) = scratch

        @pl.when(pl.program_id(1) == 0)
        def _():
            for s in range(n_sub):
                s_scr[s] = s0_ref[...]

        for s in range(n_sub):
            q_ref, k_ref, v_ref, g_ref = data[4 * s:4 * s + 4]
            o, s_new = _ret_chunk(t, q_ref[...], k_ref[...], v_ref[...], g_ref[...], s_scr[s],
                                  cos_ref[...], sup_ref[...], sdn_ref[...], *tabs)
            _store_rows(out_ref.at[s], o)
            s_scr[s] = s_new
            st_ref[s] = s_new
    else:
        q_ref, k_ref, v_ref, g_ref = data

        def one(i, carry):
            rows = pl.ds(pl.multiple_of(i * t, t), t)
            o, s_new = _ret_chunk(t, q_ref[rows, :], k_ref[rows, :], v_ref[rows, :], g_ref[rows, :], s0_ref[i],
                                  cos_ref[...], sup_ref[...], sdn_ref[...], *tabs)
            out_ref[rows, :] = o.astype(out_ref.dtype)
            st_ref[i] = s_new
            return carry

        lax.fori_loop(0, n_sub, one, 0, unroll=8)


def _stacked_state_io(n_sub, sd, layer_i, st_prev, n_in):
    spec = pl.BlockSpec((None, n_sub, *sd), lambda i: (layer_i, i, 0, 0))
    if st_prev is None:
        return spec, [], [], {}
    return spec, [pl.BlockSpec(memory_space=pl.ANY)], [st_prev], {n_in: 1}


def _ret_call(p, row0, n_seq, seq_len, s0, pos0, chained, t, n_sub=1, out_rows=None, layer_i=0, st_prev=None):
    n_rows = n_seq * seq_len
    rows_blk = t if chained else t * n_sub
    blk0 = row0 // rows_blk
    assert row0 % rows_blk == 0
    out_blk = rows_blk if out_rows is None else out_rows
    assert out_rows is None or n_rows == rows_blk
    tabs = _ret_tables(t)
    consts = [tabs[n] for n in ("dq", "dm", "dk", "ds", "hm")]
    const_specs = [_const_spec(c.shape, (0, 0)) for c in consts]
    cos, s_up, s_dn = _rope_tables(pos0 + jnp.arange(seq_len, dtype=jnp.int32))
    sd = RET_QK, RET_DV
    if chained:
        assert n_seq % n_sub == 0
        n_chunk = seq_len // t
        grid = (n_seq // n_sub, n_chunk)
        rmap = lambda s, cb: (lambda b, c: (blk0 + (b * n_sub + s) * n_chunk + c, cb))
        data_specs = []
        for s in range(n_sub):
            data_specs += [pl.BlockSpec((t, RET_QK), rmap(s, 2)), pl.BlockSpec((t, RET_QK), rmap(s, 3)),
                           pl.BlockSpec((t, RET_WIDTH), rmap(s, 2)), pl.BlockSpec((t, RET_WIDTH), rmap(s, 3))]
        data_specs.append(_const_spec(sd, (0, 0)))
        rope_specs = [pl.BlockSpec((t, RET_QK), lambda b, c: (c, 0))] * 3
        out_specs = [pl.BlockSpec((n_sub, out_blk, RET_WIDTH), lambda b, c: (b, c, 0)),
                     pl.BlockSpec((n_sub, *sd), lambda b, c: (b, 0, 0))]
        out_shape = [jax.ShapeDtypeStruct((n_seq, n_chunk * out_blk, RET_WIDTH), BF16),
                     jax.ShapeDtypeStruct((n_seq, *sd), F32)]
        extra_specs, extra_args, aliases = [], [], {}
        scratch = [pltpu.VMEM((n_sub, *sd), F32)]
        sem = ("arbitrary", "arbitrary")
        n_p = 4 * n_sub
    else:
        assert seq_len == t and n_seq % n_sub == 0
        grid = (n_seq // n_sub,)
        rmap = lambda cb: (lambda i: (blk0 + i, cb))
        st_spec, extra_specs, extra_args, aliases = _stacked_state_io(n_sub, sd, layer_i, st_prev, 13)
        data_specs = [pl.BlockSpec((rows_blk, RET_QK), rmap(2)), pl.BlockSpec((rows_blk, RET_QK), rmap(3)),
                      pl.BlockSpec((rows_blk, RET_WIDTH), rmap(2)), pl.BlockSpec((rows_blk, RET_WIDTH), rmap(3)),
                      st_spec]
        rope_specs = [_const_spec((t, RET_QK), (0, 0))] * 3
        out_specs = [pl.BlockSpec((rows_blk, RET_WIDTH), lambda i: (i, 0)), st_spec]
        out_shape = [jax.ShapeDtypeStruct((n_rows, RET_WIDTH), BF16), jax.ShapeDtypeStruct(s0.shape, F32)]
        scratch = []
        sem = ("arbitrary",)
        n_p = 4
    in_specs = [*data_specs, *rope_specs, *const_specs, *extra_specs]
    assert not aliases or list(aliases) == [len(in_specs) - 1]
    out, st = pl.pallas_call(
        functools.partial(_ret_kernel, chained, t, n_sub),
        grid=grid,
        in_specs=in_specs,
        out_specs=out_specs,
        out_shape=out_shape,
        scratch_shapes=scratch,
        input_output_aliases=aliases,
        compiler_params=_cparams(sem),
        name="retention",
    )(*([p] * n_p), s0, cos, s_up, s_dn, *consts, *extra_args)
    return out.reshape(-1, RET_WIDTH), st


def _gla_tables(t, n_blk):
    cg = t * n_blk
    blk = np.arange(cg) // t
    same = blk[:, None] == blk[None, :]
    cum = (same & (np.arange(cg)[:, None] >= np.arange(cg)[None, :])).astype(np.float32)
    ones = (np.arange(t * GLA_DK)[:, None] // GLA_DK == np.arange(LANES)[None, :]).astype(np.float32)
    return jnp.asarray(cum, BF16), jnp.asarray(ones, BF16)


LOG2E = 1.4426950408889634


def _gla_kernel(chained, t, n_blk, n_par, *refs):
    data = [refs[5 * s:5 * s + 5] for s in range(n_par)]
    s0_ref, wa_ref, ba_ref, ng_ref, cum_ref, ones_ref = refs[5 * n_par:5 * n_par + 6]
    rest = refs[5 * n_par + 6:]
    out_ref, st_ref, rt_scr, o_scr, *scratch = rest[-5:] if chained else rest[-4:]
    cg = t * n_blk
    nh, dk, dv = GLA_HEADS, GLA_DK, GLA_DV
    par = range(n_par)
    if chained:
        (s_scr,) = scratch

        @pl.when(pl.program_id(1) == 0)
        def _():
            for s in par:
                s_scr[s] = s0_ref[...]

    def pad_rows(x):
        return jnp.concatenate([x, jnp.zeros((LANES - x.shape[0], x.shape[1]), x.dtype)], axis=0)

    q, ksc, vb, b, qe, ke, last = [], [], [], [], [], [], []
    for s in par:
        q_ref, k_ref, v_ref, _, lr_ref = data[s]
        q.append(q_ref[...])
        ksc.append(k_ref[...] * (dk ** -0.5))
        vb.append(v_ref[...].astype(BF16))
        la = jax.nn.log_sigmoid(_dot(lr_ref[...].astype(BF16), wa_ref[...]) + ba_ref[...]) / GLA_TAU
        b.append(_dot_exact01(cum_ref[...], la) * LOG2E)
        last3 = b[s].reshape(n_blk, t, GLA_QK)[:, t - 1:t, :]
        bl = jnp.broadcast_to(last3, (n_blk, t, GLA_QK)).reshape(cg, GLA_QK)
        last.append(last3.reshape(n_blk, GLA_QK))
        qe.append(q[s] * jnp.exp2(b[s]))
        ke.append(ksc[s] * jnp.exp2(bl - b[s]))

    row_t = {lo: lo + lax.broadcasted_iota(jnp.int32, (t - lo, GLA_QK), 0) for lo in range(0, t, SUBLANES)}
    for j in range(n_blk):
        rows = slice(j * t, (j + 1) * t)
        for s in par:
            qj, kj, bj = q[s][rows], ksc[s][rows], b[s][rows]
            for i in range(t):
                lo = i // SUBLANES * SUBLANES
                e = jnp.exp2(jnp.where(row_t[lo] >= i, bj[lo:] - bj[i:i + 1, :], NEG_BIG))
                prod = (qj[lo:] * kj[i:i + 1, :]) * e
                for h in range(nh):
                    ph = prod[:, h * dk:(h + 1) * dk]
                    if lo:
                        ph = jnp.concatenate([jnp.zeros((lo, dk), F32), ph], axis=0)
                    rt_scr[s, (j * nh + h) * t:(j * nh + h + 1) * t, i * dk:(i + 1) * dk] = ph.astype(rt_scr.dtype)
    scores = [_dot(rt_scr[s].astype(BF16), ones_ref[...]) for s in par]

    inter, off = [], []
    if chained:
        for s in par:
            pref = [jnp.zeros((1, GLA_QK), F32)]
            for i in range(n_blk):
                pref.append(pref[i] + last[s][i:i + 1, :])
            expand = lambda rs: jnp.concatenate([jnp.broadcast_to(r, (t, GLA_QK)) for r in rs], axis=0)
            qhat = qe[s] * jnp.exp2(expand(pref[:n_blk]))
            khat = ke[s] * jnp.exp2(pref[n_blk] - expand(pref[1:]))
            a_col = jnp.exp2(pad_rows(pref[n_blk])).T[:, 0:1]
            qhat, khat = qhat.astype(BF16), khat.astype(BF16)
            per_head = []
            for h in range(nh):
                hk = slice(h * dk, (h + 1) * dk)
                state = s_scr[s, hk, :]
                per_head.append(_dot(qhat[:, hk], state.astype(BF16)))
                state = state * a_col[hk] + _dot_tn(khat[:, hk], vb[s][:, h * dv:(h + 1) * dv])
                s_scr[s, hk, :] = state
                st_ref[s, hk, :] = state
            inter.append(per_head)
            per_blk = [None]
            for i in range(1, n_blk):
                parts = [ke[s][j * t:(j + 1) * t] * jnp.exp2(pref[i] - pref[j + 1]) for j in range(i)]
                rhs = pad_rows(jnp.concatenate(parts, axis=0)).astype(BF16)
                qi = qe[s][i * t:(i + 1) * t].astype(BF16)
                per_blk.append([_dot_nt(qi[:, h * dk:(h + 1) * dk], rhs[:, h * dk:(h + 1) * dk])
                                for h in range(nh)])
            off.append(per_blk)
    else:
        a_cols = jnp.exp2(pad_rows(last[0])).T
        qb, kb = qe[0].astype(BF16), ke[0].astype(BF16)
        for j in range(n_blk):
            rows = slice(j * t, (j + 1) * t)
            for h in range(nh):
                hk = slice(h * dk, (h + 1) * dk)
                cols = slice(h * dv, (h + 1) * dv)
                state = s0_ref[j, hk, :]
                o_scr[0, rows, cols] = _dot(qb[rows, hk], state.astype(BF16))
                st_ref[j, hk, :] = state * a_cols[hk, j:j + 1] + _dot_tn(kb[rows, hk], vb[0][rows, cols])
    for s in par:
        r = data[s][3][...]
        gated = []
        for h in range(nh):
            pieces = []
            for j in range(n_blk):
                piece = scores[s][(j * nh + h) * t:(j * nh + h + 1) * t, :]
                piece = pltpu.roll(piece, j * t, 1) if j else piece
                if chained and j:
                    piece = piece + off[s][j][h]
                pieces.append(piece)
            pfull = jnp.concatenate(pieces, axis=0) if n_blk > 1 else pieces[0]
            cols = slice(h * dv, (h + 1) * dv)
            o_inter = inter[s][h] if chained else o_scr[s, :, cols]
            oh = o_inter + _dot(pfull[:, :cg].astype(BF16), vb[s][:, cols])
            on = oh * lax.rsqrt(jnp.mean(oh * oh, axis=-1, keepdims=True) + EPS) * ng_ref[...]
            rh = r[:, cols]
            gated.append(on * (rh * jax.nn.sigmoid(rh)))
        _store_rows(out_ref.at[s] if chained else out_ref, jnp.concatenate(gated, axis=1))


def _gla_call(p, row0, n_seq, seq_len, s0, wa, ba, ng, chained, t, n_blk, n_par=1, out_rows=None, layer_i=0,
              st_prev=None):
    cg = t * n_blk
    blk0 = row0 // cg
    assert row0 % cg == 0 and cg <= LANES
    out_blk = cg if out_rows is None else out_rows
    assert out_rows is None or seq_len == cg
    cum, ones = _gla_tables(t, n_blk)
    consts = [wa, ba, ng, cum, ones]
    const_specs = [_const_spec(c.shape, (0,) * c.ndim) for c in consts]
    sd = GLA_QK, GLA_DV
    lr_col = (2 * GLA_QK + 2 * GLA_V) // LANES
    rt_dtype = BF16 if t % (2 * SUBLANES) == 0 else F32
    if chained:
        assert n_seq % n_par == 0
        n_chunk = seq_len // cg
        grid = (n_seq // n_par, n_chunk)
        rmap = lambda s, cb: (lambda b, c: (blk0 + (b * n_par + s) * n_chunk + c, cb))
        s_spec = _const_spec(sd, (0, 0))
        out_specs = [pl.BlockSpec((n_par, out_blk, GLA_V), lambda b, c: (b, c, 0)),
                     pl.BlockSpec((n_par, *sd), lambda b, c: (b, 0, 0))]
        out_shape = [jax.ShapeDtypeStruct((n_seq, n_chunk * out_blk, GLA_V), BF16),
                     jax.ShapeDtypeStruct((n_seq, *sd), F32)]
        extra_specs, extra_args, aliases = [], [], {}
        scratch = [pltpu.VMEM((n_par, *sd), F32)]
        sem = ("arbitrary", "arbitrary")
    else:
        assert seq_len == t and n_seq % n_blk == 0 and n_par == 1
        grid = (n_seq // n_blk,)
        rmap = lambda s, cb: (lambda i: (blk0 + i, cb))
        s_spec, extra_specs, extra_args, aliases = _stacked_state_io(n_blk, sd, layer_i, st_prev, 11)
        out_specs = [pl.BlockSpec((cg, GLA_V), lambda i: (i, 0)), s_spec]
        out_shape = [jax.ShapeDtypeStruct((n_seq * seq_len, GLA_V), BF16), jax.ShapeDtypeStruct(s0.shape, F32)]
        scratch = []
        sem = ("arbitrary",)
    data_specs = []
    for s in range(n_par):
        data_specs += [pl.BlockSpec((cg, GLA_QK), rmap(s, 0)), pl.BlockSpec((cg, GLA_QK), rmap(s, 1)),
                       pl.BlockSpec((cg, GLA_V), rmap(s, 1)), pl.BlockSpec((cg, GLA_V), rmap(s, 2)),
                       pl.BlockSpec((cg, LANES), rmap(s, lr_col))]
    in_specs = [*data_specs, s_spec, *const_specs, *extra_specs]
    assert not aliases or list(aliases) == [len(in_specs) - 1]
    out, st = pl.pallas_call(
        functools.partial(_gla_kernel, chained, t, n_blk, n_par),
        grid=grid,
        in_specs=in_specs,
        out_specs=out_specs,
        out_shape=out_shape,
        scratch_shapes=[pltpu.VMEM((n_par, n_blk * GLA_HEADS * t, t * GLA_DK), rt_dtype),
                        pltpu.VMEM((n_par, cg, GLA_V), F32), *scratch],
        input_output_aliases=aliases,
        compiler_params=_cparams(sem),
        name="gla",
    )(*([p] * (5 * n_par)), s0, *consts, *extra_args)
    return out.reshape(-1, GLA_V), st


S5_TT_PROMPT = 128
S5_TC_SAMPLE = 256
RET_T_PROMPT = 512
RET_PAR_PROMPT = 1
RET_SUB_SAMPLE = 16
GLA_T_PROMPT = 16
GLA_BLK_PROMPT = 8
GLA_BLK_SAMPLE = 8
GLA_PAR_PROMPT = 2


def _mixer_ab(p, i, prm, st_s5_re, st_s5_im, st_ret, s_ret_prev):
    zeros_h = jnp.zeros((1, S5_CH), F32)
    a_m, hre_m, him_m = _s5_call(p, ROW0_META, 1, N_META, zeros_h, zeros_h, prm, True, N_META, out_rows=DENSE_TM)
    a_p, hre_p, him_p = _s5_batch_call(p, 0, BATCH, SEQ, hre_m[0], him_m[0], prm, S5_TT_PROMPT)
    a_s, hre_s, him_s = _s5_call(p, ROW0_SAMPLE, DEC_BATCH, DEC_SEQ, st_s5_re[i].reshape(DEC_BATCH, S5_CH),
                                 st_s5_im[i].reshape(DEC_BATCH, S5_CH), prm, False, S5_TC_SAMPLE)
    last = lambda h: h.reshape(DEC_BATCH, S5_GROUPS, S5_STATE)

    zeros_s = jnp.zeros((RET_QK, RET_DV), F32)
    b_m, s_m = _ret_call(p, ROW0_META, 1, N_META, zeros_s, 0, True, N_META, out_rows=DENSE_TM)
    b_p, s_p = _ret_call(p, 0, BATCH, SEQ, s_m[0], N_META, True, RET_T_PROMPT, n_sub=RET_PAR_PROMPT)
    b_s, s_ret = _ret_call(p, ROW0_SAMPLE, DEC_BATCH, DEC_SEQ, st_ret, PAST_LEN, False, DEC_SEQ, RET_SUB_SAMPLE,
                           layer_i=i, st_prev=s_ret_prev)
    states = dict(
        p_s5_re=hre_p.reshape(BATCH, S5_GROUPS, S5_STATE), p_s5_im=him_p.reshape(BATCH, S5_GROUPS, S5_STATE),
        p_ret=s_p.reshape(BATCH, RET_HEADS, RET_DK, RET_DV), s_s5_re=last(hre_s), s_s5_im=last(him_s))
    return [(a_p, a_s, a_m), (b_p, b_s, b_m)], states, s_ret


def _mixer_gla(p, i, wa, ba, ng, st_gla, s_gla_prev):
    zeros_s = jnp.zeros((GLA_QK, GLA_DV), F32)
    o_m, s_m = _gla_call(p, ROW0_META, 1, N_META, zeros_s, wa, ba, ng, True, N_META, 1, out_rows=DENSE_TM)
    o_p, s_p = _gla_call(p, 0, BATCH, SEQ, s_m[0], wa, ba, ng, True, GLA_T_PROMPT, GLA_BLK_PROMPT,
                         n_par=GLA_PAR_PROMPT)
    o_s, s_gla = _gla_call(p, ROW0_SAMPLE, DEC_BATCH, DEC_SEQ, st_gla, wa, ba, ng, False, DEC_SEQ, GLA_BLK_SAMPLE,
                           layer_i=i, st_prev=s_gla_prev)
    states = dict(p_gla=s_p.reshape(BATCH, GLA_HEADS, GLA_DK, GLA_DV))
    return [(o_p, o_s, o_m)], states, s_gla


def kernel(x_prompt, x_sample, state_s5_re, state_s5_im, state_ret, state_gla, meta_tokens, norm_ffn1, norm_mix,
           norm_ffn2, norm_final, ffn1_w_gu, ffn1_w_down, ffn2_w_gu, ffn2_w_down, ab_w_in, ab_w_out, s5_a_re,
           s5_a_im, s5_log_dt, s5_b_re, s5_b_im, s5_c_re, s5_c_im, s5_d, s5_w_glu, gla_w_in, gla_w_alpha2,
           gla_b_alpha, gla_norm, gla_w_out):
    ffn1_gu, ffn1_dn, ffn2_gu, ffn2_dn = ffn1_w_gu, ffn1_w_down, ffn2_w_gu, ffn2_w_down
    ab_in, ab_out, gla_in, gla_out = ab_w_in, ab_w_out, gla_w_in, gla_w_out
    gla_wa = jnp.pad(gla_w_alpha2, ((0, 0), (0, LANES - GLA_LOWRANK), (0, 0))).astype(BF16)
    n1 = norm_ffn1.reshape(DEPTH, 1, D_MODEL)
    nm = norm_mix.reshape(DEPTH, 1, D_MODEL)
    n2 = norm_ffn2.reshape(DEPTH, 1, D_MODEL)

    meta_pad = jnp.pad(meta_tokens.astype(x_prompt.dtype), ((0, DENSE_TM - N_META), (0, 0)))
    xs = [x_prompt.reshape(ROWS_PROMPT, D_MODEL), x_sample.reshape(ROWS_SAMPLE, D_MODEL), meta_pad]
    st_ret = state_ret.reshape(N_EVEN, DEC_BATCH, RET_QK, RET_DV)
    st_gla = state_gla.reshape(N_ODD, DEC_BATCH, GLA_QK, GLA_DV)
    s_ret = s_gla = None
    collected = {}
    for layer in range(DEPTH):
        i = layer // 2
        last = layer == DEPTH - 1
        g_final = norm_final.reshape(1, D_MODEL) if last else None
        if layer % 2 == 0:
            x1, p = _pre_call(xs, n1, ffn1_gu, ffn1_dn, nm, ab_in, layer, i, AB_IN)
            prm = _s5_params(s5_a_re[i], s5_a_im[i], s5_log_dt[i], s5_b_re[i], s5_b_im[i], s5_c_re[i],
                             s5_c_im[i], s5_d[i], s5_w_glu[i])
            mixes, states, s_ret = _mixer_ab(p, i, prm, state_s5_re, state_s5_im, st_ret, s_ret)
            x = _post_call(x1, mixes, ab_out, n2, ffn2_gu, ffn2_dn, layer, i, g_final)
        else:
            x1, p = _pre_call(xs, n1, ffn1_gu, ffn1_dn, nm, gla_in, layer, i, GLA_IN_PAD)
            mixes, states, s_gla = _mixer_gla(p, i, gla_wa[i], gla_b_alpha[i].reshape(1, GLA_QK),
                                              gla_norm[i].reshape(1, GLA_DV), st_gla, s_gla)
            x = _post_call(x1, mixes, gla_out, n2, ffn2_gu, ffn2_dn, layer, i, g_final)
        xs = [x]
        for name, val in states.items():
            collected.setdefault(name, []).append(val)
    out = {name: jnp.stack(vals) for name, vals in collected.items()}
    y_prompt, y_rest = x
    return (y_prompt.reshape(BATCH, SEQ, D_MODEL), y_rest[:ROWS_SAMPLE].reshape(DEC_BATCH, DEC_SEQ, D_MODEL),
            out["p_s5_re"], out["p_s5_im"], out["p_ret"], out["p_gla"], out["s_s5_re"], out["s_s5_im"],
            s_ret.reshape(N_EVEN, DEC_BATCH, RET_HEADS, RET_DK, RET_DV),
            s_gla.reshape(N_ODD, DEC_BATCH, GLA_HEADS, GLA_DK, GLA_DV))
```

```python
import functools

import numpy as np
import jax
import jax.numpy as jnp
from jax import lax
from jax.experimental import pallas as pl
from jax.experimental.pallas import tpu as pltpu

F32 = jnp.float32
BF16 = jnp.bfloat16

D_MODEL = 1024
BATCH = 8
SEQ = 2048
DEPTH = 4
DEC_BATCH = 128
DEC_SEQ = 8
PAST_LEN = 16384
N_META = 16
N_EVEN = (DEPTH + 1) // 2
N_ODD = DEPTH // 2
S5_WIDTH = D_MODEL // 2
S5_GROUP = 16
S5_GROUPS = S5_WIDTH // S5_GROUP
S5_STATE = 64
S5_CH = S5_GROUPS * S5_STATE
RET_HEADS = 4
RET_DK = D_MODEL // 16
RET_DV = 2 * RET_DK
RET_QK = RET_HEADS * RET_DK
RET_WIDTH = RET_HEADS * RET_DV
AB_IN = S5_WIDTH + 2 * RET_QK + 2 * RET_WIDTH
AB_OUT = S5_WIDTH + RET_WIDTH
GLA_HEADS = 4
GLA_DK = D_MODEL // (2 * GLA_HEADS)
GLA_DV = D_MODEL // GLA_HEADS
GLA_QK = GLA_HEADS * GLA_DK
GLA_V = GLA_HEADS * GLA_DV
GLA_LOWRANK = 16
GLA_TAU = 16.0
GLA_IN = 2 * GLA_QK + 2 * GLA_V + GLA_LOWRANK
LANES = 128
GLA_IN_PAD = 2 * GLA_QK + 2 * GLA_V + LANES
D_FF = 128 * ((8 * D_MODEL // 3 + 127) // 128)
EPS = 1e-6
ROPE_BASE = 10000.0
NEG_BIG = -1e30

ROWS_PROMPT = BATCH * SEQ
ROWS_SAMPLE = DEC_BATCH * DEC_SEQ
ROW0_SAMPLE = ROWS_PROMPT
ROW0_META = ROWS_PROMPT + ROWS_SAMPLE

VMEM_LIMIT = 56 * 1024 * 1024
DENSE_TM = 256
TILES_PROMPT = ROWS_PROMPT // DENSE_TM
TILES_SAMPLE = ROWS_SAMPLE // DENSE_TM
N_TILES = TILES_PROMPT + TILES_SAMPLE + 1
ROWS_PAD = N_TILES * DENSE_TM
ROWS_REST = ROWS_PAD - ROWS_PROMPT


def _cparams(sem):
    return pltpu.CompilerParams(dimension_semantics=sem, vmem_limit_bytes=VMEM_LIMIT)


def _dot(a, b):
    return jnp.dot(a, b, preferred_element_type=F32)


def _dot_tn(a, b):
    return lax.dot_general(a, b, (((0,), (0,)), ((), ())), preferred_element_type=F32)


def _dot_nt(a, b):
    return lax.dot_general(a, b, (((1,), (1,)), ((), ())), preferred_element_type=F32)


def _dot_exact01(m_bf, x):
    h1 = x.astype(BF16)
    r1 = x - h1.astype(F32)
    h2 = r1.astype(BF16)
    r2 = r1 - h2.astype(F32)
    h3 = r2.astype(BF16)
    return _dot(m_bf, h1) + _dot(m_bf, h2) + _dot(m_bf, h3)


def _rms(x, g):
    return x * lax.rsqrt(jnp.mean(x * x, axis=-1, keepdims=True) + EPS) * g


def _swiglu_half(x, g, wg, wu, wd):
    h = _rms(x, g).astype(BF16)
    gate = _dot(h, wg)
    up = _dot(h, wu)
    act = (gate * jax.nn.sigmoid(gate) * up).astype(BF16)
    return x + 0.5 * _dot(act, wd)


def _store_rows(out_ref, val):
    rows = val.shape[0]
    out_ref[0:rows, :] = val.astype(out_ref.dtype)
    if out_ref.shape[0] > rows:
        out_ref[rows:, :] = jnp.zeros((out_ref.shape[0] - rows, out_ref.shape[1]), out_ref.dtype)


def _const_spec(shape, index):
    return pl.BlockSpec(shape, lambda *_: index, pipeline_mode=pl.Buffered(1))


N_WCHUNK = 8


def _tile(g):
    return jnp.maximum(g - N_WCHUNK, 0)


def _row_spec(width):
    return pl.BlockSpec((DENSE_TM, width), lambda g: (_tile(g), 0))


def _group_specs(width):
    tm = DENSE_TM
    return [pl.BlockSpec((tm, width), lambda g: (jnp.minimum(_tile(g), TILES_PROMPT - 1), 0)),
            pl.BlockSpec((tm, width), lambda g: (jnp.clip(_tile(g) - TILES_PROMPT, 0, TILES_SAMPLE - 1), 0)),
            pl.BlockSpec((tm, width), lambda g: (0, 0))]


def _pick_group(refs):
    if len(refs) == 1:
        return refs[0][...]
    i = pl.program_id(0) - N_WCHUNK
    return jnp.where(i < TILES_PROMPT, refs[0][...],
                     jnp.where(i < TILES_PROMPT + TILES_SAMPLE, refs[1][...], refs[2][...]))


def _chunk_spec(w, lead):
    _, rows, cols = w.shape
    assert rows % (N_WCHUNK * 2 * SUBLANES) == 0
    return pl.BlockSpec((None, rows // N_WCHUNK, cols), lambda g: (lead, jnp.minimum(g, N_WCHUNK - 1), 0))


def _stage(dst, chunk, g, cols=None):
    rows = chunk.shape[0]
    r = pl.ds(pl.multiple_of(g * rows, rows), rows)
    val = chunk[...] if cols is None else chunk[:, cols]
    width = val.shape[1]
    dst[r, 0:width] = val.astype(BF16)
    if dst.shape[1] > width:
        dst[r, width:] = jnp.zeros((rows, dst.shape[1] - width), BF16)


def _pre_kernel(n_x, *refs):
    x_refs = refs[:n_x]
    g1_ref, wgu_ref, wd_ref, g2_ref, win_ref, x1_ref, p_ref, wg_s, wu_s, wd_s, win_s = refs[n_x:]
    g = pl.program_id(0)

    @pl.when(g < N_WCHUNK)
    def _():
        _stage(wg_s, wgu_ref, g, slice(0, D_FF))
        _stage(wu_s, wgu_ref, g, slice(D_FF, 2 * D_FF))
        _stage(wd_s, wd_ref, g)
        _stage(win_s, win_ref, g)

    @pl.when(g >= N_WCHUNK)
    def _():
        x1 = _swiglu_half(_pick_group(x_refs), g1_ref[...], wg_s[...], wu_s[...], wd_s[...])
        x1_ref[...] = x1
        p_ref[...] = _dot(_rms(x1, g2_ref[...]).astype(BF16), win_s[...])


def _ffn_scratch():
    return [pltpu.VMEM((D_MODEL, D_FF), BF16), pltpu.VMEM((D_MODEL, D_FF), BF16), pltpu.VMEM((D_FF, D_MODEL), BF16)]


def _pre_call(xs, g1, w_gu, w_down, g2, w_in, layer, mix_idx, n_in):
    x_specs = [_row_spec(D_MODEL)] if len(xs) == 1 else _group_specs(D_MODEL)
    return pl.pallas_call(
        functools.partial(_pre_kernel, len(xs)),
        grid=(N_WCHUNK + N_TILES,),
        in_specs=[
            *x_specs,
            _const_spec((None, 1, D_MODEL), (layer, 0, 0)),
            _chunk_spec(w_gu, layer),
            _chunk_spec(w_down, layer),
            _const_spec((None, 1, D_MODEL), (layer, 0, 0)),
            _chunk_spec(w_in, mix_idx),
        ],
        out_specs=[_row_spec(D_MODEL), _row_spec(n_in)],
        out_shape=[jax.ShapeDtypeStruct((ROWS_PAD, D_MODEL), F32), jax.ShapeDtypeStruct((ROWS_PAD, n_in), F32)],
        scratch_shapes=[*_ffn_scratch(), pltpu.VMEM((D_MODEL, n_in), BF16)],
        compiler_params=_cparams(("arbitrary",)),
        name="pre",
    )(*xs, g1, w_gu, w_down, g2, w_in)


def _post_kernel(n_mix, final, *refs):
    x1_ref = refs[0]
    mix_refs = refs[1:1 + 3 * n_mix]
    wout_ref, g_ref, wgu_ref, wd_ref = refs[1 + 3 * n_mix:5 + 3 * n_mix]
    rest = refs[5 + 3 * n_mix:]
    wout_s, wg_s, wu_s, wd_s = rest[-4:]
    g = pl.program_id(0)

    @pl.when(g < N_WCHUNK)
    def _():
        _stage(wout_s, wout_ref, g)
        _stage(wg_s, wgu_ref, g, slice(0, D_FF))
        _stage(wu_s, wgu_ref, g, slice(D_FF, 2 * D_FF))
        _stage(wd_s, wd_ref, g)

    @pl.when(g >= N_WCHUNK)
    def _():
        x2 = x1_ref[...]
        width = AB_OUT // n_mix
        for i in range(n_mix):
            x2 = x2 + _dot(_pick_group(mix_refs[3 * i:3 * i + 3]), wout_s[i * width:(i + 1) * width, :])
        y = _swiglu_half(x2, g_ref[...], wg_s[...], wu_s[...], wd_s[...])
        if final:
            gf_ref, yp_ref, yr_ref = rest[:3]
            y = _rms(y, gf_ref[...])

            @pl.when(g < N_WCHUNK + TILES_PROMPT)
            def _():
                yp_ref[...] = y

            @pl.when(g >= N_WCHUNK + TILES_PROMPT)
            def _():
                yr_ref[...] = y
        else:
            rest[0][...] = y


def _post_call(x1, mixes, w_out, g, w_gu, w_down, layer, mix_idx, g_final):
    tm = DENSE_TM
    final = g_final is not None
    in_specs = [_row_spec(D_MODEL)]
    for triple in mixes:
        in_specs += _group_specs(triple[0].shape[1])
    in_specs += [
        _chunk_spec(w_out, mix_idx),
        _const_spec((None, 1, D_MODEL), (layer, 0, 0)),
        _chunk_spec(w_gu, layer),
        _chunk_spec(w_down, layer),
    ]
    args = [x1, *[m for triple in mixes for m in triple], w_out, g, w_gu, w_down]
    if final:
        in_specs.append(_const_spec((1, D_MODEL), (0, 0)))
        args.append(g_final)
        out_specs = [pl.BlockSpec((tm, D_MODEL), lambda s: (jnp.minimum(_tile(s), TILES_PROMPT - 1), 0)),
                     pl.BlockSpec((tm, D_MODEL), lambda s: (jnp.maximum(_tile(s) - TILES_PROMPT, 0), 0))]
        out_shape = [jax.ShapeDtypeStruct((ROWS_PROMPT, D_MODEL), F32),
                     jax.ShapeDtypeStruct((ROWS_REST, D_MODEL), F32)]
    else:
        out_specs = _row_spec(D_MODEL)
        out_shape = jax.ShapeDtypeStruct((ROWS_PAD, D_MODEL), F32)
    return pl.pallas_call(
        functools.partial(_post_kernel, len(mixes), final),
        grid=(N_WCHUNK + N_TILES,),
        in_specs=in_specs,
        out_specs=out_specs,
        out_shape=out_shape,
        scratch_shapes=[pltpu.VMEM((AB_OUT, D_MODEL), BF16), *_ffn_scratch()],
        compiler_params=_cparams(("arbitrary",)),
        name="post",
    )(*args)


S5_HALF_IN = S5_WIDTH // 2
S5_HALF_CH = S5_CH // 2
SUBLANES = 8


def _s5_kernel(chained, tc, u_ref, h0re_ref, h0im_ref, bcat_ref, cre_ref, cim_ref, ast_re_ref, ast_im_ref,
               apw_re_ref, apw_im_ref, d_ref, wglu_ref, out_ref, hre_out, him_out, xre, xim, *carry):
    nb = tc // SUBLANES
    u = u_ref[...]
    ub = u.astype(BF16)
    for hf in range(2):
        xh = _dot(ub[:, hf * S5_HALF_IN:(hf + 1) * S5_HALF_IN], bcat_ref[hf])
        xre[:, hf * S5_HALF_CH:(hf + 1) * S5_HALF_CH] = xh[:, :S5_HALF_CH]
        xim[:, hf * S5_HALF_CH:(hf + 1) * S5_HALF_CH] = xh[:, S5_HALF_CH:]

    sr = xre[...].reshape(nb, SUBLANES, S5_CH)
    si = xim[...].reshape(nb, SUBLANES, S5_CH)
    rowi = lax.broadcasted_iota(jnp.int32, (nb, SUBLANES, S5_CH), 1)
    for step, d in enumerate((1, 2, 4)):
        ar = ast_re_ref[step:step + 1, :][None]
        ai = ast_im_ref[step:step + 1, :][None]
        pr = pltpu.roll(sr, d, 1)
        pi = pltpu.roll(si, d, 1)
        keep = rowi >= d
        sr, si = (sr + jnp.where(keep, ar * pr - ai * pi, 0.0),
                  si + jnp.where(keep, ar * pi + ai * pr, 0.0))
    apr = apw_re_ref[...]
    api = apw_im_ref[...]
    if chained:
        cre, cim = carry
        xre[...] = sr.reshape(tc, S5_CH)
        xim[...] = si.reshape(tc, S5_CH)

        @pl.when(pl.program_id(1) == 0)
        def _():
            cre[...] = h0re_ref[...]
            cim[...] = h0im_ref[...]

        def group(r, c):
            hr, hi = c
            rows = pl.ds(pl.multiple_of(r * SUBLANES, SUBLANES), SUBLANES)
            nr = xre[rows, :] + apr * hr - api * hi
            ni = xim[rows, :] + apr * hi + api * hr
            xre[rows, :] = nr
            xim[rows, :] = ni
            return nr[SUBLANES - 1:, :], ni[SUBLANES - 1:, :]

        hr, hi = lax.fori_loop(0, nb, group, (cre[...], cim[...]), unroll=min(nb, 4))
        cre[...] = hr
        cim[...] = hi
        hre_out[...] = hr
        him_out[...] = hi
    else:
        seq_of_row = lax.broadcasted_iota(jnp.int32, (tc, nb), 0) // SUBLANES
        rep = jnp.where(seq_of_row == lax.broadcasted_iota(jnp.int32, (tc, nb), 1), 1.0, 0.0).astype(BF16)
        last_row = lax.broadcasted_iota(jnp.int32, (nb, tc), 0) * SUBLANES + (SUBLANES - 1)
        pick = jnp.where(last_row == lax.broadcasted_iota(jnp.int32, (nb, tc), 1), 1.0, 0.0).astype(BF16)
        h0r = _dot_exact01(rep, h0re_ref[...]).reshape(nb, SUBLANES, S5_CH)
        h0i = _dot_exact01(rep, h0im_ref[...]).reshape(nb, SUBLANES, S5_CH)
        fr = (sr + apr[None] * h0r - api[None] * h0i).reshape(tc, S5_CH)
        fi = (si + apr[None] * h0i + api[None] * h0r).reshape(tc, S5_CH)
        xre[...] = fr
        xim[...] = fi
        hre_out[...] = _dot_exact01(pick, fr)
        him_out[...] = _dot_exact01(pick, fi)

    ys = []
    for hf in range(2):
        cols = slice(hf * S5_HALF_CH, (hf + 1) * S5_HALF_CH)
        ys.append(_dot(xre[:, cols].astype(BF16), cre_ref[hf]) + _dot(xim[:, cols].astype(BF16), cim_ref[hf]))
    y = jnp.concatenate(ys, axis=1) + d_ref[...] * u
    z = jax.nn.gelu(y)
    _store_rows(out_ref, z * jax.nn.sigmoid(_dot(z.astype(BF16), wglu_ref[...])))


def _s5_call(p, row0, n_seq, seq_len, h0re, h0im, prm, chained, tc, out_rows=None):
    n_rows = n_seq * seq_len
    blk0 = row0 // tc
    assert row0 % tc == 0 and n_rows % tc == 0
    out_blk = tc if out_rows is None else out_rows
    assert out_rows is None or n_rows == tc
    consts = [prm["bcat"], prm["cre"], prm["cim"], prm["ast_re"], prm["ast_im"], prm["apw_re"], prm["apw_im"],
              prm["d"], prm["wglu"]]
    const_specs = [_const_spec(c.shape, (0,) * c.ndim) for c in consts]
    scratch = [pltpu.VMEM((tc, S5_CH), F32), pltpu.VMEM((tc, S5_CH), F32)]
    if chained:
        n_chunk = seq_len // tc
        grid = (n_seq, n_chunk)
        u_spec = pl.BlockSpec((tc, S5_WIDTH), lambda b, c: (blk0 + b * n_chunk + c, 0))
        h_specs = [_const_spec((1, S5_CH), (0, 0))] * 2
        out_specs = [pl.BlockSpec((out_blk, S5_WIDTH), lambda b, c: (b * n_chunk + c, 0)),
                     pl.BlockSpec((None, 1, S5_CH), lambda b, c: (b, 0, 0)),
                     pl.BlockSpec((None, 1, S5_CH), lambda b, c: (b, 0, 0))]
        out_shape = [jax.ShapeDtypeStruct((n_rows // tc * out_blk, S5_WIDTH), BF16),
                     jax.ShapeDtypeStruct((n_seq, 1, S5_CH), F32), jax.ShapeDtypeStruct((n_seq, 1, S5_CH), F32)]
        scratch += [pltpu.VMEM((1, S5_CH), F32), pltpu.VMEM((1, S5_CH), F32)]
        sem = ("arbitrary", "arbitrary")
    else:
        assert seq_len == SUBLANES
        grid = (n_rows // tc,)
        u_spec = pl.BlockSpec((tc, S5_WIDTH), lambda i: (blk0 + i, 0))
        nb = tc // SUBLANES
        h_specs = [pl.BlockSpec((nb, S5_CH), lambda i: (i, 0))] * 2
        out_specs = [pl.BlockSpec((tc, S5_WIDTH), lambda i: (i, 0)),
                     pl.BlockSpec((nb, S5_CH), lambda i: (i, 0)), pl.BlockSpec((nb, S5_CH), lambda i: (i, 0))]
        out_shape = [jax.ShapeDtypeStruct((n_rows, S5_WIDTH), BF16),
                     jax.ShapeDtypeStruct((n_seq, S5_CH), F32), jax.ShapeDtypeStruct((n_seq, S5_CH), F32)]
        sem = ("arbitrary",)
    return pl.pallas_call(
        functools.partial(_s5_kernel, chained, tc),
        grid=grid,
        in_specs=[u_spec, *h_specs, *const_specs],
        out_specs=out_specs,
        out_shape=out_shape,
        scratch_shapes=scratch,
        compiler_params=_cparams(sem),
        name="s5",
    )(p, h0re, h0im, *consts)


S5_SLABS = S5_WIDTH // LANES


def _s5_batch_kernel(n_seq, tt, *refs):
    u_refs = refs[:n_seq]
    (h0re_ref, h0im_ref, bcat_ref, cre_ref, cim_ref, are_ref, aim_ref, d_ref, wglu_ref,
     out_ref, hre_out, him_out, u_tb, o_tb, xre, xim, cre, cim) = refs[n_seq:]
    @pl.when(pl.program_id(0) == 0)
    def _():
        cre[...] = jnp.broadcast_to(h0re_ref[...], (n_seq, S5_CH))
        cim[...] = jnp.broadcast_to(h0im_ref[...], (n_seq, S5_CH))

    for b in range(n_seq):
        ub = u_refs[b][...]
        for sl in range(S5_SLABS):
            u_tb[sl, pl.ds(b, tt, stride=n_seq), :] = ub[:, sl * LANES:(sl + 1) * LANES]
    u = jnp.concatenate([u_tb[sl] for sl in range(S5_SLABS)], axis=1)
    ubf = u.astype(BF16)
    for hf in range(2):
        xh = _dot(ubf[:, hf * S5_HALF_IN:(hf + 1) * S5_HALF_IN], bcat_ref[hf])
        xre[:, hf * S5_HALF_CH:(hf + 1) * S5_HALF_CH] = xh[:, :S5_HALF_CH]
        xim[:, hf * S5_HALF_CH:(hf + 1) * S5_HALF_CH] = xh[:, S5_HALF_CH:]

    for hf in range(2):
        cols = slice(hf * S5_HALF_CH, (hf + 1) * S5_HALF_CH)
        ar = are_ref[:, cols]
        ai = aim_ref[:, cols]

        def step(t, c):
            hr, hi = c
            rw = pl.ds(pl.multiple_of(t * n_seq, n_seq), n_seq)
            nr = xre[rw, cols] + (ar * hr - ai * hi)
            ni = xim[rw, cols] + (ar * hi + ai * hr)
            xre[rw, cols] = nr
            xim[rw, cols] = ni
            return nr, ni

        hr, hi = lax.fori_loop(0, tt, step, (cre[:, cols], cim[:, cols]), unroll=tt)
        cre[:, cols] = hr
        cim[:, cols] = hi
        hre_out[:, cols] = hr
        him_out[:, cols] = hi

    ys = []
    for hf in range(2):
        cols = slice(hf * S5_HALF_CH, (hf + 1) * S5_HALF_CH)
        ys.append(_dot(xre[:, cols].astype(BF16), cre_ref[hf]) + _dot(xim[:, cols].astype(BF16), cim_ref[hf]))
    y = jnp.concatenate(ys, axis=1) + d_ref[...] * u
    z = jax.nn.gelu(y)
    o = z * jax.nn.sigmoid(_dot(z.astype(BF16), wglu_ref[...]))
    for sl in range(S5_SLABS):
        o_tb[sl] = o[:, sl * LANES:(sl + 1) * LANES]
    for b in range(n_seq):
        ob = jnp.concatenate([o_tb[sl, pl.ds(b, tt, stride=n_seq), :] for sl in range(S5_SLABS)], axis=1)
        out_ref[b] = ob.astype(out_ref.dtype)


def _s5_batch_call(p, row0, n_seq, seq_len, h0re, h0im, prm, tt):
    assert n_seq == SUBLANES and seq_len % tt == 0 and row0 % tt == 0
    n_chunk = seq_len // tt
    blk0 = row0 // tt
    a8 = lambda a: jnp.broadcast_to(a[:1], (SUBLANES, S5_CH))
    consts = [prm["bcat"], prm["cre"], prm["cim"], a8(prm["apw_re"]), a8(prm["apw_im"]), prm["d"], prm["wglu"]]
    const_specs = [_const_spec(c.shape, (0,) * c.ndim) for c in consts]
    u_specs = [pl.BlockSpec((tt, S5_WIDTH), functools.partial(lambda b, c: (blk0 + b * n_chunk + c, 0), b))
               for b in range(n_seq)]
    rows = n_seq * tt
    out, hre, him = pl.pallas_call(
        functools.partial(_s5_batch_kernel, n_seq, tt),
        grid=(n_chunk,),
        in_specs=[*u_specs, _const_spec((1, S5_CH), (0, 0)), _const_spec((1, S5_CH), (0, 0)), *const_specs],
        out_specs=[pl.BlockSpec((n_seq, tt, S5_WIDTH), lambda c: (0, c, 0)),
                   pl.BlockSpec((n_seq, S5_CH), lambda c: (0, 0)), pl.BlockSpec((n_seq, S5_CH), lambda c: (0, 0))],
        out_shape=[jax.ShapeDtypeStruct((n_seq, seq_len, S5_WIDTH), BF16),
                   jax.ShapeDtypeStruct((n_seq, S5_CH), F32), jax.ShapeDtypeStruct((n_seq, S5_CH), F32)],
        scratch_shapes=[pltpu.VMEM((S5_SLABS, rows, LANES), F32), pltpu.VMEM((S5_SLABS, rows, LANES), F32),
                        pltpu.VMEM((rows, S5_CH), F32), pltpu.VMEM((rows, S5_CH), F32),
                        pltpu.VMEM((n_seq, S5_CH), F32), pltpu.VMEM((n_seq, S5_CH), F32)],
        compiler_params=_cparams(("arbitrary",)),
        name="s5_batch",
    )(*([p] * n_seq), h0re, h0im, *consts)
    return out.reshape(n_seq * seq_len, S5_WIDTH), hre, him


def _s5_params(a_re, a_im, log_dt, b_re, b_im, c_re, c_im, d_skip, w_glu):
    dt = jnp.exp(log_dt)[:, None]
    mag = jnp.exp(dt * a_re)
    abar_re, abar_im = mag * jnp.cos(dt * a_im), mag * jnp.sin(dt * a_im)
    den = a_re * a_re + a_im * a_im
    num_re = abar_re - 1.0
    f_re = (num_re * a_re + abar_im * a_im) / den
    f_im = (abar_im * a_re - num_re * a_im) / den
    bbar_re = f_re[..., None] * b_re - f_im[..., None] * b_im
    bbar_im = f_re[..., None] * b_im + f_im[..., None] * b_re

    def block_diag_in(w):
        w = w.reshape(2, S5_GROUPS // 2, S5_STATE, S5_GROUP)
        eye = jnp.eye(S5_GROUPS // 2, dtype=F32)
        return jnp.einsum("hgpn,gk->hgnkp", w, eye).reshape(2, S5_HALF_IN, S5_HALF_CH)

    def block_diag_out(w):
        w = w.reshape(2, S5_GROUPS // 2, S5_GROUP, S5_STATE)
        eye = jnp.eye(S5_GROUPS // 2, dtype=F32)
        return jnp.einsum("hgnp,gk->hgpkn", w, eye).reshape(2, S5_HALF_CH, S5_HALF_IN)

    bcat = jnp.concatenate([block_diag_in(bbar_re), block_diag_in(bbar_im)], axis=-1).astype(BF16)
    ar, ai = abar_re.reshape(1, S5_CH), abar_im.reshape(1, S5_CH)
    pows_re, pows_im = [ar], [ai]
    for _ in range(SUBLANES - 1):
        pr, pi = pows_re[-1], pows_im[-1]
        pows_re.append(pr * ar - pi * ai)
        pows_im.append(pr * ai + pi * ar)
    zeros = jnp.zeros((SUBLANES - 3, S5_CH), F32)
    return dict(
        bcat=bcat,
        cre=block_diag_out(c_re).astype(BF16),
        cim=block_diag_out(-c_im).astype(BF16),
        ast_re=jnp.concatenate([pows_re[0], pows_re[1], pows_re[3], zeros], axis=0),
        ast_im=jnp.concatenate([pows_im[0], pows_im[1], pows_im[3], zeros], axis=0),
        apw_re=jnp.concatenate(pows_re, axis=0),
        apw_im=jnp.concatenate(pows_im, axis=0),
        d=d_skip.reshape(1, S5_WIDTH),
        wglu=w_glu.astype(BF16),
    )


def _ret_tables(t):
    gam = np.log(1.0 - 2.0 ** (-5.0 - np.arange(RET_HEADS, dtype=np.float64)))
    tt = np.arange(t, dtype=np.float64)
    dq = np.exp(gam[:, None] * (tt[None, :] + 1.0))
    dq = np.broadcast_to(dq.reshape(RET_HEADS * t, 1), (RET_HEADS * t, RET_DV))
    diff = tt[:, None] - tt[None, :]
    dm = np.where(diff >= 0, np.exp(gam[:, None, None] * np.maximum(diff, 0.0)[None]), 0.0)
    dk = np.exp(gam[:, None] * (t - 1.0 - tt[None, :]))
    dk = np.repeat(dk.T, RET_DK, axis=1)
    ds = np.repeat(np.exp(gam * t), RET_DK)[:, None] * np.ones((1, RET_DV))
    hm = np.repeat(np.eye(RET_HEADS), RET_DK, axis=1)
    f = lambda a: jnp.asarray(np.ascontiguousarray(a), F32)
    return dict(dq=f(dq), dm=f(dm.reshape(RET_HEADS * t, t)), dk=f(dk), ds=f(ds),
                hm=f(np.concatenate([hm, np.zeros((SUBLANES - RET_HEADS, RET_QK))], axis=0)))


def _rope_tables(pos):
    half = RET_DK // 2
    inv_freq = 1.0 / (ROPE_BASE ** (jnp.arange(half, dtype=F32) / half))
    ang = pos.astype(F32)[:, None] * inv_freq[None, :]
    cos, sin = jnp.cos(ang), jnp.sin(ang)
    zero = jnp.zeros_like(sin)
    tile = lambda a, b: jnp.tile(jnp.concatenate([a, b], axis=1), (1, RET_HEADS))
    return tile(cos, cos), tile(-sin, zero), tile(zero, sin)


def _ret_chunk(t, q, k, v, g, s, cos, s_up, s_dn, dq, dm, dk, ds, hm):
    half = RET_DK // 2

    def rope(x):
        return x * cos + pltpu.roll(x, RET_QK - half, 1) * s_up + pltpu.roll(x, half, 1) * s_dn

    qr = rope(q)
    kr = rope(k) * (RET_DK ** -0.5)
    qs = jnp.concatenate([qr * hm[h:h + 1, :] for h in range(RET_HEADS)], axis=0).astype(BF16)
    inter = _dot(qs, s.astype(BF16)) * dq
    prob = (_dot_nt(qs, kr.astype(BF16)) * dm).astype(BF16)
    vb = v.astype(BF16)
    outs = []
    for h in range(RET_HEADS):
        rows = slice(h * t, (h + 1) * t)
        cols = slice(h * RET_DV, (h + 1) * RET_DV)
        o = inter[rows] + _dot(prob[rows], vb[:, cols])
        mu = jnp.mean(o, axis=-1, keepdims=True)
        oc = o - mu
        var = jnp.mean(oc * oc, axis=-1, keepdims=True)
        gh = g[:, cols]
        outs.append(oc * lax.rsqrt(var + EPS) * (gh * jax.nn.sigmoid(gh)))
    kd = kr * dk
    ks = jnp.concatenate([kd * hm[h:h + 1, :] for h in range(RET_HEADS)], axis=0).astype(BF16)
    vs = jnp.concatenate([vb[:, h * RET_DV:(h + 1) * RET_DV] for h in range(RET_HEADS)], axis=0)
    s_new = s * ds + _dot_tn(ks, vs)
    return jnp.concatenate(outs, axis=1), s_new


def _ret_kernel(chained, t, n_sub, *refs):
    n_data = 4 * n_sub if chained else 4
    data = refs[:n_data]
    s0_ref, cos_ref, sup_ref, sdn_ref, dq_ref, dm_ref, dk_ref, ds_ref, hm_ref = refs[n_data:n_data + 9]
    rest = refs[n_data + 9:]
    out_ref, st_ref, *scratch = rest[-3:] if chained else rest[-2:]
    tabs = (dq_ref[...], dm_ref[...], dk_ref[...], ds_ref[...], hm_ref[...])
    if chained:
        (s_scr,) = scratch

        @pl.when(pl.program_id(1) == 0)
        def _():
            for s in range(n_sub):
                s_scr[s] = s0_ref[...]

        for s in range(n_sub):
            q_ref, k_ref, v_ref, g_ref = data[4 * s:4 * s + 4]
            o, s_new = _ret_chunk(t, q_ref[...], k_ref[...], v_ref[...], g_ref[...], s_scr[s],
                                  cos_ref[...], sup_ref[...], sdn_ref[...], *tabs)
            _store_rows(out_ref.at[s], o)
            s_scr[s] = s_new
            st_ref[s] = s_new
    else:
        q_ref, k_ref, v_ref, g_ref = data

        def one(i, carry):
            rows = pl.ds(pl.multiple_of(i * t, t), t)
            o, s_new = _ret_chunk(t, q_ref[rows, :], k_ref[rows, :], v_ref[rows, :], g_ref[rows, :], s0_ref[i],
                                  cos_ref[...], sup_ref[...], sdn_ref[...], *tabs)
            out_ref[rows, :] = o.astype(out_ref.dtype)
            st_ref[i] = s_new
            return carry

        lax.fori_loop(0, n_sub, one, 0, unroll=8)


def _stacked_state_io(n_sub, sd, layer_i, st_prev, n_in):
    spec = pl.BlockSpec((None, n_sub, *sd), lambda i: (layer_i, i, 0, 0))
    if st_prev is None:
        return spec, [], [], {}
    return spec, [pl.BlockSpec(memory_space=pl.ANY)], [st_prev], {n_in: 1}


def _ret_call(p, row0, n_seq, seq_len, s0, pos0, chained, t, n_sub=1, out_rows=None, layer_i=0, st_prev=None):
    n_rows = n_seq * seq_len
    rows_blk = t if chained else t * n_sub
    blk0 = row0 // rows_blk
    assert row0 % rows_blk == 0
    out_blk = rows_blk if out_rows is None else out_rows
    assert out_rows is None or n_rows == rows_blk
    tabs = _ret_tables(t)
    consts = [tabs[n] for n in ("dq", "dm", "dk", "ds", "hm")]
    const_specs = [_const_spec(c.shape, (0, 0)) for c in consts]
    cos, s_up, s_dn = _rope_tables(pos0 + jnp.arange(seq_len, dtype=jnp.int32))
    sd = RET_QK, RET_DV
    if chained:
        assert n_seq % n_sub == 0
        n_chunk = seq_len // t
        grid = (n_seq // n_sub, n_chunk)
        rmap = lambda s, cb: (lambda b, c: (blk0 + (b * n_sub + s) * n_chunk + c, cb))
        data_specs = []
        for s in range(n_sub):
            data_specs += [pl.BlockSpec((t, RET_QK), rmap(s, 2)), pl.BlockSpec((t, RET_QK), rmap(s, 3)),
                           pl.BlockSpec((t, RET_WIDTH), rmap(s, 2)), pl.BlockSpec((t, RET_WIDTH), rmap(s, 3))]
        data_specs.append(_const_spec(sd, (0, 0)))
        rope_specs = [pl.BlockSpec((t, RET_QK), lambda b, c: (c, 0))] * 3
        out_specs = [pl.BlockSpec((n_sub, out_blk, RET_WIDTH), lambda b, c: (b, c, 0)),
                     pl.BlockSpec((n_sub, *sd), lambda b, c: (b, 0, 0))]
        out_shape = [jax.ShapeDtypeStruct((n_seq, n_chunk * out_blk, RET_WIDTH), BF16),
                     jax.ShapeDtypeStruct((n_seq, *sd), F32)]
        extra_specs, extra_args, aliases = [], [], {}
        scratch = [pltpu.VMEM((n_sub, *sd), F32)]
        sem = ("arbitrary", "arbitrary")
        n_p = 4 * n_sub
    else:
        assert seq_len == t and n_seq % n_sub == 0
        grid = (n_seq // n_sub,)
        rmap = lambda cb: (lambda i: (blk0 + i, cb))
        st_spec, extra_specs, extra_args, aliases = _stacked_state_io(n_sub, sd, layer_i, st_prev, 13)
        data_specs = [pl.BlockSpec((rows_blk, RET_QK), rmap(2)), pl.BlockSpec((rows_blk, RET_QK), rmap(3)),
                      pl.BlockSpec((rows_blk, RET_WIDTH), rmap(2)), pl.BlockSpec((rows_blk, RET_WIDTH), rmap(3)),
                      st_spec]
        rope_specs = [_const_spec((t, RET_QK), (0, 0))] * 3
        out_specs = [pl.BlockSpec((rows_blk, RET_WIDTH), lambda i: (i, 0)), st_spec]
        out_shape = [jax.ShapeDtypeStruct((n_rows, RET_WIDTH), BF16), jax.ShapeDtypeStruct(s0.shape, F32)]
        scratch = []
        sem = ("arbitrary",)
        n_p = 4
    in_specs = [*data_specs, *rope_specs, *const_specs, *extra_specs]
    assert not aliases or list(aliases) == [len(in_specs) - 1]
    out, st = pl.pallas_call(
        functools.partial(_ret_kernel, chained, t, n_sub),
        grid=grid,
        in_specs=in_specs,
        out_specs=out_specs,
        out_shape=out_shape,
        scratch_shapes=scratch,
        input_output_aliases=aliases,
        compiler_params=_cparams(sem),
        name="retention",
    )(*([p] * n_p), s0, cos, s_up, s_dn, *consts, *extra_args)
    return out.reshape(-1, RET_WIDTH), st


def _gla_tables(t, n_blk):
    cg = t * n_blk
    blk = np.arange(cg) // t
    same = blk[:, None] == blk[None, :]
    cum = (same & (np.arange(cg)[:, None] >= np.arange(cg)[None, :])).astype(np.float32)
    ones = (np.arange(t * GLA_DK)[:, None] // GLA_DK == np.arange(LANES)[None, :]).astype(np.float32)
    return jnp.asarray(cum, BF16), jnp.asarray(ones, BF16)


LOG2E = 1.4426950408889634


def _gla_kernel(chained, t, n_blk, n_par, *refs):
    data = [refs[5 * s:5 * s + 5] for s in range(n_par)]
    s0_ref, wa_ref, ba_ref, ng_ref, cum_ref, ones_ref = refs[5 * n_par:5 * n_par + 6]
    rest = refs[5 * n_par + 6:]
    out_ref, st_ref, rt_scr, o_scr, *scratch = rest[-5:] if chained else rest[-4:]
    cg = t * n_blk
    nh, dk, dv = GLA_HEADS, GLA_DK, GLA_DV
    par = range(n_par)
    if chained:
        (s_scr,) = scratch

        @pl.when(pl.program_id(1) == 0)
        def _():
            for s in par:
                s_scr[s] = s0_ref[...]

    def pad_rows(x):
        return jnp.concatenate([x, jnp.zeros((LANES - x.shape[0], x.shape[1]), x.dtype)], axis=0)

    q, ksc, vb, b, qe, ke, last = [], [], [], [], [], [], []
    for s in par:
        q_ref, k_ref, v_ref, _, lr_ref = data[s]
        q.append(q_ref[...])
        ksc.append(k_ref[...] * (dk ** -0.5))
        vb.append(v_ref[...].astype(BF16))
        la = jax.nn.log_sigmoid(_dot(lr_ref[...].astype(BF16), wa_ref[...]) + ba_ref[...]) / GLA_TAU
        b.append(_dot_exact01(cum_ref[...], la) * LOG2E)
        last3 = b[s].reshape(n_blk, t, GLA_QK)[:, t - 1:t, :]
        bl = jnp.broadcast_to(last3, (n_blk, t, GLA_QK)).reshape(cg, GLA_QK)
        last.append(last3.reshape(n_blk, GLA_QK))
        qe.append(q[s] * jnp.exp2(b[s]))
        ke.append(ksc[s] * jnp.exp2(bl - b[s]))

    row_t = {lo: lo + lax.broadcasted_iota(jnp.int32, (t - lo, GLA_QK), 0) for lo in range(0, t, SUBLANES)}
    for j in range(n_blk):
        rows = slice(j * t, (j + 1) * t)
        for s in par:
            qj, kj, bj = q[s][rows], ksc[s][rows], b[s][rows]
            for i in range(t):
                lo = i // SUBLANES * SUBLANES
                e = jnp.exp2(jnp.where(row_t[lo] >= i, bj[lo:] - bj[i:i + 1, :], NEG_BIG))
                prod = (qj[lo:] * kj[i:i + 1, :]) * e
                for h in range(nh):
                    ph = prod[:, h * dk:(h + 1) * dk]
                    if lo:
                        ph = jnp.concatenate([jnp.zeros((lo, dk), F32), ph], axis=0)
                    rt_scr[s, (j * nh + h) * t:(j * nh + h + 1) * t, i * dk:(i + 1) * dk] = ph.astype(rt_scr.dtype)
    scores = [_dot(rt_scr[s].astype(BF16), ones_ref[...]) for s in par]

    inter, off = [], []
    if chained:
        for s in par:
            pref = [jnp.zeros((1, GLA_QK), F32)]
            for i in range(n_blk):
                pref.append(pref[i] + last[s][i:i + 1, :])
            expand = lambda rs: jnp.concatenate([jnp.broadcast_to(r, (t, GLA_QK)) for r in rs], axis=0)
            qhat = qe[s] * jnp.exp2(expand(pref[:n_blk]))
            khat = ke[s] * jnp.exp2(pref[n_blk] - expand(pref[1:]))
            a_col = jnp.exp2(pad_rows(pref[n_blk])).T[:, 0:1]
            qhat, khat = qhat.astype(BF16), khat.astype(BF16)
            per_head = []
            for h in range(nh):
                hk = slice(h * dk, (h + 1) * dk)
                state = s_scr[s, hk, :]
                per_head.append(_dot(qhat[:, hk], state.astype(BF16)))
                state = state * a_col[hk] + _dot_tn(khat[:, hk], vb[s][:, h * dv:(h + 1) * dv])
                s_scr[s, hk, :] = state
                st_ref[s, hk, :] = state
            inter.append(per_head)
            per_blk = [None]
            for i in range(1, n_blk):
                parts = [ke[s][j * t:(j + 1) * t] * jnp.exp2(pref[i] - pref[j + 1]) for j in range(i)]
                rhs = pad_rows(jnp.concatenate(parts, axis=0)).astype(BF16)
                qi = qe[s][i * t:(i + 1) * t].astype(BF16)
                per_blk.append([_dot_nt(qi[:, h * dk:(h + 1) * dk], rhs[:, h * dk:(h + 1) * dk])
                                for h in range(nh)])
            off.append(per_blk)
    else:
        a_cols = jnp.exp2(pad_rows(last[0])).T
        qb, kb = qe[0].astype(BF16), ke[0].astype(BF16)
        for j in range(n_blk):
            rows = slice(j * t, (j + 1) * t)
            for h in range(nh):
                hk = slice(h * dk, (h + 1) * dk)
                cols = slice(h * dv, (h + 1) * dv)
                state = s0_ref[j, hk, :]
                o_scr[0, rows, cols] = _dot(qb[rows, hk], state.astype(BF16))
                st_ref[j, hk, :] = state * a_cols[hk, j:j + 1] + _dot_tn(kb[rows, hk], vb[0][rows, cols])
    for s in par:
        r = data[s][3][...]
        gated = []
        for h in range(nh):
            pieces = []
            for j in range(n_blk):
                piece = scores[s][(j * nh + h) * t:(j * nh + h + 1) * t, :]
                piece = pltpu.roll(piece, j * t, 1) if j else piece
                if chained and j:
                    piece = piece + off[s][j][h]
                pieces.append(piece)
            pfull = jnp.concatenate(pieces, axis=0) if n_blk > 1 else pieces[0]
            cols = slice(h * dv, (h + 1) * dv)
            o_inter = inter[s][h] if chained else o_scr[s, :, cols]
            oh = o_inter + _dot(pfull[:, :cg].astype(BF16), vb[s][:, cols])
            on = oh * lax.rsqrt(jnp.mean(oh * oh, axis=-1, keepdims=True) + EPS) * ng_ref[...]
            rh = r[:, cols]
            gated.append(on * (rh * jax.nn.sigmoid(rh)))
        _store_rows(out_ref.at[s] if chained else out_ref, jnp.concatenate(gated, axis=1))


def _gla_call(p, row0, n_seq, seq_len, s0, wa, ba, ng, chained, t, n_blk, n_par=1, out_rows=None, layer_i=0,
              st_prev=None):
    cg = t * n_blk
    blk0 = row0 // cg
    assert row0 % cg == 0 and cg <= LANES
    out_blk = cg if out_rows is None else out_rows
    assert out_rows is None or seq_len == cg
    cum, ones = _gla_tables(t, n_blk)
    consts = [wa, ba, ng, cum, ones]
    const_specs = [_const_spec(c.shape, (0,) * c.ndim) for c in consts]
    sd = GLA_QK, GLA_DV
    lr_col = (2 * GLA_QK + 2 * GLA_V) // LANES
    rt_dtype = BF16 if t % (2 * SUBLANES) == 0 else F32
    if chained:
        assert n_seq % n_par == 0
        n_chunk = seq_len // cg
        grid = (n_seq // n_par, n_chunk)
        rmap = lambda s, cb: (lambda b, c: (blk0 + (b * n_par + s) * n_chunk + c, cb))
        s_spec = _const_spec(sd, (0, 0))
        out_specs = [pl.BlockSpec((n_par, out_blk, GLA_V), lambda b, c: (b, c, 0)),
                     pl.BlockSpec((n_par, *sd), lambda b, c: (b, 0, 0))]
        out_shape = [jax.ShapeDtypeStruct((n_seq, n_chunk * out_blk, GLA_V), BF16),
                     jax.ShapeDtypeStruct((n_seq, *sd), F32)]
        extra_specs, extra_args, aliases = [], [], {}
        scratch = [pltpu.VMEM((n_par, *sd), F32)]
        sem = ("arbitrary", "arbitrary")
    else:
        assert seq_len == t and n_seq % n_blk == 0 and n_par == 1
        grid = (n_seq // n_blk,)
        rmap = lambda s, cb: (lambda i: (blk0 + i, cb))
        s_spec, extra_specs, extra_args, aliases = _stacked_state_io(n_blk, sd, layer_i, st_prev, 11)
        out_specs = [pl.BlockSpec((cg, GLA_V), lambda i: (i, 0)), s_spec]
        out_shape = [jax.ShapeDtypeStruct((n_seq * seq_len, GLA_V), BF16), jax.ShapeDtypeStruct(s0.shape, F32)]
        scratch = []
        sem = ("arbitrary",)
    data_specs = []
    for s in range(n_par):
        data_specs += [pl.BlockSpec((cg, GLA_QK), rmap(s, 0)), pl.BlockSpec((cg, GLA_QK), rmap(s, 1)),
                       pl.BlockSpec((cg, GLA_V), rmap(s, 1)), pl.BlockSpec((cg, GLA_V), rmap(s, 2)),
                       pl.BlockSpec((cg, LANES), rmap(s, lr_col))]
    in_specs = [*data_specs, s_spec, *const_specs, *extra_specs]
    assert not aliases or list(aliases) == [len(in_specs) - 1]
    out, st = pl.pallas_call(
        functools.partial(_gla_kernel, chained, t, n_blk, n_par),
        grid=grid,
        in_specs=in_specs,
        out_specs=out_specs,
        out_shape=out_shape,
        scratch_shapes=[pltpu.VMEM((n_par, n_blk * GLA_HEADS * t, t * GLA_DK), rt_dtype),
                        pltpu.VMEM((n_par, cg, GLA_V), F32), *scratch],
        input_output_aliases=aliases,
        compiler_params=_cparams(sem),
        name="gla",
    )(*([p] * (5 * n_par)), s0, *consts, *extra_args)
    return out.reshape(-1, GLA_V), st


S5_TT_PROMPT = 128
S5_TC_SAMPLE = 256
RET_T_PROMPT = 512
RET_PAR_PROMPT = 1
RET_SUB_SAMPLE = 16
GLA_T_PROMPT = 16
GLA_BLK_PROMPT = 8
GLA_BLK_SAMPLE = 8
GLA_PAR_PROMPT = 2


def _mixer_ab(p, i, prm, st_s5_re, st_s5_im, st_ret, s_ret_prev):
    zeros_h = jnp.zeros((1, S5_CH), F32)
    a_m, hre_m, him_m = _s5_call(p, ROW0_META, 1, N_META, zeros_h, zeros_h, prm, True, N_META, out_rows=DENSE_TM)
    a_p, hre_p, him_p = _s5_batch_call(p, 0, BATCH, SEQ, hre_m[0], him_m[0], prm, S5_TT_PROMPT)
    a_s, hre_s, him_s = _s5_call(p, ROW0_SAMPLE, DEC_BATCH, DEC_SEQ, st_s5_re[i].reshape(DEC_BATCH, S5_CH),
                                 st_s5_im[i].reshape(DEC_BATCH, S5_CH), prm, False, S5_TC_SAMPLE)
    last = lambda h: h.reshape(DEC_BATCH, S5_GROUPS, S5_STATE)

    zeros_s = jnp.zeros((RET_QK, RET_DV), F32)
    b_m, s_m = _ret_call(p, ROW0_META, 1, N_META, zeros_s, 0, True, N_META, out_rows=DENSE_TM)
    b_p, s_p = _ret_call(p, 0, BATCH, SEQ, s_m[0], N_META, True, RET_T_PROMPT, n_sub=RET_PAR_PROMPT)
    b_s, s_ret = _ret_call(p, ROW0_SAMPLE, DEC_BATCH, DEC_SEQ, st_ret, PAST_LEN, False, DEC_SEQ, RET_SUB_SAMPLE,
                           layer_i=i, st_prev=s_ret_prev)
    states = dict(
        p_s5_re=hre_p.reshape(BATCH, S5_GROUPS, S5_STATE), p_s5_im=him_p.reshape(BATCH, S5_GROUPS, S5_STATE),
        p_ret=s_p.reshape(BATCH, RET_HEADS, RET_DK, RET_DV), s_s5_re=last(hre_s), s_s5_im=last(him_s))
    return [(a_p, a_s, a_m), (b_p, b_s, b_m)], states, s_ret


def _mixer_gla(p, i, wa, ba, ng, st_gla, s_gla_prev):
    zeros_s = jnp.zeros((GLA_QK, GLA_DV), F32)
    o_m, s_m = _gla_call(p, ROW0_META, 1, N_META, zeros_s, wa, ba, ng, True, N_META, 1, out_rows=DENSE_TM)
    o_p, s_p = _gla_call(p, 0, BATCH, SEQ, s_m[0], wa, ba, ng, True, GLA_T_PROMPT, GLA_BLK_PROMPT,
                         n_par=GLA_PAR_PROMPT)
    o_s, s_gla = _gla_call(p, ROW0_SAMPLE, DEC_BATCH, DEC_SEQ, st_gla, wa, ba, ng, False, DEC_SEQ, GLA_BLK_SAMPLE,
                           layer_i=i, st_prev=s_gla_prev)
    states = dict(p_gla=s_p.reshape(BATCH, GLA_HEADS, GLA_DK, GLA_DV))
    return [(o_p, o_s, o_m)], states, s_gla


def kernel(x_prompt, x_sample, state_s5_re, state_s5_im, state_ret, state_gla, meta_tokens, norm_ffn1, norm_mix,
           norm_ffn2, norm_final, ffn1_w_gu, ffn1_w_down, ffn2_w_gu, ffn2_w_down, ab_w_in, ab_w_out, s5_a_re,
           s5_a_im, s5_log_dt, s5_b_re, s5_b_im, s5_c_re, s5_c_im, s5_d, s5_w_glu, gla_w_in, gla_w_alpha2,
           gla_b_alpha, gla_norm, gla_w_out):
    ffn1_gu, ffn1_dn, ffn2_gu, ffn2_dn = ffn1_w_gu, ffn1_w_down, ffn2_w_gu, ffn2_w_down
    ab_in, ab_out, gla_in, gla_out = ab_w_in, ab_w_out, gla_w_in, gla_w_out
    gla_wa = jnp.pad(gla_w_alpha2, ((0, 0), (0, LANES - GLA_LOWRANK), (0, 0))).astype(BF16)
    n1 = norm_ffn1.reshape(DEPTH, 1, D_MODEL)
    nm = norm_mix.reshape(DEPTH, 1, D_MODEL)
    n2 = norm_ffn2.reshape(DEPTH, 1, D_MODEL)

    meta_pad = jnp.pad(meta_tokens.astype(x_prompt.dtype), ((0, DENSE_TM - N_META), (0, 0)))
    xs = [x_prompt.reshape(ROWS_PROMPT, D_MODEL), x_sample.reshape(ROWS_SAMPLE, D_MODEL), meta_pad]
    st_ret = state_ret.reshape(N_EVEN, DEC_BATCH, RET_QK, RET_DV)
    st_gla = state_gla.reshape(N_ODD, DEC_BATCH, GLA_QK, GLA_DV)
    s_ret = s_gla = None
    collected = {}
    for layer in range(DEPTH):
        i = layer // 2
        last = layer == DEPTH - 1
        g_final = norm_final.reshape(1, D_MODEL) if last else None
        if layer % 2 == 0:
            x1, p = _pre_call(xs, n1, ffn1_gu, ffn1_dn, nm, ab_in, layer, i, AB_IN)
            prm = _s5_params(s5_a_re[i], s5_a_im[i], s5_log_dt[i], s5_b_re[i], s5_b_im[i], s5_c_re[i],
                             s5_c_im[i], s5_d[i], s5_w_glu[i])
            mixes, states, s_ret = _mixer_ab(p, i, prm, state_s5_re, state_s5_im, st_ret, s_ret)
            x = _post_call(x1, mixes, ab_out, n2, ffn2_gu, ffn2_dn, layer, i, g_final)
        else:
            x1, p = _pre_call(xs, n1, ffn1_gu, ffn1_dn, nm, gla_in, layer, i, GLA_IN_PAD)
            mixes, states, s_gla = _mixer_gla(p, i, gla_wa[i], gla_b_alpha[i].reshape(1, GLA_QK),
                                              gla_norm[i].reshape(1, GLA_DV), st_gla, s_gla)
            x = _post_call(x1, mixes, gla_out, n2, ffn2_gu, ffn2_dn, layer, i, g_final)
        xs = [x]
        for name, val in states.items():
            collected.setdefault(name, []).append(val)
    out = {name: jnp.stack(vals) for name, vals in collected.items()}
    y_prompt, y_rest = x
    return (y_prompt.reshape(BATCH, SEQ, D_MODEL), y_rest[:ROWS_SAMPLE].reshape(DEC_BATCH, DEC_SEQ, D_MODEL),
            out["p_s5_re"], out["p_s5_im"], out["p_ret"], out["p_gla"], out["s_s5_re"], out["s_s5_im"],
            s_ret.reshape(N_EVEN, DEC_BATCH, RET_HEADS, RET_DK, RET_DV),
            s_gla.reshape(N_ODD, DEC_BATCH, GLA_HEADS, GLA_DK, GLA_DV))
```
